```python
import math
import jax, jax.numpy as jnp
from jax import lax
import numpy as np

D_MODEL = 1024
BATCH = 8
SEQ = 4096
DEPTH = 2

RW_HEADS = 8
RW_HEAD_DIM = 64
RW_WIDTH = RW_HEADS * RW_HEAD_DIM
RW_DECAY_LORA = 64
RW_ICLR_LORA = 64
RW_GATE_LORA = 128
RW_COLS = 3 * RW_WIDTH + RW_DECAY_LORA + RW_ICLR_LORA + RW_GATE_LORA
RW_GN_EPS = 64e-5
DA_HEADS = 4
DA_HEAD_DIM = 64
DA_WIDTH = DA_HEADS * 2 * DA_HEAD_DIM
DA_COLS = 3 * DA_WIDTH
DA_BLOCK = 128
GD_HEADS = 4
GD_HEAD_DIM = 128
GD_WIDTH = GD_HEADS * GD_HEAD_DIM
GD_CONV = 4
GD_CHUNK = 64
GD_COLS = 3 * GD_WIDTH + 2 * GD_HEADS + GD_WIDTH
N_BRANCH = 3
BRANCH_WIDTH = 512
GATE_COLS = N_BRANCH * D_MODEL
IN_COLS = RW_COLS + DA_COLS + GD_COLS + GATE_COLS
D_FF = 2816
ALPHA = (2.0 * DEPTH) ** 0.25
BETA = (8.0 * DEPTH) ** -0.25

kernel_name = "hybrid_rwkv7_diffattn_gdn_macaron_deepnorm"

F32 = jnp.float32


def _layernorm(x, g, b, eps=1e-5):
    xf = x.astype(F32)
    mu = xf.mean(-1, keepdims=True)
    var = jnp.square(xf - mu).mean(-1, keepdims=True)
    return ((xf - mu) * lax.rsqrt(var + eps) * g + b).astype(x.dtype)


def _rmsnorm(x, g, eps):
    xf = x.astype(F32)
    return xf * lax.rsqrt(jnp.mean(jnp.square(xf), -1, keepdims=True) + eps) * g


def _l2norm(x, eps=1e-6):
    xf = x.astype(F32)
    return xf * lax.rsqrt(jnp.sum(jnp.square(xf), -1, keepdims=True) + eps)


def _swiglu(x, w_in, w_out):
    gate, up = jnp.split(x @ w_in, 2, axis=-1)
    return (jax.nn.silu(gate) * up) @ w_out


def _token_shift(z, mu):
    prev = jnp.pad(z, ((0, 0), (1, 0), (0, 0)))[:, :-1]
    return z + mu * (prev - z)


def _causal_dwconv(z, w):
    K = w.shape[0]
    L = z.shape[1]
    zp = jnp.pad(z, ((0, 0), (K - 1, 0), (0, 0)))
    out = zp[:, 0:L] * w[0]
    for j in range(1, K):
        out = out + zp[:, j:j + L] * w[j]
    return out


def _rwkv7(hr, mu, w0, w_up, a0, a_up, g_up, k_k, k_a, r_k, ln_g, ln_b):
    B, L, _ = hr.shape
    H, N, C = RW_HEADS, RW_HEAD_DIM, RW_WIDTH
    hr = _token_shift(hr, mu)
    r = hr[..., 0:C]
    k = hr[..., C:2 * C]
    v = hr[..., 2 * C:3 * C]
    wd = hr[..., 3 * C:3 * C + RW_DECAY_LORA]
    ad = hr[..., 3 * C + RW_DECAY_LORA:3 * C + RW_DECAY_LORA + RW_ICLR_LORA]
    gd = hr[..., 3 * C + RW_DECAY_LORA + RW_ICLR_LORA:]
    heads = lambda t: t.reshape(B, L, H, N)
    logw = -math.exp(-0.5) * jax.nn.sigmoid((w0 + jnp.tanh(wd) @ w_up).astype(F32))
    a = jax.nn.sigmoid((a0 + ad @ a_up).astype(F32))
    g = jax.nn.sigmoid(gd) @ g_up
    r, k, v = heads(r.astype(F32)), heads(k.astype(F32)), heads(v.astype(F32))
    a, w = heads(a), jnp.exp(heads(logw))
    kk = _l2norm(k * k_k.reshape(H, N).astype(F32))
    k = k * (1.0 + (a - 1.0) * k_a.reshape(H, N).astype(F32))

    def step(S, inp):
        r_t, w_t, k_t, v_t, kk_t, a_t = inp
        sk = jnp.einsum('bhvk,bhk->bhv', S, kk_t)
        S = (S * w_t[:, :, None, :] - sk[..., None] * (kk_t * a_t)[:, :, None, :]
             + v_t[..., None] * k_t[:, :, None, :])
        return S, jnp.einsum('bhvk,bhk->bhv', S, r_t)

    tm = lambda t: jnp.moveaxis(t, 1, 0)
    S0 = jnp.zeros((B, H, N, N), F32)
    _, y = lax.scan(step, S0, (tm(r), tm(w), tm(k), tm(v), tm(kk), tm(a)))
    y = jnp.moveaxis(y, 0, 1)
    ym = y.mean(-1, keepdims=True)
    yv = jnp.square(y - ym).mean(-1, keepdims=True)
    y = ((y - ym) * lax.rsqrt(yv + RW_GN_EPS)).reshape(B, L, C) * ln_g + ln_b
    bonus = jnp.sum(r * k * r_k.astype(F32), -1, keepdims=True) * v
    y = (y + bonus.reshape(B, L, C)) * g
    return y.astype(hr.dtype)


def _diff_attention(hd, lam_q1, lam_k1, lam_q2, lam_k2, norm_g, lam_init):
    B, L, _ = hd.shape
    H, d = DA_HEADS, DA_HEAD_DIM
    q = hd[..., 0:DA_WIDTH].astype(F32).reshape(B, L, H, 2, d).transpose(3, 0, 2, 1, 4)
    k = hd[..., DA_WIDTH:2 * DA_WIDTH].astype(F32).reshape(B, L, H, 2, d).transpose(3, 0, 2, 1, 4)
    v = hd[..., 2 * DA_WIDTH:].astype(F32).reshape(B, L, H, 2 * d).transpose(0, 2, 1, 3)
    lam = (jnp.exp(jnp.sum(lam_q1.astype(F32) * lam_k1.astype(F32)))
           - jnp.exp(jnp.sum(lam_q2.astype(F32) * lam_k2.astype(F32))) + lam_init)
    slopes = jnp.exp2(-8.0 * jnp.arange(1, H + 1, dtype=F32) / H)
    scale = d ** -0.5
    nb = L // DA_BLOCK
    qb = jnp.moveaxis(q.reshape(2, B, H, nb, DA_BLOCK, d), 3, 0)
    kpos = jnp.arange(L)

    def block(args):
        q_blk, i = args
        qpos = i * DA_BLOCK + jnp.arange(DA_BLOCK)
        dist = (qpos[:, None] - kpos[None, :]).astype(F32)
        bias = jnp.where(dist >= 0, -slopes[:, None, None] * dist, -jnp.inf)
        s = jnp.einsum('mbhqd,mbhkd->mbhqk', q_blk, k) * scale + bias
        p = jax.nn.softmax(s, axis=-1)
        p = p[0] - lam * p[1]
        return jnp.einsum('bhqk,bhkv->bhqv', p, v)

    o = lax.map(block, (qb, jnp.arange(nb)))
    o = jnp.moveaxis(o, 0, 2).reshape(B, H, L, 2 * d)
    o = _rmsnorm(o, norm_g, 1e-5) * (1.0 - lam_init)
    return o.transpose(0, 2, 1, 3).reshape(B, L, DA_WIDTH).astype(hd.dtype)


def _chunk_gated_delta(q, k, v, g, beta):
    B, H, L, dk = q.shape
    dv = v.shape[-1]
    C = GD_CHUNK
    n = L // C
    q = q.reshape(B, H, n, C, dk)
    k = k.reshape(B, H, n, C, dk)
    v = v.reshape(B, H, n, C, dv)
    g = g.reshape(B, H, n, C)
    beta = beta.reshape(B, H, n, C)
    G = jnp.cumsum(g, axis=-1)
    idx = jnp.arange(C)
    incl = idx[:, None] >= idx[None, :]
    strict = idx[:, None] > idx[None, :]
    decay = jnp.exp(jnp.where(incl, G[..., :, None] - G[..., None, :], -jnp.inf))
    kb = k * beta[..., None]
    Lm = jnp.where(strict, jnp.einsum('bhncd,bhnsd->bhncs', kb, k) * decay, 0.0)
    A = Lm + jnp.eye(C, dtype=Lm.dtype)
    U = lax.linalg.triangular_solve(A, v * beta[..., None], left_side=True, lower=True, unit_diagonal=True)
    W = lax.linalg.triangular_solve(A, kb * jnp.exp(G)[..., None], left_side=True, lower=True, unit_diagonal=True)
    intra = jnp.einsum('bhncd,bhnsd->bhncs', q, k) * decay
    G_last = G[..., -1:]
    qg = q * jnp.exp(G)[..., None]
    kd = k * jnp.exp(G_last - G)[..., None]
    gl = jnp.exp(G_last[..., 0])

    def step(S, inp):
        qg_c, kd_c, u_c, w_c, intra_c, gl_c = inp
        v_new = u_c - jnp.einsum('bhcd,bhdv->bhcv', w_c, S)
        o = jnp.einsum('bhcd,bhdv->bhcv', qg_c, S) + jnp.einsum('bhcs,bhsv->bhcv', intra_c, v_new)
        S = S * gl_c[..., None, None] + jnp.einsum('bhcd,bhcv->bhdv', kd_c, v_new)
        return S, o

    mv = lambda t: jnp.moveaxis(t, 2, 0)
    S0 = jnp.zeros((B, H, dk, dv), F32)
    _, o = lax.scan(step, S0, (mv(qg), mv(kd), mv(U), mv(W), mv(intra), mv(gl)))
    return jnp.moveaxis(o, 0, 2).reshape(B, H, L, dv)


def _gated_deltanet(hg, conv_w, a_log, dt_bias, norm_g):
    B, L, _ = hg.shape
    H, dh, Wd = GD_HEADS, GD_HEAD_DIM, GD_WIDTH
    qkv = jax.nn.silu(_causal_dwconv(hg[..., 0:3 * Wd], conv_w))
    a_lg = hg[..., 3 * Wd:3 * Wd + H].astype(F32)
    b_lg = hg[..., 3 * Wd + H:3 * Wd + 2 * H].astype(F32)
    z = hg[..., 3 * Wd + 2 * H:]
    heads = lambda t: t.reshape(B, L, H, dh).transpose(0, 2, 1, 3).astype(F32)
    q = _l2norm(heads(qkv[..., 0:Wd])) * (dh ** -0.5)
    k = _l2norm(heads(qkv[..., Wd:2 * Wd]))
    v = heads(qkv[..., 2 * Wd:])
    g = -(jnp.exp(a_log.astype(F32)) * jax.nn.softplus(a_lg + dt_bias.astype(F32))).transpose(0, 2, 1)
    beta = jax.nn.sigmoid(b_lg).transpose(0, 2, 1)
    o = _chunk_gated_delta(q, k, v, g, beta)
    o = _rmsnorm(o, norm_g, 1e-6) * jax.nn.silu(heads(z))
    return o.transpose(0, 2, 1, 3).reshape(B, L, Wd).astype(hg.dtype)


def _hybrid_mixer(x, layer, w_in, rw_shift_mu, rw_w0, rw_w_up, rw_a0, rw_a_up, rw_g_up, rw_k_k,
                  rw_k_a, rw_r_k, rw_ln_g, rw_ln_b, da_lam_q1, da_lam_k1, da_lam_q2, da_lam_k2,
                  da_norm_g, gd_conv_w, gd_a_log, gd_dt_bias, gd_norm_g, w_branch, w_out):
    B, L, _ = x.shape
    h = x @ w_in
    o1 = RW_COLS
    o2 = o1 + DA_COLS
    o3 = o2 + GD_COLS
    ya = _rwkv7(h[..., 0:o1], rw_shift_mu, rw_w0, rw_w_up, rw_a0, rw_a_up, rw_g_up,
                rw_k_k, rw_k_a, rw_r_k, rw_ln_g, rw_ln_b)
    lam_init = 0.8 - 0.6 * math.exp(-0.3 * layer)
    yb = _diff_attention(h[..., o1:o2], da_lam_q1, da_lam_k1, da_lam_q2, da_lam_k2, da_norm_g, lam_init)
    yc = _gated_deltanet(h[..., o2:o3], gd_conv_w, gd_a_log, gd_dt_bias, gd_norm_g)
    ys = jnp.stack([ya, yb, yc], axis=2)
    branches = jnp.einsum('blnc,ncd->blnd', ys, w_branch)
    gates = jax.nn.sigmoid(h[..., o3:].reshape(B, L, N_BRANCH, D_MODEL))
    merged = jnp.einsum('blnd,blnd->bld', gates, branches)
    return merged @ w_out


def setup_inputs(seed: int = 0) -> dict:
    key = jax.random.key(seed)
    ks = jax.random.split(key, 40)
    nrm = lambda i, shape, s: jax.random.normal(ks[i], shape, F32) * s
    gain = lambda i, shape: 1.0 + nrm(i, shape, 0.02)
    Dp, D = DEPTH, D_MODEL
    dt = jnp.exp(jax.random.uniform(ks[24], (Dp, GD_HEADS), F32, math.log(1e-3), math.log(1e-1)))
    return {
        "x": nrm(0, (BATCH, SEQ, D), 1.0),
        "ffn1_w_in": nrm(1, (Dp, D, 2 * D_FF), D ** -0.5),
        "ffn1_w_out": nrm(2, (Dp, D_FF, D), BETA * D_FF ** -0.5),
        "ln1_g": gain(3, (Dp, D)),
        "ln1_b": nrm(4, (Dp, D), 0.02),
        "mix_w_in": nrm(5, (Dp, D, IN_COLS), D ** -0.5),
        "rw_shift_mu": jax.random.uniform(ks[6], (Dp, RW_COLS), F32),
        "rw_w0": nrm(7, (Dp, RW_WIDTH), 0.5),
        "rw_w_up": nrm(8, (Dp, RW_DECAY_LORA, RW_WIDTH), 0.1),
        "rw_a0": nrm(9, (Dp, RW_WIDTH), 0.1),
        "rw_a_up": nrm(10, (Dp, RW_ICLR_LORA, RW_WIDTH), 0.1),
        "rw_g_up": nrm(11, (Dp, RW_GATE_LORA, RW_WIDTH), RW_GATE_LORA ** -0.5),
        "rw_k_k": 0.85 + nrm(12, (Dp, RW_WIDTH), 0.02),
        "rw_k_a": gain(13, (Dp, RW_WIDTH)),
        "rw_r_k": nrm(14, (Dp, RW_HEADS, RW_HEAD_DIM), 0.1),
        "rw_ln_g": gain(15, (Dp, RW_WIDTH)),
        "rw_ln_b": nrm(16, (Dp, RW_WIDTH), 0.02),
        "da_lam_q1": nrm(17, (Dp, DA_HEAD_DIM), 0.1),
        "da_lam_k1": nrm(18, (Dp, DA_HEAD_DIM), 0.1),
        "da_lam_q2": nrm(19, (Dp, DA_HEAD_DIM), 0.1),
        "da_lam_k2": nrm(20, (Dp, DA_HEAD_DIM), 0.1),
        "da_norm_g": gain(21, (Dp, 2 * DA_HEAD_DIM)),
        "gd_conv_w": nrm(22, (Dp, GD_CONV, 3 * GD_WIDTH), GD_CONV ** -0.5),
        "gd_a_log": jnp.log(jax.random.uniform(ks[23], (Dp, GD_HEADS), F32, 1.0, 16.0)),
        "gd_dt_bias": dt + jnp.log(-jnp.expm1(-dt)),
        "gd_norm_g": gain(25, (Dp, GD_HEAD_DIM)),
        "mix_w_branch": nrm(26, (Dp, N_BRANCH, BRANCH_WIDTH, D), BETA * BRANCH_WIDTH ** -0.5),
        "mix_w_out": nrm(27, (Dp, D, D), BETA * D ** -0.5),
        "ln2_g": gain(28, (Dp, D)),
        "ln2_b": nrm(29, (Dp, D), 0.02),
        "ffn2_w_in": nrm(30, (Dp, D, 2 * D_FF), D ** -0.5),
        "ffn2_w_out": nrm(31, (Dp, D_FF, D), BETA * D_FF ** -0.5),
        "ln3_g": gain(32, (Dp, D)),
        "ln3_b": nrm(33, (Dp, D), 0.02),
    }


def reference(x, ffn1_w_in, ffn1_w_out, ln1_g, ln1_b, mix_w_in, rw_shift_mu, rw_w0, rw_w_up,
              rw_a0, rw_a_up, rw_g_up, rw_k_k, rw_k_a, rw_r_k, rw_ln_g, rw_ln_b, da_lam_q1,
              da_lam_k1, da_lam_q2, da_lam_k2, da_norm_g, gd_conv_w, gd_a_log, gd_dt_bias,
              gd_norm_g, mix_w_branch, mix_w_out, ln2_g, ln2_b, ffn2_w_in, ffn2_w_out,
              ln3_g, ln3_b):
    for l in range(DEPTH):
        x = _layernorm(ALPHA * x + 0.5 * _swiglu(x, ffn1_w_in[l], ffn1_w_out[l]), ln1_g[l], ln1_b[l])
        mix = _hybrid_mixer(x, l, mix_w_in[l], rw_shift_mu[l], rw_w0[l], rw_w_up[l], rw_a0[l],
                            rw_a_up[l], rw_g_up[l], rw_k_k[l], rw_k_a[l], rw_r_k[l], rw_ln_g[l],
                            rw_ln_b[l], da_lam_q1[l], da_lam_k1[l], da_lam_q2[l], da_lam_k2[l],
                            da_norm_g[l], gd_conv_w[l], gd_a_log[l], gd_dt_bias[l], gd_norm_g[l],
                            mix_w_branch[l], mix_w_out[l])
        x = _layernorm(ALPHA * x + mix, ln2_g[l], ln2_b[l])
        x = _layernorm(ALPHA * x + 0.5 * _swiglu(x, ffn2_w_in[l], ffn2_w_out[l]), ln3_g[l], ln3_b[l])
    return x
```

```python
import functools
import math

import jax
import jax.numpy as jnp
from jax import lax
from jax.experimental import pallas as pl
from jax.experimental.pallas import tpu as pltpu

F32 = jnp.float32
BF16 = jnp.bfloat16

D_MODEL = 1024
DEPTH = 2
D_FF = 2816
RW_HEADS = 8
RW_HEAD_DIM = 64
RW_WIDTH = 512
RW_DECAY_LORA = 64
RW_ICLR_LORA = 64
RW_GATE_LORA = 128
RW_COLS = 3 * RW_WIDTH + RW_DECAY_LORA + RW_ICLR_LORA + RW_GATE_LORA
RW_GN_EPS = 64e-5
DA_HEADS = 4
DA_HEAD_DIM = 64
DA_WIDTH = 512
DA_COLS = 3 * DA_WIDTH
GD_HEADS = 4
GD_HEAD_DIM = 128
GD_WIDTH = 512
GD_CONV = 4
GD_COLS = 3 * GD_WIDTH + 2 * GD_HEADS + GD_WIDTH
N_BRANCH = 3
ALPHA = (2.0 * DEPTH) ** 0.25

LANES = 128
SUBLANES = 8
CHUNK = 64
PAIR = 2 * CHUNK
GD_AB_PAD = LANES
GD_IN = 3 * GD_WIDTH + GD_AB_PAD + GD_WIDTH
VMEM_LIMIT = 56 * 1024 * 1024


def _cparams(n_grid):
    return pltpu.CompilerParams(dimension_semantics=("arbitrary",) * n_grid,
                                vmem_limit_bytes=VMEM_LIMIT)


def _const_spec(shape):
    nd = len(shape)
    return pl.BlockSpec(shape, lambda *_: (0,) * nd, pipeline_mode=pl.Buffered(1))


def _mm(a, b):
    return jnp.dot(a.astype(BF16), b.astype(BF16), preferred_element_type=F32)


def _mm_nt(a, b):
    return lax.dot_general(a.astype(BF16), b.astype(BF16), (((1,), (1,)), ((), ())),
                           preferred_element_type=F32)


def _mm_tn(a, b):
    return lax.dot_general(a.astype(BF16), b.astype(BF16), (((0,), (0,)), ((), ())),
                           preferred_element_type=F32)


def _split_lhs_dot(x, exact_rhs, terms):
    acc = None
    rem = x
    for _ in range(terms):
        hi = rem.astype(BF16)
        part = jnp.dot(hi, exact_rhs, preferred_element_type=F32)
        acc = part if acc is None else acc + part
        rem = rem - hi.astype(F32)
    return acc


def _split_rhs_dot(exact_lhs, x, terms):
    acc = None
    rem = x
    for _ in range(terms):
        hi = rem.astype(BF16)
        part = jnp.dot(exact_lhs, hi, preferred_element_type=F32)
        acc = part if acc is None else acc + part
        rem = rem - hi.astype(F32)
    return acc


def _layernorm(z, g, b, eps=1e-5):
    mu = jnp.mean(z, axis=-1, keepdims=True)
    zc = z - mu
    var = jnp.mean(zc * zc, axis=-1, keepdims=True)
    return zc * lax.rsqrt(var + eps) * g + b


def _iota(shape, dim):
    return lax.broadcasted_iota(jnp.int32, shape, dim)


def _inv_unit_lower(low):
    n = -low
    eye = (_iota((PAIR, PAIR), 0) == _iota((PAIR, PAIR), 1)).astype(F32)
    inv = eye + n
    p = n
    for _ in range(int(math.log2(CHUNK)) - 1):
        p = _mm(p, p)
        inv = inv + _mm(inv, p)
    return inv


def _stack_heads(x, lo_mask):
    return jnp.concatenate([jnp.where(lo_mask, x, 0.0), jnp.where(lo_mask, 0.0, x)], axis=0)


def _ffn_ln_kernel(x_ref, wg_ref, wu_ref, wo_ref, g_ref, b_ref, y_ref, yb_ref, acc_ref, *, nf):
    x = x_ref[...]
    xb = x.astype(BF16)
    acc_ref[...] = jnp.zeros_like(acc_ref)

    def body(f, carry):
        gate = jnp.dot(xb, wg_ref[f], preferred_element_type=F32)
        up = jnp.dot(xb, wu_ref[f], preferred_element_type=F32)
        act = (gate * jax.nn.sigmoid(gate) * up).astype(BF16)
        acc_ref[...] += jnp.dot(act, wo_ref[f], preferred_element_type=F32)
        return carry

    lax.fori_loop(0, nf, body, 0)
    y = _layernorm(ALPHA * x + 0.5 * acc_ref[...], g_ref[...], b_ref[...])
    y_ref[...] = y
    yb_ref[...] = y.astype(BF16)


def _ffn_ln(x, w_in, w_out, g, b, *, tm=256, tf=256):
    t, d = x.shape
    dff = w_out.shape[0]
    nf = dff // tf
    assert nf * tf == dff and t % tm == 0
    wg = w_in[:, :dff].astype(BF16).reshape(d, nf, tf).transpose(1, 0, 2)
    wu = w_in[:, dff:].astype(BF16).reshape(d, nf, tf).transpose(1, 0, 2)
    wo = w_out.astype(BF16).reshape(nf, tf, d)
    row = pl.BlockSpec((tm, d), lambda i: (i, 0))
    return pl.pallas_call(
        functools.partial(_ffn_ln_kernel, nf=nf),
        grid=(t // tm,),
        in_specs=[row, _const_spec((nf, d, tf)), _const_spec((nf, d, tf)), _const_spec((nf, tf, d)),
                  _const_spec((1, d)), _const_spec((1, d))],
        out_specs=[row, row],
        out_shape=[jax.ShapeDtypeStruct((t, d), F32), jax.ShapeDtypeStruct((t, d), BF16)],
        scratch_shapes=[pltpu.VMEM((tm, d), F32)],
        compiler_params=_cparams(1),
        name="ffn_ln",
    )(x, wg, wu, wo, g.reshape(1, d), b.reshape(1, d))


def _mix_in_kernel(xb_ref, w_ref, rw_ref, da_ref, gd_ref, *, tn):
    xb = xb_ref[...]
    off = 0
    for out_ref in (rw_ref, da_ref, gd_ref):
        width = out_ref.shape[-1]
        for c0 in range(0, width, tn):
            c1 = min(c0 + tn, width)
            res = jnp.dot(xb, w_ref[:, off + c0:off + c1], preferred_element_type=F32)
            out_ref[:, c0:c1] = res.astype(out_ref.dtype)
        off += width


def _mix_in(xb, w_cat, *, tm=512, tn=256):
    t, d = xb.shape
    n = w_cat.shape[1]
    assert n == RW_COLS + DA_COLS + GD_IN and t % tm == 0
    return pl.pallas_call(
        functools.partial(_mix_in_kernel, tn=tn),
        grid=(t // tm,),
        in_specs=[pl.BlockSpec((tm, d), lambda i: (i, 0)), _const_spec((d, n))],
        out_specs=[pl.BlockSpec((tm, RW_COLS), lambda i: (i, 0)),
                   pl.BlockSpec((tm, DA_COLS), lambda i: (i, 0)),
                   pl.BlockSpec((tm, GD_IN), lambda i: (i, 0))],
        out_shape=[jax.ShapeDtypeStruct((t, RW_COLS), F32),
                   jax.ShapeDtypeStruct((t, DA_COLS), BF16),
                   jax.ShapeDtypeStruct((t, GD_IN), F32)],
        compiler_params=_cparams(1),
        name="mix_in",
    )(xb, w_cat)


def _rwkv_kernel(h_ref, mu_ref, w0_ref, a0_ref, kk_ref, ka_ref, rk_ref, lng_ref, lnb_ref,
                 lora_ref, gup_ref, bd_ref, tril_ref, y_ref, prev_ref, s_ref):
    c = pl.program_id(1)

    @pl.when(c == 0)
    def _():
        prev_ref[...] = jnp.zeros_like(prev_ref)
        s_ref[...] = jnp.zeros_like(s_ref)

    h = h_ref[0]
    hcat = jnp.concatenate([prev_ref[...], h], axis=0)
    hprev = pltpu.roll(hcat, 1, 0)[SUBLANES:SUBLANES + CHUNK]
    prev_ref[...] = h[CHUNK - SUBLANES:CHUNK]
    hs = h + mu_ref[...] * (hprev - h)

    w = RW_WIDTH
    r = hs[:, 0:w]
    k = hs[:, w:2 * w]
    v = hs[:, 2 * w:3 * w]
    wa = hs[:, 3 * w:3 * w + LANES]
    gd = hs[:, 3 * w + LANES:3 * w + 2 * LANES]
    lane = _iota((CHUNK, LANES), 1)
    lo = lane < RW_HEAD_DIM
    lora = _mm(jnp.where(lo, jnp.tanh(wa), wa), lora_ref[...])
    logw = -math.exp(-0.5) * jax.nn.sigmoid(w0_ref[...] + lora[:, 0:w])
    a = jax.nn.sigmoid(a0_ref[...] + lora[:, w:2 * w])
    g = _mm(jax.nn.sigmoid(gd), gup_ref[...])

    bd = bd_ref[...]
    kkr = k * kk_ref[...]
    kk = kkr * lax.rsqrt(_split_lhs_dot(kkr * kkr, bd, 2) + 1e-6)
    k2 = k * (1.0 + (a - 1.0) * ka_ref[...])
    bvec = kk * a
    bonus = _split_lhs_dot(r * k2 * rk_ref[...], bd, 2) * v

    e_in = _split_rhs_dot(tril_ref[...], logw, 3)
    e_ex = e_in - logw
    rho = e_in[CHUNK // 2 - 1:CHUNK // 2]
    e_last = e_in[CHUNK - 1:CHUNK]
    r_t = r * jnp.exp(e_in - rho)
    kk_t = kk * jnp.exp(e_ex - rho)
    e_neg = jnp.exp(rho - e_in)
    k_h = k2 * e_neg
    b_h = bvec * e_neg
    s_scale = jnp.exp(rho)
    d_out = jnp.exp(e_last - rho)

    tt = _iota((CHUNK, LANES), 0)
    ss = lane & (CHUNK - 1)
    strict = tt > ss
    incl = tt >= ss
    eye = (_iota((PAIR, PAIR), 0) == _iota((PAIR, PAIR), 1)).astype(F32)

    ys = []
    for p in range(RW_HEADS // 2):
        sl = slice(p * LANES, (p + 1) * LANES)
        kks = _stack_heads(kk_t[:, sl], lo)
        rs = _stack_heads(r_t[:, sl], lo)
        vs = _stack_heads(v[:, sl], lo)
        khs = _stack_heads(k_h[:, sl], lo)
        bhs = _stack_heads(b_h[:, sl], lo)
        z = _mm_nt(jnp.concatenate([kk_t[:, sl], r_t[:, sl]], axis=0),
                   jnp.concatenate([khs, bhs], axis=0))
        ak = _stack_heads(jnp.where(strict, z[0:CHUNK, 0:PAIR], 0.0), lo)
        ab = _stack_heads(jnp.where(strict, z[0:CHUNK, PAIR:2 * PAIR], 0.0), lo)
        bk = _stack_heads(jnp.where(incl, z[CHUNK:PAIR, 0:PAIR], 0.0), lo)
        bb = _stack_heads(jnp.where(incl, z[CHUNK:PAIR, PAIR:2 * PAIR], 0.0), lo)
        inv = _inv_unit_lower(ab)
        pq = _mm(inv, jnp.concatenate([kks, _mm(ak, vs)], axis=1))
        pm = pq[:, 0:PAIR]
        qm = pq[:, PAIR:2 * PAIR]
        rp = rs - _mm(bb, pm)
        y0 = _mm(bk, vs) - _mm(bb, qm)
        ptb = _mm_tn(pm, bhs)
        hm = _mm_tn(jnp.concatenate([vs, qm], axis=0), jnp.concatenate([khs, -bhs], axis=0))
        sp = s_ref[p] * s_scale[:, sl]
        yo = _mm_nt(rp, sp) + y0
        s_ref[p] = (sp - _mm(sp, ptb) + hm) * d_out[:, sl]
        ys.append(yo[0:CHUNK] + yo[CHUNK:PAIR])
    y = jnp.concatenate(ys, axis=1)

    inv_n = 1.0 / RW_HEAD_DIM
    ym = _split_lhs_dot(y, bd, 2) * inv_n
    yc = y - ym
    yv = _split_lhs_dot(yc * yc, bd, 2) * inv_n
    yn = yc * lax.rsqrt(yv + RW_GN_EPS) * lng_ref[...] + lnb_ref[...]
    y_ref[0] = ((yn + bonus) * g).astype(y_ref.dtype)


def _rwkv7(h_rw, mu, w0, w_up, a0, a_up, g_up, k_k, k_a, r_k, ln_g, ln_b):
    b, l, _ = h_rw.shape
    w = RW_WIDTH
    assert l % CHUNK == 0
    lora = jnp.zeros((LANES, 2 * w), F32)
    lora = lora.at[0:RW_DECAY_LORA, 0:w].set(w_up).at[RW_DECAY_LORA:LANES, w:2 * w].set(a_up)
    hid = jnp.arange(w) // RW_HEAD_DIM
    bd = (hid[:, None] == hid[None, :]).astype(BF16)
    tril = (jnp.arange(CHUNK)[:, None] >= jnp.arange(CHUNK)[None, :]).astype(BF16)
    vec = lambda t: t.reshape(1, -1).astype(F32)
    params = [vec(mu), vec(w0), vec(a0), vec(k_k), vec(k_a), vec(r_k), vec(ln_g), vec(ln_b),
              lora.astype(BF16), g_up.astype(BF16), bd, tril]
    return pl.pallas_call(
        _rwkv_kernel,
        grid=(b, l // CHUNK),
        in_specs=[pl.BlockSpec((1, CHUNK, RW_COLS), lambda i, j: (i, j, 0))]
        + [_const_spec(p.shape) for p in params],
        out_specs=pl.BlockSpec((1, CHUNK, w), lambda i, j: (i, j, 0)),
        out_shape=jax.ShapeDtypeStruct((b, l, w), BF16),
        scratch_shapes=[pltpu.VMEM((SUBLANES, RW_COLS), F32),
                        pltpu.VMEM((RW_HEADS // 2, PAIR, PAIR), F32)],
        compiler_params=_cparams(2),
        name="rwkv7",
    )(h_rw, *params)


def _attn_kernel(slope_ref, lam_ref, ng_ref, q_ref, k_ref, v_ref, o_ref, m_ref, l_ref, acc_ref,
                 *, tq, tk, lam_init):
    qi = pl.program_id(2)
    d = DA_HEAD_DIM
    scale = d ** -0.5
    q = q_ref[0].astype(F32) * scale
    lo = _iota((tq, LANES), 1) < d
    q2 = jnp.concatenate([jnp.where(lo, q, 0.0), jnp.where(lo, 0.0, q)], axis=0).astype(BF16)
    slope = slope_ref[0, 0:1, 0:1]
    m_ref[...] = jnp.full_like(m_ref, -jnp.inf)
    l_ref[...] = jnp.zeros_like(l_ref)
    acc_ref[...] = jnp.zeros_like(acc_ref)
    rel = (_iota((2 * tq, tk), 0) & (tq - 1)) - _iota((2 * tq, tk), 1)
    relf = rel.astype(F32)

    def step(j, masked):
        kb = k_ref[0, pl.ds(j * tk, tk), :]
        vb = v_ref[0, pl.ds(j * tk, tk), :]
        s = lax.dot_general(q2, kb, (((1,), (1,)), ((), ())), preferred_element_type=F32)
        dist = relf + (qi * tq - j * tk).astype(F32)
        s = s - slope * dist
        if masked:
            s = jnp.where(rel >= 0, s, -jnp.inf)
        m_old = m_ref[...]
        m_new = jnp.maximum(m_old, jnp.max(s, axis=-1, keepdims=True))
        alpha = jnp.exp(m_old - m_new)
        p = jnp.exp(s - m_new)
        l_ref[...] = alpha * l_ref[...] + jnp.sum(p, axis=-1, keepdims=True)
        acc_ref[...] = alpha * acc_ref[...] + jnp.dot(p.astype(BF16), vb, preferred_element_type=F32)
        m_ref[...] = m_new

    def body(j, carry):
        step(j, False)
        return carry

    nfull = (qi * tq) // tk
    lax.fori_loop(0, nfull, body, 0)
    step(nfull, True)

    lp = lam_ref[...]
    lam = (jnp.exp(jnp.sum(lp[0:1] * lp[1:2], axis=-1, keepdims=True))
           - jnp.exp(jnp.sum(lp[2:3] * lp[3:4], axis=-1, keepdims=True)) + lam_init)
    o = acc_ref[...] / l_ref[...]
    o = o[0:tq] - lam * o[tq:2 * tq]
    o = o * lax.rsqrt(jnp.mean(o * o, axis=-1, keepdims=True) + 1e-5) * ng_ref[...]
    o_ref[0] = (o * (1.0 - lam_init)).astype(o_ref.dtype)


def _diff_attention(h_da, lam_q1, lam_k1, lam_q2, lam_k2, norm_g, lam_init, *, tq=512):
    b, l, _ = h_da.shape
    tq = min(tq, l)
    tk = tq
    assert l % tq == 0 and tq & (tq - 1) == 0
    hh = DA_HEADS
    slopes = jnp.exp2(-8.0 * jnp.arange(1, hh + 1, dtype=F32) / hh)
    slopes = jnp.broadcast_to(slopes[:, None, None], (hh, SUBLANES, LANES))
    lam_p = jnp.stack([lam_q1, lam_k1, lam_q2, lam_k2]).astype(F32)
    ng = norm_g.reshape(1, 2 * DA_HEAD_DIM).astype(F32)
    return pl.pallas_call(
        functools.partial(_attn_kernel, tq=tq, tk=tk, lam_init=lam_init),
        grid=(b, hh, l // tq),
        in_specs=[pl.BlockSpec((1, SUBLANES, LANES), lambda i, h, j: (h, 0, 0)),
                  pl.BlockSpec(lam_p.shape, lambda i, h, j: (0, 0)),
                  pl.BlockSpec(ng.shape, lambda i, h, j: (0, 0)),
                  pl.BlockSpec((1, tq, LANES), lambda i, h, j: (i, j, h)),
                  pl.BlockSpec((1, l, LANES), lambda i, h, j: (i, 0, hh + h)),
                  pl.BlockSpec((1, l, LANES), lambda i, h, j: (i, 0, 2 * hh + h))],
        out_specs=pl.BlockSpec((1, tq, LANES), lambda i, h, j: (i, j, h)),
        out_shape=jax.ShapeDtypeStruct((b, l, DA_WIDTH), BF16),
        scratch_shapes=[pltpu.VMEM((2 * tq, 1), F32), pltpu.VMEM((2 * tq, 1), F32),
                        pltpu.VMEM((2 * tq, LANES), F32)],
        compiler_params=_cparams(3),
        name="diffattn",
    )(slopes, lam_p, ng, h_da, h_da, h_da)


def _softplus(x):
    return jnp.maximum(x, 0.0) + jnp.log(1.0 + jnp.exp(-jnp.abs(x)))


def _gdn_kernel(h_ref, cw_ref, alog_ref, dtb_ref, ng_ref, tril_ref, y_ref, prev_ref, s_ref):
    c = pl.program_id(1)

    @pl.when(c == 0)
    def _():
        prev_ref[...] = jnp.zeros_like(prev_ref)
        s_ref[...] = jnp.zeros_like(s_ref)

    wq = 3 * GD_WIDTH
    dh = GD_HEAD_DIM
    x = h_ref[0, :, 0:wq]
    xcat = jnp.concatenate([prev_ref[...], x], axis=0)
    prev_ref[...] = x[CHUNK - SUBLANES:CHUNK]
    cw = cw_ref[...]
    conv = xcat[SUBLANES:SUBLANES + CHUNK] * cw[GD_CONV - 1:GD_CONV]
    for j in range(GD_CONV - 1):
        shifted = pltpu.roll(xcat, GD_CONV - 1 - j, 0)[SUBLANES:SUBLANES + CHUNK]
        conv = conv + shifted * cw[j:j + 1]
    qkv = conv * jax.nn.sigmoid(conv)

    ab = h_ref[0, :, wq:wq + GD_AB_PAD]
    gfull = -jnp.exp(alog_ref[...]) * _softplus(ab + dtb_ref[...])
    gcum = _split_rhs_dot(tril_ref[...], gfull, 3)
    beta_full = jax.nn.sigmoid(ab)

    r0 = _iota((PAIR, PAIR), 0)
    c0 = _iota((PAIR, PAIR), 1)
    same = (r0 >= CHUNK) == (c0 >= CHUNK)
    incl = same & (r0 >= c0)
    strict = same & (r0 > c0)

    def l2n(t):
        return t * lax.rsqrt(jnp.sum(t * t, axis=-1, keepdims=True) + 1e-6)

    for p in range(GD_HEADS // 2):
        hs = (2 * p, 2 * p + 1)
        col = lambda full, h: full[:, h:h + 1]
        q_s = jnp.concatenate([l2n(qkv[:, h * dh:(h + 1) * dh]) * dh ** -0.5 for h in hs], axis=0)
        k_s = jnp.concatenate([l2n(qkv[:, GD_WIDTH + h * dh:GD_WIDTH + (h + 1) * dh]) for h in hs], axis=0)
        v_s = jnp.concatenate([qkv[:, 2 * GD_WIDTH + h * dh:2 * GD_WIDTH + (h + 1) * dh] for h in hs], axis=0)
        g_s = jnp.concatenate([col(gcum, h) for h in hs], axis=0)
        beta_s = jnp.concatenate([col(beta_full, GD_HEADS + h) for h in hs], axis=0)
        g_last = jnp.concatenate([jnp.broadcast_to(gcum[CHUNK - 1:CHUNK, h:h + 1], (CHUNK, 1)) for h in hs], axis=0)

        g_b = jnp.broadcast_to(g_s, (PAIR, PAIR))
        decay = jnp.exp(jnp.where(incl, g_b - g_b.T, -jnp.inf))
        kb = k_s * beta_s
        kk = _mm_nt(jnp.concatenate([kb, q_s], axis=0), k_s)
        low = jnp.where(strict, kk[0:PAIR] * decay, 0.0)
        intra = kk[PAIR:2 * PAIR] * decay
        inv = _inv_unit_lower(low)
        eg = jnp.exp(g_s)
        uw = _mm(inv, jnp.concatenate([v_s * beta_s, kb * eg], axis=1))
        u = uw[:, 0:dh]
        wm = uw[:, dh:2 * dh]
        qg = q_s * eg
        kd = k_s * jnp.exp(g_last - g_s)
        v_new = []
        o_state = []
        for i, h in enumerate(hs):
            rows = slice(i * CHUNK, (i + 1) * CHUNK)
            s = s_ref[h]
            ws = _mm(jnp.concatenate([wm[rows], qg[rows]], axis=0), s)
            v_new.append(u[rows] - ws[0:CHUNK])
            o_state.append(ws[CHUNK:PAIR])
        v_new = jnp.concatenate(v_new, axis=0)
        o = jnp.concatenate(o_state, axis=0) + _mm(intra, v_new)
        for i, h in enumerate(hs):
            rows = slice(i * CHUNK, (i + 1) * CHUNK)
            gl = jnp.exp(gcum[CHUNK - 1:CHUNK, h:h + 1])
            s_ref[h] = s_ref[h] * gl + _mm_tn(kd[rows], v_new[rows])
            oh = o[rows]
            oh = oh * lax.rsqrt(jnp.mean(oh * oh, axis=-1, keepdims=True) + 1e-6) * ng_ref[...]
            zc = wq + GD_AB_PAD + h * dh
            z = h_ref[0, :, zc:zc + dh]
            y_ref[0, :, h * dh:(h + 1) * dh] = (oh * (z * jax.nn.sigmoid(z))).astype(y_ref.dtype)


def _gated_deltanet(h_gd, conv_w, a_log, dt_bias, norm_g):
    b, l, _ = h_gd.shape
    assert l % CHUNK == 0
    pad = lambda t: jnp.zeros((1, GD_AB_PAD), F32).at[0, 0:GD_HEADS].set(t.astype(F32))
    tril = (jnp.arange(CHUNK)[:, None] >= jnp.arange(CHUNK)[None, :]).astype(BF16)
    params = [conv_w.astype(F32), pad(a_log), pad(dt_bias), norm_g.reshape(1, GD_HEAD_DIM).astype(F32), tril]
    return pl.pallas_call(
        _gdn_kernel,
        grid=(b, l // CHUNK),
        in_specs=[pl.BlockSpec((1, CHUNK, GD_IN), lambda i, j: (i, j, 0))]
        + [_const_spec(p.shape) for p in params],
        out_specs=pl.BlockSpec((1, CHUNK, GD_WIDTH), lambda i, j: (i, j, 0)),
        out_shape=jax.ShapeDtypeStruct((b, l, GD_WIDTH), BF16),
        scratch_shapes=[pltpu.VMEM((SUBLANES, 3 * GD_WIDTH), F32),
                        pltpu.VMEM((GD_HEADS, GD_HEAD_DIM, GD_HEAD_DIM), F32)],
        compiler_params=_cparams(2),
        name="gdn",
    )(h_gd, *params)


def _merge_ln_kernel(x_ref, xb_ref, ya_ref, yb_ref, yc_ref, wgate_ref, wbr_ref, wout_ref, g_ref, b_ref,
                     y_ref, y16_ref):
    xb = xb_ref[...]
    merged = None
    for n, br_ref in enumerate((ya_ref, yb_ref, yc_ref)):
        gate = jax.nn.sigmoid(jnp.dot(xb, wgate_ref[n], preferred_element_type=F32))
        term = gate * jnp.dot(br_ref[...], wbr_ref[n], preferred_element_type=F32)
        merged = term if merged is None else merged + term
    mix = jnp.dot(merged.astype(BF16), wout_ref[...], preferred_element_type=F32)
    y = _layernorm(ALPHA * x_ref[...] + mix, g_ref[...], b_ref[...])
    y_ref[...] = y
    y16_ref[...] = y.astype(BF16)


def _merge_ln(x, xb, ya, yb, yc, w_gate, w_branch, w_out, g, b, *, tm=256):
    t, d = x.shape
    bw = ya.shape[-1]
    row = pl.BlockSpec((tm, d), lambda i: (i, 0))
    brow = pl.BlockSpec((tm, bw), lambda i: (i, 0))
    return pl.pallas_call(
        _merge_ln_kernel,
        grid=(t // tm,),
        in_specs=[row, row, brow, brow, brow, _const_spec((N_BRANCH, d, d)),
                  _const_spec((N_BRANCH, bw, d)), _const_spec((d, d)), _const_spec((1, d)), _const_spec((1, d))],
        out_specs=[row, row],
        out_shape=[jax.ShapeDtypeStruct((t, d), F32), jax.ShapeDtypeStruct((t, d), BF16)],
        compiler_params=_cparams(1),
        name="merge_ln",
    )(x, xb, ya, yb, yc, w_gate, w_branch, w_out, g.reshape(1, d), b.reshape(1, d))


def _split_mix_w_in(w_in):
    d = w_in.shape[0]
    o1 = RW_COLS
    o2 = o1 + DA_COLS
    o3 = o2 + GD_COLS
    gq = o2 + 3 * GD_WIDTH
    ab = jnp.zeros((d, GD_AB_PAD), w_in.dtype).at[:, 0:2 * GD_HEADS].set(w_in[:, gq:gq + 2 * GD_HEADS])
    w_cat = jnp.concatenate([w_in[:, 0:o2], w_in[:, o2:gq], ab, w_in[:, gq + 2 * GD_HEADS:o3]], axis=1)
    w_gate = w_in[:, o3:].reshape(d, N_BRANCH, d).transpose(1, 0, 2)
    return w_cat.astype(BF16), w_gate.astype(BF16)


def kernel(x, ffn1_w_in, ffn1_w_out, ln1_g, ln1_b, mix_w_in, rw_shift_mu, rw_w0, rw_w_up, rw_a0, rw_a_up, rw_g_up, rw_k_k, rw_k_a, rw_r_k, rw_ln_g, rw_ln_b, da_lam_q1, da_lam_k1, da_lam_q2, da_lam_k2, da_norm_g, gd_conv_w, gd_a_log, gd_dt_bias, gd_norm_g, mix_w_branch, mix_w_out, ln2_g, ln2_b, ffn2_w_in, ffn2_w_out, ln3_g, ln3_b):
    b, l, d = x.shape
    t = b * l
    xf = x.reshape(t, d)
    for i in range(DEPTH):
        xf, xb = _ffn_ln(xf, ffn1_w_in[i], ffn1_w_out[i], ln1_g[i], ln1_b[i])
        w_cat, w_gate = _split_mix_w_in(mix_w_in[i])
        h_rw, h_da, h_gd = _mix_in(xb, w_cat)
        ya = _rwkv7(h_rw.reshape(b, l, RW_COLS), rw_shift_mu[i], rw_w0[i], rw_w_up[i], rw_a0[i], rw_a_up[i],
                    rw_g_up[i], rw_k_k[i], rw_k_a[i], rw_r_k[i].reshape(-1), rw_ln_g[i], rw_ln_b[i])
        lam_init = 0.8 - 0.6 * math.exp(-0.3 * i)
        yb = _diff_attention(h_da.reshape(b, l, DA_COLS), da_lam_q1[i], da_lam_k1[i], da_lam_q2[i],
                             da_lam_k2[i], da_norm_g[i], lam_init)
        yc = _gated_deltanet(h_gd.reshape(b, l, GD_IN), gd_conv_w[i], gd_a_log[i], gd_dt_bias[i], gd_norm_g[i])
        xf, xb = _merge_ln(xf, xb, ya.reshape(t, RW_WIDTH), yb.reshape(t, DA_WIDTH), yc.reshape(t, GD_WIDTH),
                           w_gate, mix_w_branch[i].astype(BF16), mix_w_out[i].astype(BF16), ln2_g[i], ln2_b[i])
        xf, xb = _ffn_ln(xf, ffn2_w_in[i], ffn2_w_out[i], ln3_g[i], ln3_b[i])
    return xf.reshape(b, l, d)
```

```python
import functools
import math

import jax
import jax.numpy as jnp
from jax import lax
from jax.experimental import pallas as pl
from jax.experimental.pallas import tpu as pltpu

F32 = jnp.float32
BF16 = jnp.bfloat16

D_MODEL = 1024
DEPTH = 2
D_FF = 2816
RW_HEADS = 8
RW_HEAD_DIM = 64
RW_WIDTH = 512
RW_DECAY_LORA = 64
RW_ICLR_LORA = 64
RW_GATE_LORA = 128
RW_COLS = 3 * RW_WIDTH + RW_DECAY_LORA + RW_ICLR_LORA + RW_GATE_LORA
RW_GN_EPS = 64e-5
DA_HEADS = 4
DA_HEAD_DIM = 64
DA_WIDTH = 512
DA_COLS = 3 * DA_WIDTH
GD_HEADS = 4
GD_HEAD_DIM = 128
GD_WIDTH = 512
GD_CONV = 4
GD_COLS = 3 * GD_WIDTH + 2 * GD_HEADS + GD_WIDTH
N_BRANCH = 3
ALPHA = (2.0 * DEPTH) ** 0.25

LANES = 128
SUBLANES = 8
CHUNK = 64
PAIR = 2 * CHUNK
GD_AB_PAD = LANES
GD_IN = 3 * GD_WIDTH + GD_AB_PAD + GD_WIDTH
VMEM_LIMIT = 56 * 1024 * 1024
SCAN_BATCHES = 2
SCAN_CHUNKS = 2


def _cparams(n_grid):
    return pltpu.CompilerParams(dimension_semantics=("arbitrary",) * n_grid,
                                vmem_limit_bytes=VMEM_LIMIT)


def _const_spec(shape):
    nd = len(shape)
    return pl.BlockSpec(shape, lambda *_: (0,) * nd, pipeline_mode=pl.Buffered(1))


def _mm(a, b):
    return jnp.dot(a.astype(BF16), b.astype(BF16), preferred_element_type=F32)


def _mm_nt(a, b):
    return lax.dot_general(a.astype(BF16), b.astype(BF16), (((1,), (1,)), ((), ())),
                           preferred_element_type=F32)


def _mm_tn(a, b):
    return lax.dot_general(a.astype(BF16), b.astype(BF16), (((0,), (0,)), ((), ())),
                           preferred_element_type=F32)


def _split_lhs_dot(x, exact_rhs, terms):
    acc = None
    rem = x
    for _ in range(terms):
        hi = rem.astype(BF16)
        part = jnp.dot(hi, exact_rhs, preferred_element_type=F32)
        acc = part if acc is None else acc + part
        rem = rem - hi.astype(F32)
    return acc


def _split_rhs_dot(exact_lhs, x, terms):
    acc = None
    rem = x
    for _ in range(terms):
        hi = rem.astype(BF16)
        part = jnp.dot(exact_lhs, hi, preferred_element_type=F32)
        acc = part if acc is None else acc + part
        rem = rem - hi.astype(F32)
    return acc


def _layernorm(z, g, b, eps=1e-5):
    mu = jnp.mean(z, axis=-1, keepdims=True)
    zc = z - mu
    var = jnp.mean(zc * zc, axis=-1, keepdims=True)
    return zc * lax.rsqrt(var + eps) * g + b


def _iota(shape, dim):
    return lax.broadcasted_iota(jnp.int32, shape, dim)


def _inv_unit_lower_many(lows):
    eye = (_iota((PAIR, PAIR), 0) == _iota((PAIR, PAIR), 1)).astype(F32)
    pw = [-low for low in lows]
    inv = [eye + n for n in pw]
    for _ in range(int(math.log2(CHUNK)) - 1):
        pw = [_mm(p, p) for p in pw]
        inv = [i + _mm(i, p) for i, p in zip(inv, pw)]
    return inv


def _stack_heads(x, lo_mask):
    return jnp.concatenate([jnp.where(lo_mask, x, 0.0), jnp.where(lo_mask, 0.0, x)], axis=0)


def _chunk_tril(n_chunks):
    r = jnp.arange(n_chunks * CHUNK)
    return ((r[:, None] >= r[None, :]) & (r[:, None] // CHUNK == r[None, :] // CHUNK)).astype(BF16)


def _rows_of_chunk(x, row):
    n = x.shape[0] // CHUNK
    parts = [jnp.broadcast_to(x[i * CHUNK + row:i * CHUNK + row + 1], (CHUNK,) + x.shape[1:]) for i in range(n)]
    return jnp.concatenate(parts, axis=0) if n > 1 else parts[0]


def _ffn_ln_kernel(x_ref, wg_ref, wu_ref, wo_ref, g_ref, b_ref, y_ref, yb_ref, acc_ref, *, nf):
    x = x_ref[...]
    xb = x.astype(BF16)
    acc_ref[...] = jnp.zeros_like(acc_ref)

    def body(f, carry):
        gate = jnp.dot(xb, wg_ref[f], preferred_element_type=F32)
        up = jnp.dot(xb, wu_ref[f], preferred_element_type=F32)
        act = (gate * jax.nn.sigmoid(gate) * up).astype(BF16)
        acc_ref[...] += jnp.dot(act, wo_ref[f], preferred_element_type=F32)
        return carry

    lax.fori_loop(0, nf, body, 0)
    y = _layernorm(ALPHA * x + 0.5 * acc_ref[...], g_ref[...], b_ref[...])
    y_ref[...] = y
    yb_ref[...] = y.astype(BF16)


def _ffn_ln(x, w_in, w_out, g, b, *, tm=512, tf=256):
    t, d = x.shape
    dff = w_out.shape[0]
    nf = dff // tf
    assert nf * tf == dff and t % tm == 0
    wg = w_in[:, :dff].astype(BF16).reshape(d, nf, tf).transpose(1, 0, 2)
    wu = w_in[:, dff:].astype(BF16).reshape(d, nf, tf).transpose(1, 0, 2)
    wo = w_out.astype(BF16).reshape(nf, tf, d)
    row = pl.BlockSpec((tm, d), lambda i: (i, 0))
    return pl.pallas_call(
        functools.partial(_ffn_ln_kernel, nf=nf),
        grid=(t // tm,),
        in_specs=[row, _const_spec((nf, d, tf)), _const_spec((nf, d, tf)), _const_spec((nf, tf, d)),
                  _const_spec((1, d)), _const_spec((1, d))],
        out_specs=[row, row],
        out_shape=[jax.ShapeDtypeStruct((t, d), F32), jax.ShapeDtypeStruct((t, d), BF16)],
        scratch_shapes=[pltpu.VMEM((tm, d), F32)],
        compiler_params=_cparams(1),
        name="ffn_ln",
    )(x, wg, wu, wo, g.reshape(1, d), b.reshape(1, d))


def _mix_in_kernel(xb_ref, w_ref, rw_ref, da_ref, gd_ref, *, tn):
    xb = xb_ref[...]
    off = 0
    for out_ref in (rw_ref, da_ref, gd_ref):
        width = out_ref.shape[-1]
        for c0 in range(0, width, tn):
            c1 = min(c0 + tn, width)
            res = jnp.dot(xb, w_ref[:, off + c0:off + c1], preferred_element_type=F32)
            out_ref[:, c0:c1] = res.astype(out_ref.dtype)
        off += width


def _mix_in(xb, w_cat, *, tm=512, tn=256):
    t, d = xb.shape
    n = w_cat.shape[1]
    assert n == RW_COLS + DA_COLS + GD_IN and t % tm == 0
    return pl.pallas_call(
        functools.partial(_mix_in_kernel, tn=tn),
        grid=(t // tm,),
        in_specs=[pl.BlockSpec((tm, d), lambda i: (i, 0)), _const_spec((d, n))],
        out_specs=[pl.BlockSpec((tm, RW_COLS), lambda i: (i, 0)),
                   pl.BlockSpec((tm, DA_COLS), lambda i: (i, 0)),
                   pl.BlockSpec((tm, GD_IN), lambda i: (i, 0))],
        out_shape=[jax.ShapeDtypeStruct((t, RW_COLS), F32),
                   jax.ShapeDtypeStruct((t, DA_COLS), BF16),
                   jax.ShapeDtypeStruct((t, GD_IN), F32)],
        compiler_params=_cparams(1),
        name="mix_in",
    )(xb, w_cat)


def _rwkv_kernel(h_ref, mu_ref, w0_ref, a0_ref, kk_ref, ka_ref, rk_ref, lng_ref, lnb_ref,
                 lora_ref, gup_ref, bd_ref, tril_ref, y_ref, prev_ref, s_ref, *, nb, nc):
    @pl.when(pl.program_id(1) == 0)
    def _():
        prev_ref[...] = jnp.zeros_like(prev_ref)
        s_ref[...] = jnp.zeros_like(s_ref)

    rows_b = nc * CHUNK
    n_pairs = RW_HEADS // 2
    parts = []
    for b in range(nb):
        h = h_ref[b]
        hcat = jnp.concatenate([prev_ref[b], h], axis=0)
        hprev = pltpu.roll(hcat, 1, 0)[SUBLANES:SUBLANES + rows_b]
        prev_ref[b] = h[rows_b - SUBLANES:rows_b]
        parts.append(h + mu_ref[...] * (hprev - h))
    hs = jnp.concatenate(parts, axis=0) if nb > 1 else parts[0]
    n_rows = nb * rows_b

    w = RW_WIDTH
    r = hs[:, 0:w]
    k = hs[:, w:2 * w]
    v = hs[:, 2 * w:3 * w]
    wa = hs[:, 3 * w:3 * w + LANES]
    gd = hs[:, 3 * w + LANES:3 * w + 2 * LANES]
    lo_n = _iota((n_rows, LANES), 1) < RW_HEAD_DIM
    lora = _mm(jnp.where(lo_n, jnp.tanh(wa), wa), lora_ref[...])
    logw = -math.exp(-0.5) * jax.nn.sigmoid(w0_ref[...] + lora[:, 0:w])
    a = jax.nn.sigmoid(a0_ref[...] + lora[:, w:2 * w])
    g = _mm(jax.nn.sigmoid(gd), gup_ref[...])

    bd = bd_ref[...]
    kkr = k * kk_ref[...]
    kk = kkr * lax.rsqrt(_split_lhs_dot(kkr * kkr, bd, 2) + 1e-6)
    k2 = k * (1.0 + (a - 1.0) * ka_ref[...])
    bvec = kk * a
    bonus = _split_lhs_dot(r * k2 * rk_ref[...], bd, 2) * v

    e_in = _split_rhs_dot(tril_ref[...], logw, 3)
    rho = _rows_of_chunk(e_in, CHUNK // 2 - 1)
    r_t = r * jnp.exp(e_in - rho)
    kk_t = kk * jnp.exp(e_in - logw - rho)
    e_neg = jnp.exp(rho - e_in)
    k_h = k2 * e_neg
    b_h = bvec * e_neg

    lane = _iota((CHUNK, LANES), 1)
    lo = lane < RW_HEAD_DIM
    tt = _iota((CHUNK, LANES), 0)
    ss = lane & (CHUNK - 1)
    strict = tt > ss
    incl = tt >= ss

    units = [(b, ci, p) for b in range(nb) for ci in range(nc) for p in range(n_pairs)]

    def tile(x, u):
        b, ci, p = u
        r0 = (b * nc + ci) * CHUNK
        return x[r0:r0 + CHUNK, p * LANES:(p + 1) * LANES]

    kks = [_stack_heads(tile(kk_t, u), lo) for u in units]
    rs = [_stack_heads(tile(r_t, u), lo) for u in units]
    vs = [_stack_heads(tile(v, u), lo) for u in units]
    khs = [_stack_heads(tile(k_h, u), lo) for u in units]
    bhs = [_stack_heads(tile(b_h, u), lo) for u in units]
    z = [_mm_nt(jnp.concatenate([tile(kk_t, u), tile(r_t, u)], axis=0), jnp.concatenate([kh, bh], axis=0))
         for u, kh, bh in zip(units, khs, bhs)]
    ak = [_stack_heads(jnp.where(strict, zz[0:CHUNK, 0:PAIR], 0.0), lo) for zz in z]
    ab = [_stack_heads(jnp.where(strict, zz[0:CHUNK, PAIR:2 * PAIR], 0.0), lo) for zz in z]
    bk = [_stack_heads(jnp.where(incl, zz[CHUNK:PAIR, 0:PAIR], 0.0), lo) for zz in z]
    bb = [_stack_heads(jnp.where(incl, zz[CHUNK:PAIR, PAIR:2 * PAIR], 0.0), lo) for zz in z]
    inv = _inv_unit_lower_many(ab)
    akv = [_mm(x, y) for x, y in zip(ak, vs)]
    bkv = [_mm(x, y) for x, y in zip(bk, vs)]
    pq = [_mm(i, jnp.concatenate([x, y], axis=1)) for i, x, y in zip(inv, kks, akv)]
    bpq = [_mm(x, y) for x, y in zip(bb, pq)]
    rp = [x - y[:, 0:PAIR] for x, y in zip(rs, bpq)]
    y0 = [x - y[:, PAIR:2 * PAIR] for x, y in zip(bkv, bpq)]
    ptb = [_mm_tn(x[:, 0:PAIR], y) for x, y in zip(pq, bhs)]
    hm = [_mm_tn(jnp.concatenate([x, y[:, PAIR:2 * PAIR]], axis=0), jnp.concatenate([kh, -bh], axis=0))
          for x, y, kh, bh in zip(vs, pq, khs, bhs)]

    s_scale = jnp.exp(rho)
    d_out = jnp.exp(_rows_of_chunk(e_in, CHUNK - 1) - rho)
    yo = {}
    for b in range(nb):
        for p in range(n_pairs):
            s = s_ref[b * n_pairs + p]
            for ci in range(nc):
                i = units.index((b, ci, p))
                r0 = (b * nc + ci) * CHUNK
                sl = slice(p * LANES, (p + 1) * LANES)
                sp = s * s_scale[r0:r0 + 1, sl]
                yo[(b, ci, p)] = _mm_nt(rp[i], sp) + y0[i]
                s = (sp - _mm(sp, ptb[i]) + hm[i]) * d_out[r0:r0 + 1, sl]
            s_ref[b * n_pairs + p] = s
    y = jnp.concatenate(
        [jnp.concatenate([yo[(b, ci, p)][0:CHUNK] + yo[(b, ci, p)][CHUNK:PAIR] for p in range(n_pairs)], axis=1)
         for b in range(nb) for ci in range(nc)], axis=0)

    inv_n = 1.0 / RW_HEAD_DIM
    ym = _split_lhs_dot(y, bd, 2) * inv_n
    yc = y - ym
    yv = _split_lhs_dot(yc * yc, bd, 2) * inv_n
    yn = yc * lax.rsqrt(yv + RW_GN_EPS) * lng_ref[...] + lnb_ref[...]
    out = ((yn + bonus) * g).astype(y_ref.dtype)
    for b in range(nb):
        y_ref[b] = out[b * rows_b:(b + 1) * rows_b]


def _rwkv7(h_rw, mu, w0, w_up, a0, a_up, g_up, k_k, k_a, r_k, ln_g, ln_b, *, nb=SCAN_BATCHES, nc=SCAN_CHUNKS):
    b, l, _ = h_rw.shape
    w = RW_WIDTH
    nb = min(nb, b)
    rows_b = nc * CHUNK
    assert l % rows_b == 0 and b % nb == 0
    lora = jnp.zeros((LANES, 2 * w), F32)
    lora = lora.at[0:RW_DECAY_LORA, 0:w].set(w_up).at[RW_DECAY_LORA:LANES, w:2 * w].set(a_up)
    hid = jnp.arange(w) // RW_HEAD_DIM
    bd = (hid[:, None] == hid[None, :]).astype(BF16)
    vec = lambda t: t.reshape(1, -1).astype(F32)
    params = [vec(mu), vec(w0), vec(a0), vec(k_k), vec(k_a), vec(r_k), vec(ln_g), vec(ln_b),
              lora.astype(BF16), g_up.astype(BF16), bd, _chunk_tril(nb * nc)]
    return pl.pallas_call(
        functools.partial(_rwkv_kernel, nb=nb, nc=nc),
        grid=(b // nb, l // rows_b),
        in_specs=[pl.BlockSpec((nb, rows_b, RW_COLS), lambda i, j: (i, j, 0))]
        + [_const_spec(p.shape) for p in params],
        out_specs=pl.BlockSpec((nb, rows_b, w), lambda i, j: (i, j, 0)),
        out_shape=jax.ShapeDtypeStruct((b, l, w), BF16),
        scratch_shapes=[pltpu.VMEM((nb, SUBLANES, RW_COLS), F32),
                        pltpu.VMEM((nb * RW_HEADS // 2, PAIR, PAIR), F32)],
        compiler_params=_cparams(2),
        name="rwkv7",
    )(h_rw, *params)


def _attn_kernel(slope_ref, lam_ref, ng_ref, q_ref, k_ref, v_ref, o_ref, m_ref, l_ref, acc_ref,
                 *, tq, tk, rc, lam_init):
    qi = pl.program_id(2)
    d = DA_HEAD_DIM
    scale = d ** -0.5
    q = q_ref[0].astype(F32) * scale
    lo = _iota((tq, LANES), 1) < d
    q2 = jnp.concatenate([jnp.where(lo, q, 0.0), jnp.where(lo, 0.0, q)], axis=0).astype(BF16)
    slope = slope_ref[0, 0:1, 0:1]
    m_ref[...] = jnp.full_like(m_ref, -jnp.inf)
    l_ref[...] = jnp.zeros_like(l_ref)
    acc_ref[...] = jnp.zeros_like(acc_ref)
    kcol = slope * _iota((1, tk), 1).astype(F32)

    def step(j, masked):
        kb = k_ref[0, pl.ds(j * tk, tk), :]
        vb = v_ref[0, pl.ds(j * tk, tk), :]
        kbias = kcol + slope * (j * tk - qi * tq).astype(F32)
        for c0 in range(0, 2 * tq, rc):
            rows = slice(c0, c0 + rc)
            s = lax.dot_general(q2[rows], kb, (((1,), (1,)), ((), ())), preferred_element_type=F32) + kbias
            if masked:
                off = c0 % tq
                s = jnp.where((_iota((rc, tk), 0) + off) >= _iota((rc, tk), 1), s, -jnp.inf)
            m_old = m_ref[rows]
            m_new = jnp.maximum(m_old, jnp.max(s, axis=-1, keepdims=True))
            alpha = jnp.exp(m_old - m_new)
            p = jnp.exp(s - m_new)
            l_ref[rows] = alpha * l_ref[rows] + jnp.sum(p, axis=-1, keepdims=True)
            acc_ref[rows] = alpha * acc_ref[rows] + jnp.dot(p.astype(BF16), vb, preferred_element_type=F32)
            m_ref[rows] = m_new

    def body(j, carry):
        step(j, False)
        return carry

    nfull = (qi * tq) // tk
    lax.fori_loop(0, nfull, body, 0)
    step(nfull, True)

    lp = lam_ref[...]
    lam = (jnp.exp(jnp.sum(lp[0:1] * lp[1:2], axis=-1, keepdims=True))
           - jnp.exp(jnp.sum(lp[2:3] * lp[3:4], axis=-1, keepdims=True)) + lam_init)
    o = acc_ref[...] / l_ref[...]
    o = o[0:tq] - lam * o[tq:2 * tq]
    o = o * lax.rsqrt(jnp.mean(o * o, axis=-1, keepdims=True) + 1e-5) * ng_ref[...]
    o_ref[0] = (o * (1.0 - lam_init)).astype(o_ref.dtype)


def _diff_attention(h_da, lam_q1, lam_k1, lam_q2, lam_k2, norm_g, lam_init, *, tq=512, rc=256):
    b, l, _ = h_da.shape
    tq = min(tq, l)
    rc = min(rc, tq)
    tk = tq
    assert l % tq == 0 and tq % rc == 0
    hh = DA_HEADS
    slopes = jnp.exp2(-8.0 * jnp.arange(1, hh + 1, dtype=F32) / hh)
    slopes = jnp.broadcast_to(slopes[:, None, None], (hh, SUBLANES, LANES))
    lam_p = jnp.stack([lam_q1, lam_k1, lam_q2, lam_k2]).astype(F32)
    ng = norm_g.reshape(1, 2 * DA_HEAD_DIM).astype(F32)
    return pl.pallas_call(
        functools.partial(_attn_kernel, tq=tq, tk=tk, rc=rc, lam_init=lam_init),
        grid=(b, hh, l // tq),
        in_specs=[pl.BlockSpec((1, SUBLANES, LANES), lambda i, h, j: (h, 0, 0)),
                  pl.BlockSpec(lam_p.shape, lambda i, h, j: (0, 0)),
                  pl.BlockSpec(ng.shape, lambda i, h, j: (0, 0)),
                  pl.BlockSpec((1, tq, LANES), lambda i, h, j: (i, j, h)),
                  pl.BlockSpec((1, l, LANES), lambda i, h, j: (i, 0, hh + h)),
                  pl.BlockSpec((1, l, LANES), lambda i, h, j: (i, 0, 2 * hh + h))],
        out_specs=pl.BlockSpec((1, tq, LANES), lambda i, h, j: (i, j, h)),
        out_shape=jax.ShapeDtypeStruct((b, l, DA_WIDTH), BF16),
        scratch_shapes=[pltpu.VMEM((2 * tq, 1), F32), pltpu.VMEM((2 * tq, 1), F32),
                        pltpu.VMEM((2 * tq, LANES), F32)],
        compiler_params=_cparams(3),
        name="diffattn",
    )(slopes, lam_p, ng, h_da, h_da, h_da)


def _softplus(x):
    return jnp.maximum(x, 0.0) + jnp.log(1.0 + jnp.exp(-jnp.abs(x)))


def _gdn_kernel(h_ref, cw_ref, alog_ref, dtb_ref, ng_ref, tril_ref, y_ref, prev_ref, s_ref, *, nb, nc):
    @pl.when(pl.program_id(1) == 0)
    def _():
        prev_ref[...] = jnp.zeros_like(prev_ref)
        s_ref[...] = jnp.zeros_like(s_ref)

    rows_b = nc * CHUNK
    wq = 3 * GD_WIDTH
    dh = GD_HEAD_DIM
    cw = cw_ref[...]
    parts = []
    for b in range(nb):
        x = h_ref[b, :, 0:wq]
        xcat = jnp.concatenate([prev_ref[b], x], axis=0)
        prev_ref[b] = x[rows_b - SUBLANES:rows_b]
        conv = x * cw[GD_CONV - 1:GD_CONV]
        for j in range(GD_CONV - 1):
            conv = conv + pltpu.roll(xcat, GD_CONV - 1 - j, 0)[SUBLANES:SUBLANES + rows_b] * cw[j:j + 1]
        parts.append(conv)
    conv = jnp.concatenate(parts, axis=0) if nb > 1 else parts[0]
    qkv = conv * jax.nn.sigmoid(conv)

    ab = jnp.concatenate([h_ref[b, :, wq:wq + GD_AB_PAD] for b in range(nb)], axis=0)
    gfull = -jnp.exp(alog_ref[...]) * _softplus(ab + dtb_ref[...])
    gcum = _split_rhs_dot(tril_ref[...], gfull, 3)
    glast = _rows_of_chunk(gcum, CHUNK - 1)
    beta_full = jax.nn.sigmoid(ab)

    def l2n(t):
        return t * lax.rsqrt(jnp.sum(t * t, axis=-1, keepdims=True) + 1e-6)

    qn = [l2n(qkv[:, h * dh:(h + 1) * dh]) * dh ** -0.5 for h in range(GD_HEADS)]
    kn = [l2n(qkv[:, GD_WIDTH + h * dh:GD_WIDTH + (h + 1) * dh]) for h in range(GD_HEADS)]
    vh = [qkv[:, 2 * GD_WIDTH + h * dh:2 * GD_WIDTH + (h + 1) * dh] for h in range(GD_HEADS)]

    r0 = _iota((PAIR, PAIR), 0)
    c0 = _iota((PAIR, PAIR), 1)
    same = (r0 >= CHUNK) == (c0 >= CHUNK)
    incl = same & (r0 >= c0)
    strict = same & (r0 > c0)

    n_pairs = GD_HEADS // 2
    units = [(b, ci, p) for b in range(nb) for ci in range(nc) for p in range(n_pairs)]

    def stack(per_head, u, lane_of=None):
        b, ci, p = u
        rr = slice((b * nc + ci) * CHUNK, (b * nc + ci + 1) * CHUNK)
        if lane_of is None:
            return jnp.concatenate([per_head[2 * p][rr], per_head[2 * p + 1][rr]], axis=0)
        return jnp.concatenate([per_head[rr, lane_of + 2 * p:lane_of + 2 * p + 1],
                                per_head[rr, lane_of + 2 * p + 1:lane_of + 2 * p + 2]], axis=0)

    q_s = [stack(qn, u) for u in units]
    k_s = [stack(kn, u) for u in units]
    v_s = [stack(vh, u) for u in units]
    g_s = [stack(gcum, u, 0) for u in units]
    gl_s = [stack(glast, u, 0) for u in units]
    beta_s = [stack(beta_full, u, GD_HEADS) for u in units]

    decay = []
    for gs in g_s:
        g_b = jnp.broadcast_to(gs, (PAIR, PAIR))
        decay.append(jnp.exp(jnp.where(incl, g_b - g_b.T, -jnp.inf)))
    kb = [x * y for x, y in zip(k_s, beta_s)]
    kk = [_mm_nt(jnp.concatenate([x, y], axis=0), z) for x, y, z in zip(kb, q_s, k_s)]
    low = [jnp.where(strict, x[0:PAIR] * dc, 0.0) for x, dc in zip(kk, decay)]
    intra = [x[PAIR:2 * PAIR] * dc for x, dc in zip(kk, decay)]
    inv = _inv_unit_lower_many(low)
    eg = [jnp.exp(gs) for gs in g_s]
    uw = [_mm(i, jnp.concatenate([v * bt, x * e], axis=1)) for i, v, bt, x, e in zip(inv, v_s, beta_s, kb, eg)]
    qg = [x * e for x, e in zip(q_s, eg)]
    kd = [x * jnp.exp(gl - gs) for x, gl, gs in zip(k_s, gl_s, g_s)]

    v_new = {}
    o_state = {}
    for b in range(nb):
        for h in range(GD_HEADS):
            p, i_h = divmod(h, 2)
            rows = slice(i_h * CHUNK, (i_h + 1) * CHUNK)
            s = s_ref[b * GD_HEADS + h]
            for ci in range(nc):
                i = units.index((b, ci, p))
                ws = _mm(jnp.concatenate([uw[i][rows, dh:2 * dh], qg[i][rows]], axis=0), s)
                vn = uw[i][rows, 0:dh] - ws[0:CHUNK]
                v_new[(b, ci, h)] = vn
                o_state[(b, ci, h)] = ws[CHUNK:PAIR]
                s = s * jnp.exp(gl_s[i][rows][0:1]) + _mm_tn(kd[i][rows], vn)
            s_ref[b * GD_HEADS + h] = s

    for i, (b, ci, p) in enumerate(units):
        vn = jnp.concatenate([v_new[(b, ci, 2 * p)], v_new[(b, ci, 2 * p + 1)]], axis=0)
        o = jnp.concatenate([o_state[(b, ci, 2 * p)], o_state[(b, ci, 2 * p + 1)]], axis=0) + _mm(intra[i], vn)
        for i_h in range(2):
            h = 2 * p + i_h
            oh = o[i_h * CHUNK:(i_h + 1) * CHUNK]
            oh = oh * lax.rsqrt(jnp.mean(oh * oh, axis=-1, keepdims=True) + 1e-6) * ng_ref[...]
            zc = wq + GD_AB_PAD + h * dh
            z = h_ref[b, ci * CHUNK:(ci + 1) * CHUNK, zc:zc + dh]
            y_ref[b, ci * CHUNK:(ci + 1) * CHUNK, h * dh:(h + 1) * dh] = (
                oh * (z * jax.nn.sigmoid(z))).astype(y_ref.dtype)


def _gated_deltanet(h_gd, conv_w, a_log, dt_bias, norm_g, *, nb=SCAN_BATCHES, nc=SCAN_CHUNKS):
    b, l, _ = h_gd.shape
    nb = min(nb, b)
    rows_b = nc * CHUNK
    assert l % rows_b == 0 and b % nb == 0
    pad = lambda t: jnp.zeros((1, GD_AB_PAD), F32).at[0, 0:GD_HEADS].set(t.astype(F32))
    params = [conv_w.astype(F32), pad(a_log), pad(dt_bias), norm_g.reshape(1, GD_HEAD_DIM).astype(F32),
              _chunk_tril(nb * nc)]
    return pl.pallas_call(
        functools.partial(_gdn_kernel, nb=nb, nc=nc),
        grid=(b // nb, l // rows_b),
        in_specs=[pl.BlockSpec((nb, rows_b, GD_IN), lambda i, j: (i, j, 0))]
        + [_const_spec(p.shape) for p in params],
        out_specs=pl.BlockSpec((nb, rows_b, GD_WIDTH), lambda i, j: (i, j, 0)),
        out_shape=jax.ShapeDtypeStruct((b, l, GD_WIDTH), BF16),
        scratch_shapes=[pltpu.VMEM((nb, SUBLANES, 3 * GD_WIDTH), F32),
                        pltpu.VMEM((nb * GD_HEADS, GD_HEAD_DIM, GD_HEAD_DIM), F32)],
        compiler_params=_cparams(2),
        name="gdn",
    )(h_gd, *params)


def _merge_ln_kernel(x_ref, xb_ref, ya_ref, yb_ref, yc_ref, wgate_ref, wbr_ref, wout_ref, g_ref, b_ref,
                     y_ref, y16_ref):
    xb = xb_ref[...]
    merged = None
    for n, br_ref in enumerate((ya_ref, yb_ref, yc_ref)):
        gate = jax.nn.sigmoid(jnp.dot(xb, wgate_ref[n], preferred_element_type=F32))
        term = gate * jnp.dot(br_ref[...], wbr_ref[n], preferred_element_type=F32)
        merged = term if merged is None else merged + term
    mix = jnp.dot(merged.astype(BF16), wout_ref[...], preferred_element_type=F32)
    y = _layernorm(ALPHA * x_ref[...] + mix, g_ref[...], b_ref[...])
    y_ref[...] = y
    y16_ref[...] = y.astype(BF16)


def _merge_ln(x, xb, ya, yb, yc, w_gate, w_branch, w_out, g, b, *, tm=256):
    t, d = x.shape
    bw = ya.shape[-1]
    row = pl.BlockSpec((tm, d), lambda i: (i, 0))
    brow = pl.BlockSpec((tm, bw), lambda i: (i, 0))
    return pl.pallas_call(
        _merge_ln_kernel,
        grid=(t // tm,),
        in_specs=[row, row, brow, brow, brow, _const_spec((N_BRANCH, d, d)),
                  _const_spec((N_BRANCH, bw, d)), _const_spec((d, d)), _const_spec((1, d)), _const_spec((1, d))],
        out_specs=[row, row],
        out_shape=[jax.ShapeDtypeStruct((t, d), F32), jax.ShapeDtypeStruct((t, d), BF16)],
        compiler_params=_cparams(1),
        name="merge_ln",
    )(x, xb, ya, yb, yc, w_gate, w_branch, w_out, g.reshape(1, d), b.reshape(1, d))


def _split_mix_w_in(w_in):
    d = w_in.shape[0]
    o1 = RW_COLS
    o2 = o1 + DA_COLS
    o3 = o2 + GD_COLS
    gq = o2 + 3 * GD_WIDTH
    ab = jnp.zeros((d, GD_AB_PAD), w_in.dtype).at[:, 0:2 * GD_HEADS].set(w_in[:, gq:gq + 2 * GD_HEADS])
    w_cat = jnp.concatenate([w_in[:, 0:o2], w_in[:, o2:gq], ab, w_in[:, gq + 2 * GD_HEADS:o3]], axis=1)
    w_gate = w_in[:, o3:].reshape(d, N_BRANCH, d).transpose(1, 0, 2)
    return w_cat.astype(BF16), w_gate.astype(BF16)


def kernel(x, ffn1_w_in, ffn1_w_out, ln1_g, ln1_b, mix_w_in, rw_shift_mu, rw_w0, rw_w_up, rw_a0, rw_a_up, rw_g_up, rw_k_k, rw_k_a, rw_r_k, rw_ln_g, rw_ln_b, da_lam_q1, da_lam_k1, da_lam_q2, da_lam_k2, da_norm_g, gd_conv_w, gd_a_log, gd_dt_bias, gd_norm_g, mix_w_branch, mix_w_out, ln2_g, ln2_b, ffn2_w_in, ffn2_w_out, ln3_g, ln3_b):
    b, l, d = x.shape
    t = b * l
    xf = x.reshape(t, d)
    for i in range(DEPTH):
        xf, xb = _ffn_ln(xf, ffn1_w_in[i], ffn1_w_out[i], ln1_g[i], ln1_b[i])
        w_cat, w_gate = _split_mix_w_in(mix_w_in[i])
        h_rw, h_da, h_gd = _mix_in(xb, w_cat)
        ya = _rwkv7(h_rw.reshape(b, l, RW_COLS), rw_shift_mu[i], rw_w0[i], rw_w_up[i], rw_a0[i], rw_a_up[i],
                    rw_g_up[i], rw_k_k[i], rw_k_a[i], rw_r_k[i].reshape(-1), rw_ln_g[i], rw_ln_b[i])
        lam_init = 0.8 - 0.6 * math.exp(-0.3 * i)
        yb = _diff_attention(h_da.reshape(b, l, DA_COLS), da_lam_q1[i], da_lam_k1[i], da_lam_q2[i],
                             da_lam_k2[i], da_norm_g[i], lam_init)
        yc = _gated_deltanet(h_gd.reshape(b, l, GD_IN), gd_conv_w[i], gd_a_log[i], gd_dt_bias[i], gd_norm_g[i])
        xf, xb = _merge_ln(xf, xb, ya.reshape(t, RW_WIDTH), yb.reshape(t, DA_WIDTH), yc.reshape(t, GD_WIDTH),
                           w_gate, mix_w_branch[i].astype(BF16), mix_w_out[i].astype(BF16), ln2_g[i], ln2_b[i])
        xf, xb = _ffn_ln(xf, ffn2_w_in[i], ffn2_w_out[i], ln3_g[i], ln3_b[i])
    return xf.reshape(b, l, d)
```

```python
import functools
import math

import jax
import jax.numpy as jnp
from jax import lax
from jax.experimental import pallas as pl
from jax.experimental.pallas import tpu as pltpu

F32 = jnp.float32
BF16 = jnp.bfloat16

D_MODEL = 1024
DEPTH = 2
D_FF = 2816
RW_HEADS = 8
RW_HEAD_DIM = 64
RW_WIDTH = 512
RW_DECAY_LORA = 64
RW_ICLR_LORA = 64
RW_GATE_LORA = 128
RW_COLS = 3 * RW_WIDTH + RW_DECAY_LORA + RW_ICLR_LORA + RW_GATE_LORA
RW_GN_EPS = 64e-5
DA_HEADS = 4
DA_HEAD_DIM = 64
DA_WIDTH = 512
DA_COLS = 3 * DA_WIDTH
GD_HEADS = 4
GD_HEAD_DIM = 128
GD_WIDTH = 512
GD_CONV = 4
GD_COLS = 3 * GD_WIDTH + 2 * GD_HEADS + GD_WIDTH
N_BRANCH = 3
ALPHA = (2.0 * DEPTH) ** 0.25

LANES = 128
SUBLANES = 8
CHUNK = 64
PAIR = 2 * CHUNK
DA_ONES_ROWS = 16
LOG2E = math.log2(math.e)
GD_AB_PAD = LANES
GD_IN = 3 * GD_WIDTH + GD_AB_PAD + GD_WIDTH
VMEM_LIMIT = 56 * 1024 * 1024
SCAN_BATCHES = 2
SCAN_CHUNKS = 2


def _cparams(n_grid):
    return pltpu.CompilerParams(dimension_semantics=("arbitrary",) * n_grid,
                                vmem_limit_bytes=VMEM_LIMIT)


def _const_spec(shape):
    nd = len(shape)
    return pl.BlockSpec(shape, lambda *_: (0,) * nd, pipeline_mode=pl.Buffered(1))


def _mm(a, b):
    return jnp.dot(a.astype(BF16), b.astype(BF16), preferred_element_type=F32)


def _mm_nt(a, b):
    return lax.dot_general(a.astype(BF16), b.astype(BF16), (((1,), (1,)), ((), ())),
                           preferred_element_type=F32)


def _mm_tn(a, b):
    return lax.dot_general(a.astype(BF16), b.astype(BF16), (((0,), (0,)), ((), ())),
                           preferred_element_type=F32)


def _split_lhs_dot(x, exact_rhs, terms):
    acc = None
    rem = x
    for _ in range(terms):
        hi = rem.astype(BF16)
        part = jnp.dot(hi, exact_rhs, preferred_element_type=F32)
        acc = part if acc is None else acc + part
        rem = rem - hi.astype(F32)
    return acc


def _split_rhs_dot(exact_lhs, x, terms):
    acc = None
    rem = x
    for _ in range(terms):
        hi = rem.astype(BF16)
        part = jnp.dot(exact_lhs, hi, preferred_element_type=F32)
        acc = part if acc is None else acc + part
        rem = rem - hi.astype(F32)
    return acc


def _layernorm(z, g, b, eps=1e-5):
    mu = jnp.mean(z, axis=-1, keepdims=True)
    zc = z - mu
    var = jnp.mean(zc * zc, axis=-1, keepdims=True)
    return zc * lax.rsqrt(var + eps) * g + b


def _iota(shape, dim):
    return lax.broadcasted_iota(jnp.int32, shape, dim)


def _inv_unit_lower_many(lows):
    eye = (_iota((PAIR, PAIR), 0) == _iota((PAIR, PAIR), 1)).astype(F32)
    pw = [-low for low in lows]
    inv = [eye + n for n in pw]
    for _ in range(int(math.log2(CHUNK)) - 1):
        pw = [_mm(p, p) for p in pw]
        inv = [i + _mm(i, p) for i, p in zip(inv, pw)]
    return inv


def _stack_heads(x, lo_mask):
    return jnp.concatenate([jnp.where(lo_mask, x, 0.0), jnp.where(lo_mask, 0.0, x)], axis=0)


def _chunk_tril(n_chunks):
    r = jnp.arange(n_chunks * CHUNK)
    return ((r[:, None] >= r[None, :]) & (r[:, None] // CHUNK == r[None, :] // CHUNK)).astype(BF16)


def _rows_of_chunk(x, row):
    n = x.shape[0] // CHUNK
    parts = [jnp.broadcast_to(x[i * CHUNK + row:i * CHUNK + row + 1], (CHUNK,) + x.shape[1:]) for i in range(n)]
    return jnp.concatenate(parts, axis=0) if n > 1 else parts[0]


def _ffn_ln_kernel(x_ref, wg_ref, wu_ref, wo_ref, g_ref, b_ref, y_ref, yb_ref, acc_ref, *, nf):
    x = x_ref[...]
    xb = x.astype(BF16)
    acc_ref[...] = jnp.zeros_like(acc_ref)

    def body(f, carry):
        gate = jnp.dot(xb, wg_ref[f], preferred_element_type=F32)
        up = jnp.dot(xb, wu_ref[f], preferred_element_type=F32)
        act = (gate * jax.nn.sigmoid(gate) * up).astype(BF16)
        acc_ref[...] += jnp.dot(act, wo_ref[f], preferred_element_type=F32)
        return carry

    lax.fori_loop(0, nf, body, 0)
    y = _layernorm(ALPHA * x + 0.5 * acc_ref[...], g_ref[...], b_ref[...])
    y_ref[...] = y
    yb_ref[...] = y.astype(BF16)


def _ffn_ln(x, w_in, w_out, g, b, *, tm=1024, tf=256):
    t, d = x.shape
    dff = w_out.shape[0]
    nf = dff // tf
    assert nf * tf == dff and t % tm == 0
    wg = w_in[:, :dff].astype(BF16).reshape(d, nf, tf).transpose(1, 0, 2)
    wu = w_in[:, dff:].astype(BF16).reshape(d, nf, tf).transpose(1, 0, 2)
    wo = w_out.astype(BF16).reshape(nf, tf, d)
    row = pl.BlockSpec((tm, d), lambda i: (i, 0))
    return pl.pallas_call(
        functools.partial(_ffn_ln_kernel, nf=nf),
        grid=(t // tm,),
        in_specs=[row, _const_spec((nf, d, tf)), _const_spec((nf, d, tf)), _const_spec((nf, tf, d)),
                  _const_spec((1, d)), _const_spec((1, d))],
        out_specs=[row, row],
        out_shape=[jax.ShapeDtypeStruct((t, d), F32), jax.ShapeDtypeStruct((t, d), BF16)],
        scratch_shapes=[pltpu.VMEM((tm, d), F32)],
        compiler_params=_cparams(1),
        name="ffn_ln",
    )(x, wg, wu, wo, g.reshape(1, d), b.reshape(1, d))


def _mix_in_kernel(xb_ref, w_ref, rw_ref, da_ref, gd_ref, *, tn):
    xb = xb_ref[...]
    off = 0
    for out_ref in (rw_ref, da_ref, gd_ref):
        width = out_ref.shape[-1]
        for c0 in range(0, width, tn):
            c1 = min(c0 + tn, width)
            res = jnp.dot(xb, w_ref[:, off + c0:off + c1], preferred_element_type=F32)
            out_ref[:, c0:c1] = res.astype(out_ref.dtype)
        off += width


def _mix_in(xb, w_cat, *, tm=512, tn=256):
    t, d = xb.shape
    n = w_cat.shape[1]
    assert n == RW_COLS + DA_COLS + GD_IN and t % tm == 0
    return pl.pallas_call(
        functools.partial(_mix_in_kernel, tn=tn),
        grid=(t // tm,),
        in_specs=[pl.BlockSpec((tm, d), lambda i: (i, 0)), _const_spec((d, n))],
        out_specs=[pl.BlockSpec((tm, RW_COLS), lambda i: (i, 0)),
                   pl.BlockSpec((tm, DA_COLS), lambda i: (i, 0)),
                   pl.BlockSpec((tm, GD_IN), lambda i: (i, 0))],
        out_shape=[jax.ShapeDtypeStruct((t, RW_COLS), F32),
                   jax.ShapeDtypeStruct((t, DA_COLS), BF16),
                   jax.ShapeDtypeStruct((t, GD_IN), F32)],
        compiler_params=_cparams(1),
        name="mix_in",
    )(xb, w_cat)


def _rwkv_kernel(h_ref, mu_ref, w0_ref, a0_ref, kk_ref, ka_ref, rk_ref, lng_ref, lnb_ref,
                 lora_ref, gup_ref, bd_ref, tril_ref, y_ref, prev_ref, s_ref, *, nb, nc):
    @pl.when(pl.program_id(1) == 0)
    def _():
        prev_ref[...] = jnp.zeros_like(prev_ref)
        s_ref[...] = jnp.zeros_like(s_ref)

    rows_b = nc * CHUNK
    n_pairs = RW_HEADS // 2
    parts = []
    for b in range(nb):
        h = h_ref[b]
        hcat = jnp.concatenate([prev_ref[b], h], axis=0)
        hprev = pltpu.roll(hcat, 1, 0)[SUBLANES:SUBLANES + rows_b]
        prev_ref[b] = h[rows_b - SUBLANES:rows_b]
        parts.append(h + mu_ref[...] * (hprev - h))
    hs = jnp.concatenate(parts, axis=0) if nb > 1 else parts[0]
    n_rows = nb * rows_b

    w = RW_WIDTH
    r = hs[:, 0:w]
    k = hs[:, w:2 * w]
    v = hs[:, 2 * w:3 * w]
    wa = hs[:, 3 * w:3 * w + LANES]
    gd = hs[:, 3 * w + LANES:3 * w + 2 * LANES]
    lo_n = _iota((n_rows, LANES), 1) < RW_HEAD_DIM
    lora = _mm(jnp.where(lo_n, jnp.tanh(wa), wa), lora_ref[...])
    logw = -math.exp(-0.5) * jax.nn.sigmoid(w0_ref[...] + lora[:, 0:w])
    a = jax.nn.sigmoid(a0_ref[...] + lora[:, w:2 * w])
    g = _mm(jax.nn.sigmoid(gd), gup_ref[...])

    bd = bd_ref[...]
    kkr = k * kk_ref[...]
    kk = kkr * lax.rsqrt(_split_lhs_dot(kkr * kkr, bd, 2) + 1e-6)
    k2 = k * (1.0 + (a - 1.0) * ka_ref[...])
    bvec = kk * a
    bonus = _split_lhs_dot(r * k2 * rk_ref[...], bd, 2) * v

    e_in = _split_rhs_dot(tril_ref[...], logw, 3)
    rho = _rows_of_chunk(e_in, CHUNK // 2 - 1)
    r_t = r * jnp.exp(e_in - rho)
    kk_t = kk * jnp.exp(e_in - logw - rho)
    e_neg = jnp.exp(rho - e_in)
    k_h = k2 * e_neg
    b_h = bvec * e_neg

    lane = _iota((CHUNK, LANES), 1)
    lo = lane < RW_HEAD_DIM
    tt = _iota((CHUNK, LANES), 0)
    ss = lane & (CHUNK - 1)
    strict = tt > ss
    incl = tt >= ss

    units = [(b, ci, p) for b in range(nb) for ci in range(nc) for p in range(n_pairs)]

    def tile(x, u):
        b, ci, p = u
        r0 = (b * nc + ci) * CHUNK
        return x[r0:r0 + CHUNK, p * LANES:(p + 1) * LANES]

    kks = [_stack_heads(tile(kk_t, u), lo) for u in units]
    rs = [_stack_heads(tile(r_t, u), lo) for u in units]
    vs = [_stack_heads(tile(v, u), lo) for u in units]
    khs = [_stack_heads(tile(k_h, u), lo) for u in units]
    bhs = [_stack_heads(tile(b_h, u), lo) for u in units]
    z = [_mm_nt(jnp.concatenate([tile(kk_t, u), tile(r_t, u)], axis=0), jnp.concatenate([kh, bh], axis=0))
         for u, kh, bh in zip(units, khs, bhs)]
    ak = [_stack_heads(jnp.where(strict, zz[0:CHUNK, 0:PAIR], 0.0), lo) for zz in z]
    ab = [_stack_heads(jnp.where(strict, zz[0:CHUNK, PAIR:2 * PAIR], 0.0), lo) for zz in z]
    bk = [_stack_heads(jnp.where(incl, zz[CHUNK:PAIR, 0:PAIR], 0.0), lo) for zz in z]
    bb = [_stack_heads(jnp.where(incl, zz[CHUNK:PAIR, PAIR:2 * PAIR], 0.0), lo) for zz in z]
    inv = _inv_unit_lower_many(ab)
    akv = [_mm(x, y) for x, y in zip(ak, vs)]
    bkv = [_mm(x, y) for x, y in zip(bk, vs)]
    pq = [_mm(i, jnp.concatenate([x, y], axis=1)) for i, x, y in zip(inv, kks, akv)]
    bpq = [_mm(x, y) for x, y in zip(bb, pq)]
    rp = [x - y[:, 0:PAIR] for x, y in zip(rs, bpq)]
    y0 = [x - y[:, PAIR:2 * PAIR] for x, y in zip(bkv, bpq)]
    ptb = [_mm_tn(x[:, 0:PAIR], y) for x, y in zip(pq, bhs)]
    hm = [_mm_tn(jnp.concatenate([x, y[:, PAIR:2 * PAIR]], axis=0), jnp.concatenate([kh, -bh], axis=0))
          for x, y, kh, bh in zip(vs, pq, khs, bhs)]

    s_scale = jnp.exp(rho)
    d_out = jnp.exp(_rows_of_chunk(e_in, CHUNK - 1) - rho)
    yo = {}
    for b in range(nb):
        for p in range(n_pairs):
            s = s_ref[b * n_pairs + p]
            for ci in range(nc):
                i = units.index((b, ci, p))
                r0 = (b * nc + ci) * CHUNK
                sl = slice(p * LANES, (p + 1) * LANES)
                sp = s * s_scale[r0:r0 + 1, sl]
                yo[(b, ci, p)] = _mm_nt(rp[i], sp) + y0[i]
                s = (sp - _mm(sp, ptb[i]) + hm[i]) * d_out[r0:r0 + 1, sl]
            s_ref[b * n_pairs + p] = s
    y = jnp.concatenate(
        [jnp.concatenate([yo[(b, ci, p)][0:CHUNK] + yo[(b, ci, p)][CHUNK:PAIR] for p in range(n_pairs)], axis=1)
         for b in range(nb) for ci in range(nc)], axis=0)

    inv_n = 1.0 / RW_HEAD_DIM
    ym = _split_lhs_dot(y, bd, 2) * inv_n
    yc = y - ym
    yv = _split_lhs_dot(yc * yc, bd, 2) * inv_n
    yn = yc * lax.rsqrt(yv + RW_GN_EPS) * lng_ref[...] + lnb_ref[...]
    out = ((yn + bonus) * g).astype(y_ref.dtype)
    for b in range(nb):
        y_ref[b] = out[b * rows_b:(b + 1) * rows_b]


def _rwkv7(h_rw, mu, w0, w_up, a0, a_up, g_up, k_k, k_a, r_k, ln_g, ln_b, *, nb=SCAN_BATCHES, nc=SCAN_CHUNKS):
    b, l, _ = h_rw.shape
    w = RW_WIDTH
    nb = min(nb, b)
    rows_b = nc * CHUNK
    assert l % rows_b == 0 and b % nb == 0
    lora = jnp.zeros((LANES, 2 * w), F32)
    lora = lora.at[0:RW_DECAY_LORA, 0:w].set(w_up).at[RW_DECAY_LORA:LANES, w:2 * w].set(a_up)
    hid = jnp.arange(w) // RW_HEAD_DIM
    bd = (hid[:, None] == hid[None, :]).astype(BF16)
    vec = lambda t: t.reshape(1, -1).astype(F32)
    params = [vec(mu), vec(w0), vec(a0), vec(k_k), vec(k_a), vec(r_k), vec(ln_g), vec(ln_b),
              lora.astype(BF16), g_up.astype(BF16), bd, _chunk_tril(nb * nc)]
    return pl.pallas_call(
        functools.partial(_rwkv_kernel, nb=nb, nc=nc),
        grid=(b // nb, l // rows_b),
        in_specs=[pl.BlockSpec((nb, rows_b, RW_COLS), lambda i, j: (i, j, 0))]
        + [_const_spec(p.shape) for p in params],
        out_specs=pl.BlockSpec((nb, rows_b, w), lambda i, j: (i, j, 0)),
        out_shape=jax.ShapeDtypeStruct((b, l, w), BF16),
        scratch_shapes=[pltpu.VMEM((nb, SUBLANES, RW_COLS), F32),
                        pltpu.VMEM((nb * RW_HEADS // 2, PAIR, PAIR), F32)],
        compiler_params=_cparams(2),
        name="rwkv7",
    )(h_rw, *params)


def _attn_kernel(slope_ref, lam_ref, ng_ref, q_ref, k_ref, v_ref, o_ref, ka_ref, vt_ref, s0_ref, s1_ref, m_ref,
                 acc_ref, *, tq, tk, lam_init):
    qi = pl.program_id(2)
    d = DA_HEAD_DIM
    dv = 2 * d
    slope2 = slope_ref[0, 0:1, 0:1] * LOG2E

    @pl.when(qi == 0)
    def _():
        lane = _iota((tk, LANES), 1)
        bias = slope2 * _iota((tk, LANES), 0).astype(F32)
        hi = bias.astype(BF16).astype(F32)
        mid = (bias - hi).astype(BF16).astype(F32)
        low = bias - hi - mid
        pieces = jnp.where(lane == 0, hi, jnp.where(lane == 1, mid, jnp.where(lane == 2, low, 0.0))).astype(BF16)
        ones_row = (_iota((DA_ONES_ROWS, tk), 0) == 0).astype(BF16)
        for jb in range(vt_ref.shape[0]):
            rows = slice(jb * tk, (jb + 1) * tk)
            ka_ref[rows, 0:LANES] = k_ref[0, rows, :]
            ka_ref[rows, LANES:2 * LANES] = pieces
            vt_ref[jb, 0:dv, :] = v_ref[0, rows, :].astype(F32).T.astype(BF16)
            vt_ref[jb, dv:dv + DA_ONES_ROWS, :] = ones_row

    q = q_ref[0].astype(F32) * (d ** -0.5 * LOG2E)
    lane = _iota((tq, LANES), 1)
    lo = lane < d
    bias_ones = (lane < 3).astype(BF16)
    q2 = jnp.concatenate([
        jnp.concatenate([jnp.where(lo, q, 0.0).astype(BF16), bias_ones], axis=1),
        jnp.concatenate([jnp.where(lo, 0.0, q).astype(BF16), bias_ones], axis=1)], axis=0)
    m_ref[...] = jnp.full_like(m_ref, -jnp.inf)
    acc_ref[...] = jnp.zeros_like(acc_ref)

    def scores(j, dst_ref):
        kb = ka_ref[pl.ds(j * tk, tk), :]
        dst_ref[...] = lax.dot_general(kb, q2, (((1,), (1,)), ((), ())), preferred_element_type=F32)

    def consume(j, src_ref, masked):
        shift = slope2 * (j * tk - qi * tq).astype(F32)
        s = src_ref[...]
        if masked:
            s = jnp.where(_iota((tk, 2 * tq), 0) <= (_iota((tk, 2 * tq), 1) & (tq - 1)), s, -jnp.inf)
        m_old = m_ref[...]
        m_new = jnp.maximum(m_old, jnp.max(s, axis=0, keepdims=True) + shift)
        p = jnp.exp2(s - (m_new - shift)).astype(BF16)
        acc_ref[...] = (jnp.exp2(m_old - m_new) * acc_ref[...]
                        + jnp.dot(vt_ref[j], p, preferred_element_type=F32))
        m_ref[...] = m_new

    nfull = (qi * tq) // tk
    scores(0, s0_ref)

    def pair(jj, carry):
        j = 2 * jj
        scores(j + 1, s1_ref)
        consume(j, s0_ref, False)
        scores(j + 2, s0_ref)
        consume(j + 1, s1_ref, False)
        return carry

    lax.fori_loop(0, nfull // 2, pair, 0)

    @pl.when(nfull % 2 == 1)
    def _():
        scores(nfull, s1_ref)
        consume(nfull - 1, s0_ref, False)
        consume(nfull, s1_ref, True)

    @pl.when(nfull % 2 == 0)
    def _():
        consume(nfull, s0_ref, True)

    lp = lam_ref[...]
    lam = (jnp.exp(jnp.sum(lp[0:1] * lp[1:2], axis=-1, keepdims=True))
           - jnp.exp(jnp.sum(lp[2:3] * lp[3:4], axis=-1, keepdims=True)) + lam_init)
    acc = acc_ref[...]
    ot = acc[0:dv] / acc[dv:dv + 1]
    ot = ot[:, 0:tq] - lam * ot[:, tq:2 * tq]
    ot = ot * lax.rsqrt(jnp.mean(ot * ot, axis=0, keepdims=True) + 1e-5)
    o_ref[0] = (ot.T * ng_ref[...] * (1.0 - lam_init)).astype(o_ref.dtype)


def _diff_attention(h_da, lam_q1, lam_k1, lam_q2, lam_k2, norm_g, lam_init, *, tq=512):
    b, l, _ = h_da.shape
    tq = min(tq, l)
    tk = tq
    assert l % tq == 0 and tq & (tq - 1) == 0
    hh = DA_HEADS
    slopes = jnp.exp2(-8.0 * jnp.arange(1, hh + 1, dtype=F32) / hh)
    slopes = jnp.broadcast_to(slopes[:, None, None], (hh, SUBLANES, LANES))
    lam_p = jnp.stack([lam_q1, lam_k1, lam_q2, lam_k2]).astype(F32)
    ng = norm_g.reshape(1, 2 * DA_HEAD_DIM).astype(F32)
    return pl.pallas_call(
        functools.partial(_attn_kernel, tq=tq, tk=tk, lam_init=lam_init),
        grid=(b, hh, l // tq),
        in_specs=[pl.BlockSpec((1, SUBLANES, LANES), lambda i, h, j: (h, 0, 0)),
                  pl.BlockSpec(lam_p.shape, lambda i, h, j: (0, 0)),
                  pl.BlockSpec(ng.shape, lambda i, h, j: (0, 0)),
                  pl.BlockSpec((1, tq, LANES), lambda i, h, j: (i, j, h)),
                  pl.BlockSpec((1, l, LANES), lambda i, h, j: (i, 0, hh + h)),
                  pl.BlockSpec((1, l, LANES), lambda i, h, j: (i, 0, 2 * hh + h))],
        out_specs=pl.BlockSpec((1, tq, LANES), lambda i, h, j: (i, j, h)),
        out_shape=jax.ShapeDtypeStruct((b, l, DA_WIDTH), BF16),
        scratch_shapes=[pltpu.VMEM((l, 2 * LANES), BF16),
                        pltpu.VMEM((l // tk, LANES + DA_ONES_ROWS, tk), BF16),
                        pltpu.VMEM((tk, 2 * tq), F32), pltpu.VMEM((tk, 2 * tq), F32),
                        pltpu.VMEM((1, 2 * tq), F32), pltpu.VMEM((LANES + DA_ONES_ROWS, 2 * tq), F32)],
        compiler_params=_cparams(3),
        name="diffattn",
    )(slopes, lam_p, ng, h_da, h_da, h_da)


def _softplus(x):
    return jnp.maximum(x, 0.0) + jnp.log(1.0 + jnp.exp(-jnp.abs(x)))


def _gdn_kernel(h_ref, cw_ref, alog_ref, dtb_ref, ng_ref, tril_ref, y_ref, prev_ref, s_ref, *, nb, nc):
    @pl.when(pl.program_id(1) == 0)
    def _():
        prev_ref[...] = jnp.zeros_like(prev_ref)
        s_ref[...] = jnp.zeros_like(s_ref)

    rows_b = nc * CHUNK
    wq = 3 * GD_WIDTH
    dh = GD_HEAD_DIM
    cw = cw_ref[...]
    parts = []
    for b in range(nb):
        x = h_ref[b, :, 0:wq]
        xcat = jnp.concatenate([prev_ref[b], x], axis=0)
        prev_ref[b] = x[rows_b - SUBLANES:rows_b]
        conv = x * cw[GD_CONV - 1:GD_CONV]
        for j in range(GD_CONV - 1):
            conv = conv + pltpu.roll(xcat, GD_CONV - 1 - j, 0)[SUBLANES:SUBLANES + rows_b] * cw[j:j + 1]
        parts.append(conv)
    conv = jnp.concatenate(parts, axis=0) if nb > 1 else parts[0]
    qkv = conv * jax.nn.sigmoid(conv)

    ab = jnp.concatenate([h_ref[b, :, wq:wq + GD_AB_PAD] for b in range(nb)], axis=0)
    gfull = -jnp.exp(alog_ref[...]) * _softplus(ab + dtb_ref[...])
    gcum = _split_rhs_dot(tril_ref[...], gfull, 3)
    glast = _rows_of_chunk(gcum, CHUNK - 1)
    beta_full = jax.nn.sigmoid(ab)

    def l2n(t):
        return t * lax.rsqrt(jnp.sum(t * t, axis=-1, keepdims=True) + 1e-6)

    qn = [l2n(qkv[:, h * dh:(h + 1) * dh]) * dh ** -0.5 for h in range(GD_HEADS)]
    kn = [l2n(qkv[:, GD_WIDTH + h * dh:GD_WIDTH + (h + 1) * dh]) for h in range(GD_HEADS)]
    vh = [qkv[:, 2 * GD_WIDTH + h * dh:2 * GD_WIDTH + (h + 1) * dh] for h in range(GD_HEADS)]

    r0 = _iota((PAIR, PAIR), 0)
    c0 = _iota((PAIR, PAIR), 1)
    same = (r0 >= CHUNK) == (c0 >= CHUNK)
    incl = same & (r0 >= c0)
    strict = same & (r0 > c0)

    n_pairs = GD_HEADS // 2
    units = [(b, ci, p) for b in range(nb) for ci in range(nc) for p in range(n_pairs)]

    def stack(per_head, u, lane_of=None):
        b, ci, p = u
        rr = slice((b * nc + ci) * CHUNK, (b * nc + ci + 1) * CHUNK)
        if lane_of is None:
            return jnp.concatenate([per_head[2 * p][rr], per_head[2 * p + 1][rr]], axis=0)
        return jnp.concatenate([per_head[rr, lane_of + 2 * p:lane_of + 2 * p + 1],
                                per_head[rr, lane_of + 2 * p + 1:lane_of + 2 * p + 2]], axis=0)

    q_s = [stack(qn, u) for u in units]
    k_s = [stack(kn, u) for u in units]
    v_s = [stack(vh, u) for u in units]
    g_s = [stack(gcum, u, 0) for u in units]
    gl_s = [stack(glast, u, 0) for u in units]
    beta_s = [stack(beta_full, u, GD_HEADS) for u in units]

    decay = []
    for gs in g_s:
        g_b = jnp.broadcast_to(gs, (PAIR, PAIR))
        decay.append(jnp.exp(jnp.where(incl, g_b - g_b.T, -jnp.inf)))
    kb = [x * y for x, y in zip(k_s, beta_s)]
    kk = [_mm_nt(jnp.concatenate([x, y], axis=0), z) for x, y, z in zip(kb, q_s, k_s)]
    low = [jnp.where(strict, x[0:PAIR] * dc, 0.0) for x, dc in zip(kk, decay)]
    intra = [x[PAIR:2 * PAIR] * dc for x, dc in zip(kk, decay)]
    inv = _inv_unit_lower_many(low)
    eg = [jnp.exp(gs) for gs in g_s]
    uw = [_mm(i, jnp.concatenate([v * bt, x * e], axis=1)) for i, v, bt, x, e in zip(inv, v_s, beta_s, kb, eg)]
    qg = [x * e for x, e in zip(q_s, eg)]
    kd = [x * jnp.exp(gl - gs) for x, gl, gs in zip(k_s, gl_s, g_s)]

    v_new = {}
    o_state = {}
    for b in range(nb):
        for h in range(GD_HEADS):
            p, i_h = divmod(h, 2)
            rows = slice(i_h * CHUNK, (i_h + 1) * CHUNK)
            s = s_ref[b * GD_HEADS + h]
            for ci in range(nc):
                i = units.index((b, ci, p))
                ws = _mm(jnp.concatenate([uw[i][rows, dh:2 * dh], qg[i][rows]], axis=0), s)
                vn = uw[i][rows, 0:dh] - ws[0:CHUNK]
                v_new[(b, ci, h)] = vn
                o_state[(b, ci, h)] = ws[CHUNK:PAIR]
                s = s * jnp.exp(gl_s[i][rows][0:1]) + _mm_tn(kd[i][rows], vn)
            s_ref[b * GD_HEADS + h] = s

    for i, (b, ci, p) in enumerate(units):
        vn = jnp.concatenate([v_new[(b, ci, 2 * p)], v_new[(b, ci, 2 * p + 1)]], axis=0)
        o = jnp.concatenate([o_state[(b, ci, 2 * p)], o_state[(b, ci, 2 * p + 1)]], axis=0) + _mm(intra[i], vn)
        for i_h in range(2):
            h = 2 * p + i_h
            oh = o[i_h * CHUNK:(i_h + 1) * CHUNK]
            oh = oh * lax.rsqrt(jnp.mean(oh * oh, axis=-1, keepdims=True) + 1e-6) * ng_ref[...]
            zc = wq + GD_AB_PAD + h * dh
            z = h_ref[b, ci * CHUNK:(ci + 1) * CHUNK, zc:zc + dh]
            y_ref[b, ci * CHUNK:(ci + 1) * CHUNK, h * dh:(h + 1) * dh] = (
                oh * (z * jax.nn.sigmoid(z))).astype(y_ref.dtype)


def _gated_deltanet(h_gd, conv_w, a_log, dt_bias, norm_g, *, nb=SCAN_BATCHES, nc=SCAN_CHUNKS):
    b, l, _ = h_gd.shape
    nb = min(nb, b)
    rows_b = nc * CHUNK
    assert l % rows_b == 0 and b % nb == 0
    pad = lambda t: jnp.zeros((1, GD_AB_PAD), F32).at[0, 0:GD_HEADS].set(t.astype(F32))
    params = [conv_w.astype(F32), pad(a_log), pad(dt_bias), norm_g.reshape(1, GD_HEAD_DIM).astype(F32),
              _chunk_tril(nb * nc)]
    return pl.pallas_call(
        functools.partial(_gdn_kernel, nb=nb, nc=nc),
        grid=(b // nb, l // rows_b),
        in_specs=[pl.BlockSpec((nb, rows_b, GD_IN), lambda i, j: (i, j, 0))]
        + [_const_spec(p.shape) for p in params],
        out_specs=pl.BlockSpec((nb, rows_b, GD_WIDTH), lambda i, j: (i, j, 0)),
        out_shape=jax.ShapeDtypeStruct((b, l, GD_WIDTH), BF16),
        scratch_shapes=[pltpu.VMEM((nb, SUBLANES, 3 * GD_WIDTH), F32),
                        pltpu.VMEM((nb * GD_HEADS, GD_HEAD_DIM, GD_HEAD_DIM), F32)],
        compiler_params=_cparams(2),
        name="gdn",
    )(h_gd, *params)


def _merge_ln_kernel(x_ref, xb_ref, ya_ref, yb_ref, yc_ref, wgate_ref, wbr_ref, wout_ref, g_ref, b_ref,
                     y_ref, y16_ref):
    xb = xb_ref[...]
    merged = None
    for n, br_ref in enumerate((ya_ref, yb_ref, yc_ref)):
        gate = jax.nn.sigmoid(jnp.dot(xb, wgate_ref[n], preferred_element_type=F32))
        term = gate * jnp.dot(br_ref[...], wbr_ref[n], preferred_element_type=F32)
        merged = term if merged is None else merged + term
    mix = jnp.dot(merged.astype(BF16), wout_ref[...], preferred_element_type=F32)
    y = _layernorm(ALPHA * x_ref[...] + mix, g_ref[...], b_ref[...])
    y_ref[...] = y
    y16_ref[...] = y.astype(BF16)


def _merge_ln(x, xb, ya, yb, yc, w_gate, w_branch, w_out, g, b, *, tm=256):
    t, d = x.shape
    bw = ya.shape[-1]
    row = pl.BlockSpec((tm, d), lambda i: (i, 0))
    brow = pl.BlockSpec((tm, bw), lambda i: (i, 0))
    return pl.pallas_call(
        _merge_ln_kernel,
        grid=(t // tm,),
        in_specs=[row, row, brow, brow, brow, _const_spec((N_BRANCH, d, d)),
                  _const_spec((N_BRANCH, bw, d)), _const_spec((d, d)), _const_spec((1, d)), _const_spec((1, d))],
        out_specs=[row, row],
        out_shape=[jax.ShapeDtypeStruct((t, d), F32), jax.ShapeDtypeStruct((t, d), BF16)],
        compiler_params=_cparams(1),
        name="merge_ln",
    )(x, xb, ya, yb, yc, w_gate, w_branch, w_out, g.reshape(1, d), b.reshape(1, d))


def _split_mix_w_in(w_in):
    d = w_in.shape[0]
    o1 = RW_COLS
    o2 = o1 + DA_COLS
    o3 = o2 + GD_COLS
    gq = o2 + 3 * GD_WIDTH
    ab = jnp.zeros((d, GD_AB_PAD), w_in.dtype).at[:, 0:2 * GD_HEADS].set(w_in[:, gq:gq + 2 * GD_HEADS])
    w_cat = jnp.concatenate([w_in[:, 0:o2], w_in[:, o2:gq], ab, w_in[:, gq + 2 * GD_HEADS:o3]], axis=1)
    w_gate = w_in[:, o3:].reshape(d, N_BRANCH, d).transpose(1, 0, 2)
    return w_cat.astype(BF16), w_gate.astype(BF16)


def kernel(x, ffn1_w_in, ffn1_w_out, ln1_g, ln1_b, mix_w_in, rw_shift_mu, rw_w0, rw_w_up, rw_a0, rw_a_up, rw_g_up, rw_k_k, rw_k_a, rw_r_k, rw_ln_g, rw_ln_b, da_lam_q1, da_lam_k1, da_lam_q2, da_lam_k2, da_norm_g, gd_conv_w, gd_a_log, gd_dt_bias, gd_norm_g, mix_w_branch, mix_w_out, ln2_g, ln2_b, ffn2_w_in, ffn2_w_out, ln3_g, ln3_b):
    b, l, d = x.shape
    t = b * l
    xf = x.reshape(t, d)
    for i in range(DEPTH):
        xf, xb = _ffn_ln(xf, ffn1_w_in[i], ffn1_w_out[i], ln1_g[i], ln1_b[i])
        w_cat, w_gate = _split_mix_w_in(mix_w_in[i])
        h_rw, h_da, h_gd = _mix_in(xb, w_cat)
        ya = _rwkv7(h_rw.reshape(b, l, RW_COLS), rw_shift_mu[i], rw_w0[i], rw_w_up[i], rw_a0[i], rw_a_up[i],
                    rw_g_up[i], rw_k_k[i], rw_k_a[i], rw_r_k[i].reshape(-1), rw_ln_g[i], rw_ln_b[i])
        lam_init = 0.8 - 0.6 * math.exp(-0.3 * i)
        yb = _diff_attention(h_da.reshape(b, l, DA_COLS), da_lam_q1[i], da_lam_k1[i], da_lam_q2[i],
                             da_lam_k2[i], da_norm_g[i], lam_init)
        yc = _gated_deltanet(h_gd.reshape(b, l, GD_IN), gd_conv_w[i], gd_a_log[i], gd_dt_bias[i], gd_norm_g[i])
        xf, xb = _merge_ln(xf, xb, ya.reshape(t, RW_WIDTH), yb.reshape(t, DA_WIDTH), yc.reshape(t, GD_WIDTH),
                           w_gate, mix_w_branch[i].astype(BF16), mix_w_out[i].astype(BF16), ln2_g[i], ln2_b[i])
        xf, xb = _ffn_ln(xf, ffn2_w_in[i], ffn2_w_out[i], ln3_g[i], ln3_b[i])
    return xf.reshape(b, l, d)
```

```python
import functools
import math

import jax
import jax.numpy as jnp
from jax import lax
from jax.experimental import pallas as pl
from jax.experimental.pallas import tpu as pltpu

F32 = jnp.float32
BF16 = jnp.bfloat16

D_MODEL = 1024
DEPTH = 2
D_FF = 2816
RW_HEADS = 8
RW_HEAD_DIM = 64
RW_WIDTH = 512
RW_DECAY_LORA = 64
RW_ICLR_LORA = 64
RW_GATE_LORA = 128
RW_COLS = 3 * RW_WIDTH + RW_DECAY_LORA + RW_ICLR_LORA + RW_GATE_LORA
RW_GN_EPS = 64e-5
DA_HEADS = 4
DA_HEAD_DIM = 64
DA_WIDTH = 512
DA_COLS = 3 * DA_WIDTH
GD_HEADS = 4
GD_HEAD_DIM = 128
GD_WIDTH = 512
GD_CONV = 4
GD_COLS = 3 * GD_WIDTH + 2 * GD_HEADS + GD_WIDTH
N_BRANCH = 3
ALPHA = (2.0 * DEPTH) ** 0.25

LANES = 128
SUBLANES = 8
CHUNK = 64
PAIR = 2 * CHUNK
DA_ONES_ROWS = 16
LOG2E = math.log2(math.e)
GD_AB_PAD = LANES
GD_IN = 3 * GD_WIDTH + GD_AB_PAD + GD_WIDTH
VMEM_LIMIT = 56 * 1024 * 1024
SCAN_BATCHES = 2
SCAN_CHUNKS = 2


def _cparams(n_grid):
    return pltpu.CompilerParams(dimension_semantics=("arbitrary",) * n_grid,
                                vmem_limit_bytes=VMEM_LIMIT)


def _const_spec(shape):
    nd = len(shape)
    return pl.BlockSpec(shape, lambda *_: (0,) * nd, pipeline_mode=pl.Buffered(1))


def _mm(a, b):
    return jnp.dot(a.astype(BF16), b.astype(BF16), preferred_element_type=F32)


def _mm_nt(a, b):
    return lax.dot_general(a.astype(BF16), b.astype(BF16), (((1,), (1,)), ((), ())),
                           preferred_element_type=F32)


def _mm_tn(a, b):
    return lax.dot_general(a.astype(BF16), b.astype(BF16), (((0,), (0,)), ((), ())),
                           preferred_element_type=F32)


def _split_lhs_dot(x, exact_rhs, terms):
    acc = None
    rem = x
    for _ in range(terms):
        hi = rem.astype(BF16)
        part = jnp.dot(hi, exact_rhs, preferred_element_type=F32)
        acc = part if acc is None else acc + part
        rem = rem - hi.astype(F32)
    return acc


def _split_rhs_dot(exact_lhs, x, terms):
    acc = None
    rem = x
    for _ in range(terms):
        hi = rem.astype(BF16)
        part = jnp.dot(exact_lhs, hi, preferred_element_type=F32)
        acc = part if acc is None else acc + part
        rem = rem - hi.astype(F32)
    return acc


def _layernorm(z, g, b, eps=1e-5):
    mu = jnp.mean(z, axis=-1, keepdims=True)
    zc = z - mu
    var = jnp.mean(zc * zc, axis=-1, keepdims=True)
    return zc * lax.rsqrt(var + eps) * g + b


def _iota(shape, dim):
    return lax.broadcasted_iota(jnp.int32, shape, dim)


def _inv_unit_lower_many(lows):
    eye = (_iota((PAIR, PAIR), 0) == _iota((PAIR, PAIR), 1)).astype(F32)
    pw = [-low for low in lows]
    inv = [eye + n for n in pw]
    for _ in range(int(math.log2(CHUNK)) - 1):
        pw = [_mm(p, p) for p in pw]
        inv = [i + _mm(i, p) for i, p in zip(inv, pw)]
    return inv


def _stack_heads(x, lo_mask):
    return jnp.concatenate([jnp.where(lo_mask, x, 0.0), jnp.where(lo_mask, 0.0, x)], axis=0)


def _chunk_tril(n_chunks):
    r = jnp.arange(n_chunks * CHUNK)
    return ((r[:, None] >= r[None, :]) & (r[:, None] // CHUNK == r[None, :] // CHUNK)).astype(BF16)


def _rows_of_chunk(x, row):
    n = x.shape[0] // CHUNK
    parts = [jnp.broadcast_to(x[i * CHUNK + row:i * CHUNK + row + 1], (CHUNK,) + x.shape[1:]) for i in range(n)]
    return jnp.concatenate(parts, axis=0) if n > 1 else parts[0]


def _ffn_ln_kernel(x_ref, xb_ref, win_ref, wo_ref, g_ref, b_ref, y_ref, yb_ref, acc_ref, *, tf):
    xb = xb_ref[...]
    dff = wo_ref.shape[0]
    nf = dff // tf

    def gate_up(f):
        cols = slice(f * tf, (f + 1) * tf)
        return (jnp.dot(xb, win_ref[:, cols], preferred_element_type=F32),
                jnp.dot(xb, win_ref[:, dff + f * tf:dff + (f + 1) * tf], preferred_element_type=F32))

    nxt = gate_up(0)
    for f in range(nf):
        gate, up = nxt
        if f + 1 < nf:
            nxt = gate_up(f + 1)
        act = (gate * jax.nn.sigmoid(gate) * up).astype(BF16)
        part = jnp.dot(act, wo_ref[f * tf:(f + 1) * tf, :], preferred_element_type=F32)
        if f == 0:
            acc_ref[...] = part
        else:
            acc_ref[...] += part
    y = _layernorm(ALPHA * x_ref[...] + 0.5 * acc_ref[...], g_ref[...], b_ref[...])
    y_ref[...] = y
    yb_ref[...] = y.astype(BF16)


def _ffn_ln(x, xb, w_in, w_out, g, b, *, tm=1024, tf=256):
    t, d = x.shape
    dff = w_out.shape[0]
    assert dff % tf == 0 and t % tm == 0
    row = pl.BlockSpec((tm, d), lambda i: (i, 0))
    return pl.pallas_call(
        functools.partial(_ffn_ln_kernel, tf=tf),
        grid=(t // tm,),
        in_specs=[row, row, _const_spec((d, 2 * dff)), _const_spec((dff, d)), _const_spec((1, d)), _const_spec((1, d))],
        out_specs=[row, row],
        out_shape=[jax.ShapeDtypeStruct((t, d), F32), jax.ShapeDtypeStruct((t, d), BF16)],
        scratch_shapes=[pltpu.VMEM((tm, d), F32)],
        compiler_params=_cparams(1),
        name="ffn_ln",
    )(x, xb, w_in.astype(BF16), w_out.astype(BF16), g.reshape(1, d), b.reshape(1, d))


def _mix_in_kernel(xb_ref, w_ref, rw_ref, da_ref, gd_ref, *, tn):
    xb = xb_ref[...]
    off = 0
    for out_ref in (rw_ref, da_ref, gd_ref):
        width = out_ref.shape[-1]
        for c0 in range(0, width, tn):
            c1 = min(c0 + tn, width)
            res = jnp.dot(xb, w_ref[:, off + c0:off + c1], preferred_element_type=F32)
            out_ref[:, c0:c1] = res.astype(out_ref.dtype)
        off += width


def _mix_in(xb, w_cat, *, tm=512, tn=256):
    t, d = xb.shape
    n = w_cat.shape[1]
    assert n == RW_COLS + DA_COLS + GD_IN and t % tm == 0
    return pl.pallas_call(
        functools.partial(_mix_in_kernel, tn=tn),
        grid=(t // tm,),
        in_specs=[pl.BlockSpec((tm, d), lambda i: (i, 0)), _const_spec((d, n))],
        out_specs=[pl.BlockSpec((tm, RW_COLS), lambda i: (i, 0)),
                   pl.BlockSpec((tm, DA_COLS), lambda i: (i, 0)),
                   pl.BlockSpec((tm, GD_IN), lambda i: (i, 0))],
        out_shape=[jax.ShapeDtypeStruct((t, RW_COLS), F32),
                   jax.ShapeDtypeStruct((t, DA_COLS), BF16),
                   jax.ShapeDtypeStruct((t, GD_IN), F32)],
        compiler_params=_cparams(1),
        name="mix_in",
    )(xb, w_cat)


def _rwkv_kernel(h_ref, mu_ref, w0_ref, a0_ref, kk_ref, ka_ref, rk_ref, lng_ref, lnb_ref,
                 lora_ref, gup_ref, bd_ref, tril_ref, y_ref, prev_ref, s_ref, *, nb, nc):
    rows_b = nc * CHUNK

    @pl.when(pl.program_id(1) == 0)
    def _():
        prev_ref[:, 0:SUBLANES, :] = jnp.zeros((nb, SUBLANES, RW_COLS), F32)
        s_ref[...] = jnp.zeros_like(s_ref)

    n_pairs = RW_HEADS // 2
    parts = []
    for b in range(nb):
        h = h_ref[b]
        prev_ref[b, SUBLANES:SUBLANES + rows_b, :] = h
        hprev = prev_ref[b, SUBLANES - 1:SUBLANES - 1 + rows_b, :]
        prev_ref[b, 0:SUBLANES, :] = h[rows_b - SUBLANES:rows_b]
        parts.append(h + mu_ref[...] * (hprev - h))
    hs = jnp.concatenate(parts, axis=0) if nb > 1 else parts[0]
    n_rows = nb * rows_b

    w = RW_WIDTH
    r = hs[:, 0:w]
    k = hs[:, w:2 * w]
    v = hs[:, 2 * w:3 * w]
    wa = hs[:, 3 * w:3 * w + LANES]
    gd = hs[:, 3 * w + LANES:3 * w + 2 * LANES]
    lo_n = _iota((n_rows, LANES), 1) < RW_HEAD_DIM
    lora = _mm(jnp.where(lo_n, jnp.tanh(wa), wa), lora_ref[...])
    logw = -math.exp(-0.5) * jax.nn.sigmoid(w0_ref[...] + lora[:, 0:w])
    a = jax.nn.sigmoid(a0_ref[...] + lora[:, w:2 * w])
    g = _mm(jax.nn.sigmoid(gd), gup_ref[...])

    bd = bd_ref[...]
    kkr = k * kk_ref[...]
    kk = kkr * lax.rsqrt(_split_lhs_dot(kkr * kkr, bd, 2) + 1e-6)
    k2 = k * (1.0 + (a - 1.0) * ka_ref[...])
    bvec = kk * a
    bonus = _split_lhs_dot(r * k2 * rk_ref[...], bd, 2) * v

    e_in = _split_rhs_dot(tril_ref[...], logw, 3)
    rho = _rows_of_chunk(e_in, CHUNK // 2 - 1)
    r_t = r * jnp.exp(e_in - rho)
    kk_t = kk * jnp.exp(e_in - logw - rho)
    e_neg = jnp.exp(rho - e_in)
    k_h = k2 * e_neg
    b_h = bvec * e_neg

    lane = _iota((CHUNK, LANES), 1)
    lo = lane < RW_HEAD_DIM
    tt = _iota((CHUNK, LANES), 0)
    ss = lane & (CHUNK - 1)
    strict = tt > ss
    incl = tt >= ss

    units = [(b, ci, p) for b in range(nb) for ci in range(nc) for p in range(n_pairs)]

    def tile(x, u):
        b, ci, p = u
        r0 = (b * nc + ci) * CHUNK
        return x[r0:r0 + CHUNK, p * LANES:(p + 1) * LANES]

    kks = [_stack_heads(tile(kk_t, u), lo) for u in units]
    rs = [_stack_heads(tile(r_t, u), lo) for u in units]
    vs = [_stack_heads(tile(v, u), lo) for u in units]
    khs = [_stack_heads(tile(k_h, u), lo) for u in units]
    bhs = [_stack_heads(tile(b_h, u), lo) for u in units]
    z = [_mm_nt(jnp.concatenate([tile(kk_t, u), tile(r_t, u)], axis=0), jnp.concatenate([kh, bh], axis=0))
         for u, kh, bh in zip(units, khs, bhs)]
    ak = [_stack_heads(jnp.where(strict, zz[0:CHUNK, 0:PAIR], 0.0), lo) for zz in z]
    ab = [_stack_heads(jnp.where(strict, zz[0:CHUNK, PAIR:2 * PAIR], 0.0), lo) for zz in z]
    bk = [_stack_heads(jnp.where(incl, zz[CHUNK:PAIR, 0:PAIR], 0.0), lo) for zz in z]
    bb = [_stack_heads(jnp.where(incl, zz[CHUNK:PAIR, PAIR:2 * PAIR], 0.0), lo) for zz in z]
    inv = _inv_unit_lower_many(ab)
    akv = [_mm(x, y) for x, y in zip(ak, vs)]
    bkv = [_mm(x, y) for x, y in zip(bk, vs)]
    pq = [_mm(i, jnp.concatenate([x, y], axis=1)) for i, x, y in zip(inv, kks, akv)]
    bpq = [_mm(x, y) for x, y in zip(bb, pq)]
    rp = [x - y[:, 0:PAIR] for x, y in zip(rs, bpq)]
    y0 = [x - y[:, PAIR:2 * PAIR] for x, y in zip(bkv, bpq)]
    ptb = [_mm_tn(x[:, 0:PAIR], y) for x, y in zip(pq, bhs)]
    hm = [_mm_tn(jnp.concatenate([x, y[:, PAIR:2 * PAIR]], axis=0), jnp.concatenate([kh, -bh], axis=0))
          for x, y, kh, bh in zip(vs, pq, khs, bhs)]

    s_scale = jnp.exp(rho)
    d_out = jnp.exp(_rows_of_chunk(e_in, CHUNK - 1) - rho)
    yo = {}
    chains = [(b, p) for b in range(nb) for p in range(n_pairs)]
    state = {bp: s_ref[bp[0] * n_pairs + bp[1]] for bp in chains}
    for ci in range(nc):
        for b, p in chains:
            i = units.index((b, ci, p))
            r0 = (b * nc + ci) * CHUNK
            sl = slice(p * LANES, (p + 1) * LANES)
            sp = state[(b, p)] * s_scale[r0:r0 + 1, sl]
            yo[(b, ci, p)] = _mm_nt(rp[i], sp) + y0[i]
            state[(b, p)] = (sp - _mm(sp, ptb[i]) + hm[i]) * d_out[r0:r0 + 1, sl]
    for b, p in chains:
        s_ref[b * n_pairs + p] = state[(b, p)]
    y = jnp.concatenate(
        [jnp.concatenate([yo[(b, ci, p)][0:CHUNK] + yo[(b, ci, p)][CHUNK:PAIR] for p in range(n_pairs)], axis=1)
         for b in range(nb) for ci in range(nc)], axis=0)

    inv_n = 1.0 / RW_HEAD_DIM
    ym = _split_lhs_dot(y, bd, 2) * inv_n
    yc = y - ym
    yv = _split_lhs_dot(yc * yc, bd, 2) * inv_n
    yn = yc * lax.rsqrt(yv + RW_GN_EPS) * lng_ref[...] + lnb_ref[...]
    out = ((yn + bonus) * g).astype(y_ref.dtype)
    for b in range(nb):
        y_ref[b] = out[b * rows_b:(b + 1) * rows_b]


def _rwkv7(h_rw, mu, w0, w_up, a0, a_up, g_up, k_k, k_a, r_k, ln_g, ln_b, *, nb=SCAN_BATCHES, nc=SCAN_CHUNKS):
    b, l, _ = h_rw.shape
    w = RW_WIDTH
    nb = min(nb, b)
    rows_b = nc * CHUNK
    assert l % rows_b == 0 and b % nb == 0
    lora = jnp.zeros((LANES, 2 * w), F32)
    lora = lora.at[0:RW_DECAY_LORA, 0:w].set(w_up).at[RW_DECAY_LORA:LANES, w:2 * w].set(a_up)
    hid = jnp.arange(w) // RW_HEAD_DIM
    bd = (hid[:, None] == hid[None, :]).astype(BF16)
    vec = lambda t: t.reshape(1, -1).astype(F32)
    params = [vec(mu), vec(w0), vec(a0), vec(k_k), vec(k_a), vec(r_k), vec(ln_g), vec(ln_b),
              lora.astype(BF16), g_up.astype(BF16), bd, _chunk_tril(nb * nc)]
    return pl.pallas_call(
        functools.partial(_rwkv_kernel, nb=nb, nc=nc),
        grid=(b // nb, l // rows_b),
        in_specs=[pl.BlockSpec((nb, rows_b, RW_COLS), lambda i, j: (i, j, 0))]
        + [_const_spec(p.shape) for p in params],
        out_specs=pl.BlockSpec((nb, rows_b, w), lambda i, j: (i, j, 0)),
        out_shape=jax.ShapeDtypeStruct((b, l, w), BF16),
        scratch_shapes=[pltpu.VMEM((nb, SUBLANES + rows_b, RW_COLS), F32),
                        pltpu.VMEM((nb * RW_HEADS // 2, PAIR, PAIR), F32)],
        compiler_params=_cparams(2),
        name="rwkv7",
    )(h_rw, *params)


def _attn_kernel(slope_ref, lam_ref, ng_ref, q_ref, k_ref, v_ref, o_ref, ka_ref, vt_ref, s0_ref, s1_ref, m_ref,
                 acc_ref, *, tq, tk, lam_init):
    qi = pl.program_id(2)
    d = DA_HEAD_DIM
    dv = 2 * d
    slope2 = slope_ref[0, 0:1, 0:1] * LOG2E

    @pl.when(qi == 0)
    def _():
        lane = _iota((tk, LANES), 1)
        bias = slope2 * _iota((tk, LANES), 0).astype(F32)
        hi = bias.astype(BF16).astype(F32)
        mid = (bias - hi).astype(BF16).astype(F32)
        low = bias - hi - mid
        pieces = jnp.where(lane == 0, hi, jnp.where(lane == 1, mid, jnp.where(lane == 2, low, 0.0))).astype(BF16)
        ones_row = (_iota((DA_ONES_ROWS, tk), 0) == 0).astype(BF16)
        for jb in range(vt_ref.shape[0]):
            rows = slice(jb * tk, (jb + 1) * tk)
            ka_ref[rows, 0:LANES] = k_ref[0, rows, :]
            ka_ref[rows, LANES:2 * LANES] = pieces
            vt_ref[jb, 0:dv, :] = v_ref[0, rows, :].astype(F32).T.astype(BF16)
            vt_ref[jb, dv:dv + DA_ONES_ROWS, :] = ones_row

    q = q_ref[0].astype(F32) * (d ** -0.5 * LOG2E)
    lane = _iota((tq, LANES), 1)
    lo = lane < d
    bias_ones = (lane < 3).astype(BF16)
    q2 = jnp.concatenate([
        jnp.concatenate([jnp.where(lo, q, 0.0).astype(BF16), bias_ones], axis=1),
        jnp.concatenate([jnp.where(lo, 0.0, q).astype(BF16), bias_ones], axis=1)], axis=0)
    m_ref[...] = jnp.full_like(m_ref, -jnp.inf)
    acc_ref[...] = jnp.zeros_like(acc_ref)

    def scores(j, dst_ref):
        kb = ka_ref[pl.ds(j * tk, tk), :]
        dst_ref[...] = lax.dot_general(kb, q2, (((1,), (1,)), ((), ())), preferred_element_type=F32)

    def consume(j, src_ref, masked):
        shift = slope2 * (j * tk - qi * tq).astype(F32)
        s = src_ref[...]
        if masked:
            s = jnp.where(_iota((tk, 2 * tq), 0) <= (_iota((tk, 2 * tq), 1) & (tq - 1)), s, -jnp.inf)
        m_old = m_ref[...]
        m_new = jnp.maximum(m_old, jnp.max(s, axis=0, keepdims=True) + shift)
        p = jnp.exp2(s - (m_new - shift)).astype(BF16)
        acc_ref[...] = (jnp.exp2(m_old - m_new) * acc_ref[...]
                        + jnp.dot(vt_ref[j], p, preferred_element_type=F32))
        m_ref[...] = m_new

    nfull = (qi * tq) // tk
    scores(0, s0_ref)

    def pair(jj, carry):
        j = 2 * jj
        scores(j + 1, s1_ref)
        consume(j, s0_ref, False)
        scores(j + 2, s0_ref)
        consume(j + 1, s1_ref, False)
        return carry

    lax.fori_loop(0, nfull // 2, pair, 0)

    @pl.when(nfull % 2 == 1)
    def _():
        scores(nfull, s1_ref)
        consume(nfull - 1, s0_ref, False)
        consume(nfull, s1_ref, True)

    @pl.when(nfull % 2 == 0)
    def _():
        consume(nfull, s0_ref, True)

    lp = lam_ref[...]
    lam = (jnp.exp(jnp.sum(lp[0:1] * lp[1:2], axis=-1, keepdims=True))
           - jnp.exp(jnp.sum(lp[2:3] * lp[3:4], axis=-1, keepdims=True)) + lam_init)
    acc = acc_ref[...]
    ot = acc[0:dv] / acc[dv:dv + 1]
    ot = ot[:, 0:tq] - lam * ot[:, tq:2 * tq]
    ot = ot * lax.rsqrt(jnp.mean(ot * ot, axis=0, keepdims=True) + 1e-5)
    o_ref[0] = (ot.T * ng_ref[...] * (1.0 - lam_init)).astype(o_ref.dtype)


def _diff_attention(h_da, lam_q1, lam_k1, lam_q2, lam_k2, norm_g, lam_init, *, tq=512):
    b, l, _ = h_da.shape
    tq = min(tq, l)
    tk = tq
    assert l % tq == 0 and tq & (tq - 1) == 0
    hh = DA_HEADS
    slopes = jnp.exp2(-8.0 * jnp.arange(1, hh + 1, dtype=F32) / hh)
    slopes = jnp.broadcast_to(slopes[:, None, None], (hh, SUBLANES, LANES))
    lam_p = jnp.stack([lam_q1, lam_k1, lam_q2, lam_k2]).astype(F32)
    ng = norm_g.reshape(1, 2 * DA_HEAD_DIM).astype(F32)
    return pl.pallas_call(
        functools.partial(_attn_kernel, tq=tq, tk=tk, lam_init=lam_init),
        grid=(b, hh, l // tq),
        in_specs=[pl.BlockSpec((1, SUBLANES, LANES), lambda i, h, j: (h, 0, 0)),
                  pl.BlockSpec(lam_p.shape, lambda i, h, j: (0, 0)),
                  pl.BlockSpec(ng.shape, lambda i, h, j: (0, 0)),
                  pl.BlockSpec((1, tq, LANES), lambda i, h, j: (i, j, h)),
                  pl.BlockSpec((1, l, LANES), lambda i, h, j: (i, 0, hh + h)),
                  pl.BlockSpec((1, l, LANES), lambda i, h, j: (i, 0, 2 * hh + h))],
        out_specs=pl.BlockSpec((1, tq, LANES), lambda i, h, j: (i, j, h)),
        out_shape=jax.ShapeDtypeStruct((b, l, DA_WIDTH), BF16),
        scratch_shapes=[pltpu.VMEM((l, 2 * LANES), BF16),
                        pltpu.VMEM((l // tk, LANES + DA_ONES_ROWS, tk), BF16),
                        pltpu.VMEM((tk, 2 * tq), F32), pltpu.VMEM((tk, 2 * tq), F32),
                        pltpu.VMEM((1, 2 * tq), F32), pltpu.VMEM((LANES + DA_ONES_ROWS, 2 * tq), F32)],
        compiler_params=_cparams(3),
        name="diffattn",
    )(slopes, lam_p, ng, h_da, h_da, h_da)


def _softplus(x):
    return jnp.maximum(x, 0.0) + jnp.log(1.0 + jnp.exp(-jnp.abs(x)))


def _gdn_kernel(h_ref, cw_ref, alog_ref, dtb_ref, ng_ref, tril_ref, y_ref, prev_ref, s_ref, *, nb, nc):
    rows_b = nc * CHUNK
    wq = 3 * GD_WIDTH

    @pl.when(pl.program_id(1) == 0)
    def _():
        prev_ref[:, 0:SUBLANES, :] = jnp.zeros((nb, SUBLANES, wq), F32)
        s_ref[...] = jnp.zeros_like(s_ref)

    dh = GD_HEAD_DIM
    cw = cw_ref[...]
    parts = []
    for b in range(nb):
        x = h_ref[b, :, 0:wq]
        prev_ref[b, SUBLANES:SUBLANES + rows_b, :] = x
        conv = x * cw[GD_CONV - 1:GD_CONV]
        for j in range(GD_CONV - 1):
            r0 = SUBLANES - (GD_CONV - 1 - j)
            conv = conv + prev_ref[b, r0:r0 + rows_b, :] * cw[j:j + 1]
        prev_ref[b, 0:SUBLANES, :] = x[rows_b - SUBLANES:rows_b]
        parts.append(conv)
    conv = jnp.concatenate(parts, axis=0) if nb > 1 else parts[0]
    qkv = conv * jax.nn.sigmoid(conv)

    ab = jnp.concatenate([h_ref[b, :, wq:wq + GD_AB_PAD] for b in range(nb)], axis=0)
    gfull = -jnp.exp(alog_ref[...]) * _softplus(ab + dtb_ref[...])
    gcum = _split_rhs_dot(tril_ref[...], gfull, 3)
    glast = _rows_of_chunk(gcum, CHUNK - 1)
    beta_full = jax.nn.sigmoid(ab)

    def l2n(t):
        return t * lax.rsqrt(jnp.sum(t * t, axis=-1, keepdims=True) + 1e-6)

    qn = [l2n(qkv[:, h * dh:(h + 1) * dh]) * dh ** -0.5 for h in range(GD_HEADS)]
    kn = [l2n(qkv[:, GD_WIDTH + h * dh:GD_WIDTH + (h + 1) * dh]) for h in range(GD_HEADS)]
    vh = [qkv[:, 2 * GD_WIDTH + h * dh:2 * GD_WIDTH + (h + 1) * dh] for h in range(GD_HEADS)]

    r0 = _iota((PAIR, PAIR), 0)
    c0 = _iota((PAIR, PAIR), 1)
    same = (r0 >= CHUNK) == (c0 >= CHUNK)
    incl = same & (r0 >= c0)
    strict = same & (r0 > c0)

    n_pairs = GD_HEADS // 2
    units = [(b, ci, p) for b in range(nb) for ci in range(nc) for p in range(n_pairs)]

    def stack(per_head, u, lane_of=None):
        b, ci, p = u
        rr = slice((b * nc + ci) * CHUNK, (b * nc + ci + 1) * CHUNK)
        if lane_of is None:
            return jnp.concatenate([per_head[2 * p][rr], per_head[2 * p + 1][rr]], axis=0)
        return jnp.concatenate([per_head[rr, lane_of + 2 * p:lane_of + 2 * p + 1],
                                per_head[rr, lane_of + 2 * p + 1:lane_of + 2 * p + 2]], axis=0)

    q_s = [stack(qn, u) for u in units]
    k_s = [stack(kn, u) for u in units]
    v_s = [stack(vh, u) for u in units]
    g_s = [stack(gcum, u, 0) for u in units]
    gl_s = [stack(glast, u, 0) for u in units]
    beta_s = [stack(beta_full, u, GD_HEADS) for u in units]

    decay = []
    for gs in g_s:
        g_b = jnp.broadcast_to(gs, (PAIR, PAIR))
        decay.append(jnp.exp(jnp.where(incl, g_b - g_b.T, -jnp.inf)))
    kb = [x * y for x, y in zip(k_s, beta_s)]
    kk = [_mm_nt(jnp.concatenate([x, y], axis=0), z) for x, y, z in zip(kb, q_s, k_s)]
    low = [jnp.where(strict, x[0:PAIR] * dc, 0.0) for x, dc in zip(kk, decay)]
    intra = [x[PAIR:2 * PAIR] * dc for x, dc in zip(kk, decay)]
    inv = _inv_unit_lower_many(low)
    eg = [jnp.exp(gs) for gs in g_s]
    uw = [_mm(i, jnp.concatenate([v * bt, x * e], axis=1)) for i, v, bt, x, e in zip(inv, v_s, beta_s, kb, eg)]
    qg = [x * e for x, e in zip(q_s, eg)]
    kd = [x * jnp.exp(gl - gs) for x, gl, gs in zip(k_s, gl_s, g_s)]

    v_new = {}
    o_state = {}
    chains = [(b, h) for b in range(nb) for h in range(GD_HEADS)]
    state = {bh: s_ref[bh[0] * GD_HEADS + bh[1]] for bh in chains}
    for ci in range(nc):
        ws = {}
        for b, h in chains:
            p, i_h = divmod(h, 2)
            rows = slice(i_h * CHUNK, (i_h + 1) * CHUNK)
            i = units.index((b, ci, p))
            ws[(b, h)] = _mm(jnp.concatenate([uw[i][rows, dh:2 * dh], qg[i][rows]], axis=0), state[(b, h)])
        for b, h in chains:
            p, i_h = divmod(h, 2)
            rows = slice(i_h * CHUNK, (i_h + 1) * CHUNK)
            i = units.index((b, ci, p))
            vn = uw[i][rows, 0:dh] - ws[(b, h)][0:CHUNK]
            v_new[(b, ci, h)] = vn
            o_state[(b, ci, h)] = ws[(b, h)][CHUNK:PAIR]
            state[(b, h)] = state[(b, h)] * jnp.exp(gl_s[i][rows][0:1]) + _mm_tn(kd[i][rows], vn)
    for b, h in chains:
        s_ref[b * GD_HEADS + h] = state[(b, h)]

    for i, (b, ci, p) in enumerate(units):
        vn = jnp.concatenate([v_new[(b, ci, 2 * p)], v_new[(b, ci, 2 * p + 1)]], axis=0)
        o = jnp.concatenate([o_state[(b, ci, 2 * p)], o_state[(b, ci, 2 * p + 1)]], axis=0) + _mm(intra[i], vn)
        for i_h in range(2):
            h = 2 * p + i_h
            oh = o[i_h * CHUNK:(i_h + 1) * CHUNK]
            oh = oh * lax.rsqrt(jnp.mean(oh * oh, axis=-1, keepdims=True) + 1e-6) * ng_ref[...]
            zc = wq + GD_AB_PAD + h * dh
            z = h_ref[b, ci * CHUNK:(ci + 1) * CHUNK, zc:zc + dh]
            y_ref[b, ci * CHUNK:(ci + 1) * CHUNK, h * dh:(h + 1) * dh] = (
                oh * (z * jax.nn.sigmoid(z))).astype(y_ref.dtype)


def _gated_deltanet(h_gd, conv_w, a_log, dt_bias, norm_g, *, nb=SCAN_BATCHES, nc=SCAN_CHUNKS):
    b, l, _ = h_gd.shape
    nb = min(nb, b)
    rows_b = nc * CHUNK
    assert l % rows_b == 0 and b % nb == 0
    pad = lambda t: jnp.zeros((1, GD_AB_PAD), F32).at[0, 0:GD_HEADS].set(t.astype(F32))
    params = [conv_w.astype(F32), pad(a_log), pad(dt_bias), norm_g.reshape(1, GD_HEAD_DIM).astype(F32),
              _chunk_tril(nb * nc)]
    return pl.pallas_call(
        functools.partial(_gdn_kernel, nb=nb, nc=nc),
        grid=(b // nb, l // rows_b),
        in_specs=[pl.BlockSpec((nb, rows_b, GD_IN), lambda i, j: (i, j, 0))]
        + [_const_spec(p.shape) for p in params],
        out_specs=pl.BlockSpec((nb, rows_b, GD_WIDTH), lambda i, j: (i, j, 0)),
        out_shape=jax.ShapeDtypeStruct((b, l, GD_WIDTH), BF16),
        scratch_shapes=[pltpu.VMEM((nb, SUBLANES + rows_b, 3 * GD_WIDTH), F32),
                        pltpu.VMEM((nb * GD_HEADS, GD_HEAD_DIM, GD_HEAD_DIM), F32)],
        compiler_params=_cparams(2),
        name="gdn",
    )(h_gd, *params)


def _merge_ln_kernel(x_ref, xb_ref, ya_ref, yb_ref, yc_ref, wgate_ref, wbr_ref, wout_ref, g_ref, b_ref,
                     y_ref, y16_ref):
    xb = xb_ref[...]
    merged = None
    for n, br_ref in enumerate((ya_ref, yb_ref, yc_ref)):
        gate = jax.nn.sigmoid(jnp.dot(xb, wgate_ref[n], preferred_element_type=F32))
        term = gate * jnp.dot(br_ref[...], wbr_ref[n], preferred_element_type=F32)
        merged = term if merged is None else merged + term
    mix = jnp.dot(merged.astype(BF16), wout_ref[...], preferred_element_type=F32)
    y = _layernorm(ALPHA * x_ref[...] + mix, g_ref[...], b_ref[...])
    y_ref[...] = y
    y16_ref[...] = y.astype(BF16)


def _merge_ln(x, xb, ya, yb, yc, w_gate, w_branch, w_out, g, b, *, tm=512):
    t, d = x.shape
    bw = ya.shape[-1]
    row = pl.BlockSpec((tm, d), lambda i: (i, 0))
    brow = pl.BlockSpec((tm, bw), lambda i: (i, 0))
    return pl.pallas_call(
        _merge_ln_kernel,
        grid=(t // tm,),
        in_specs=[row, row, brow, brow, brow, _const_spec((N_BRANCH, d, d)),
                  _const_spec((N_BRANCH, bw, d)), _const_spec((d, d)), _const_spec((1, d)), _const_spec((1, d))],
        out_specs=[row, row],
        out_shape=[jax.ShapeDtypeStruct((t, d), F32), jax.ShapeDtypeStruct((t, d), BF16)],
        compiler_params=_cparams(1),
        name="merge_ln",
    )(x, xb, ya, yb, yc, w_gate, w_branch, w_out, g.reshape(1, d), b.reshape(1, d))


def _split_mix_w_in(w_in):
    d = w_in.shape[0]
    o1 = RW_COLS
    o2 = o1 + DA_COLS
    o3 = o2 + GD_COLS
    gq = o2 + 3 * GD_WIDTH
    ab = jnp.zeros((d, GD_AB_PAD), w_in.dtype).at[:, 0:2 * GD_HEADS].set(w_in[:, gq:gq + 2 * GD_HEADS])
    w_cat = jnp.concatenate([w_in[:, 0:o2], w_in[:, o2:gq], ab, w_in[:, gq + 2 * GD_HEADS:o3]], axis=1)
    w_gate = w_in[:, o3:].reshape(d, N_BRANCH, d).transpose(1, 0, 2)
    return w_cat.astype(BF16), w_gate.astype(BF16)


def kernel(x, ffn1_w_in, ffn1_w_out, ln1_g, ln1_b, mix_w_in, rw_shift_mu, rw_w0, rw_w_up, rw_a0, rw_a_up, rw_g_up, rw_k_k, rw_k_a, rw_r_k, rw_ln_g, rw_ln_b, da_lam_q1, da_lam_k1, da_lam_q2, da_lam_k2, da_norm_g, gd_conv_w, gd_a_log, gd_dt_bias, gd_norm_g, mix_w_branch, mix_w_out, ln2_g, ln2_b, ffn2_w_in, ffn2_w_out, ln3_g, ln3_b):
    b, l, d = x.shape
    t = b * l
    xf = x.reshape(t, d)
    xb = xf.astype(BF16)
    for i in range(DEPTH):
        xf, xb = _ffn_ln(xf, xb, ffn1_w_in[i], ffn1_w_out[i], ln1_g[i], ln1_b[i])
        w_cat, w_gate = _split_mix_w_in(mix_w_in[i])
        h_rw, h_da, h_gd = _mix_in(xb, w_cat)
        ya = _rwkv7(h_rw.reshape(b, l, RW_COLS), rw_shift_mu[i], rw_w0[i], rw_w_up[i], rw_a0[i], rw_a_up[i],
                    rw_g_up[i], rw_k_k[i], rw_k_a[i], rw_r_k[i].reshape(-1), rw_ln_g[i], rw_ln_b[i])
        lam_init = 0.8 - 0.6 * math.exp(-0.3 * i)
        yb = _diff_attention(h_da.reshape(b, l, DA_COLS), da_lam_q1[i], da_lam_k1[i], da_lam_q2[i],
                             da_lam_k2[i], da_norm_g[i], lam_init)
        yc = _gated_deltanet(h_gd.reshape(b, l, GD_IN), gd_conv_w[i], gd_a_log[i], gd_dt_bias[i], gd_norm_g[i])
        xf, xb = _merge_ln(xf, xb, ya.reshape(t, RW_WIDTH), yb.reshape(t, DA_WIDTH), yc.reshape(t, GD_WIDTH),
                           w_gate, mix_w_branch[i].astype(BF16), mix_w_out[i].astype(BF16), ln2_g[i], ln2_b[i])
        xf, xb = _ffn_ln(xf, xb, ffn2_w_in[i], ffn2_w_out[i], ln3_g[i], ln3_b[i])
    return xf.reshape(b, l, d)
```

```python
import functools
import math

import jax
import jax.numpy as jnp
from jax import lax
from jax.experimental import pallas as pl
from jax.experimental.pallas import tpu as pltpu

F32 = jnp.float32
BF16 = jnp.bfloat16

D_MODEL = 1024
DEPTH = 2
D_FF = 2816
RW_HEADS = 8
RW_HEAD_DIM = 64
RW_WIDTH = 512
RW_DECAY_LORA = 64
RW_ICLR_LORA = 64
RW_GATE_LORA = 128
RW_COLS = 3 * RW_WIDTH + RW_DECAY_LORA + RW_ICLR_LORA + RW_GATE_LORA
RW_GN_EPS = 64e-5
DA_HEADS = 4
DA_HEAD_DIM = 64
DA_WIDTH = 512
DA_COLS = 3 * DA_WIDTH
GD_HEADS = 4
GD_HEAD_DIM = 128
GD_WIDTH = 512
GD_CONV = 4
GD_COLS = 3 * GD_WIDTH + 2 * GD_HEADS + GD_WIDTH
N_BRANCH = 3
ALPHA = (2.0 * DEPTH) ** 0.25

LANES = 128
SUBLANES = 8
CHUNK = 64
PAIR = 2 * CHUNK
DA_ONES_ROWS = 16
LOG2E = math.log2(math.e)
GD_AB_PAD = LANES
GD_IN = 3 * GD_WIDTH + GD_AB_PAD + GD_WIDTH
VMEM_LIMIT = 56 * 1024 * 1024
SCAN_BATCHES = 2
SCAN_CHUNKS = 2


def _cparams(n_grid):
    return pltpu.CompilerParams(dimension_semantics=("arbitrary",) * n_grid,
                                vmem_limit_bytes=VMEM_LIMIT)


def _const_spec(shape):
    nd = len(shape)
    return pl.BlockSpec(shape, lambda *_: (0,) * nd, pipeline_mode=pl.Buffered(1))


def _mm(a, b):
    return jnp.dot(a.astype(BF16), b.astype(BF16), preferred_element_type=F32)


def _mm_nt(a, b):
    return lax.dot_general(a.astype(BF16), b.astype(BF16), (((1,), (1,)), ((), ())),
                           preferred_element_type=F32)


def _mm_tn(a, b):
    return lax.dot_general(a.astype(BF16), b.astype(BF16), (((0,), (0,)), ((), ())),
                           preferred_element_type=F32)


def _split_lhs_dot(x, exact_rhs, terms):
    acc = None
    rem = x
    for _ in range(terms):
        hi = rem.astype(BF16)
        part = jnp.dot(hi, exact_rhs, preferred_element_type=F32)
        acc = part if acc is None else acc + part
        rem = rem - hi.astype(F32)
    return acc


def _split_rhs_dot(exact_lhs, x, terms):
    acc = None
    rem = x
    for _ in range(terms):
        hi = rem.astype(BF16)
        part = jnp.dot(exact_lhs, hi, preferred_element_type=F32)
        acc = part if acc is None else acc + part
        rem = rem - hi.astype(F32)
    return acc


def _layernorm(z, g, b, eps=1e-5):
    mu = jnp.mean(z, axis=-1, keepdims=True)
    zc = z - mu
    var = jnp.mean(zc * zc, axis=-1, keepdims=True)
    return zc * lax.rsqrt(var + eps) * g + b


def _iota(shape, dim):
    return lax.broadcasted_iota(jnp.int32, shape, dim)


def _inv_unit_lower_many(lows):
    eye = (_iota((PAIR, PAIR), 0) == _iota((PAIR, PAIR), 1)).astype(F32)
    pw = [-low for low in lows]
    inv = [eye + n for n in pw]
    for _ in range(int(math.log2(CHUNK)) - 1):
        pw = [_mm(p, p) for p in pw]
        inv = [i + _mm(i, p) for i, p in zip(inv, pw)]
    return inv


def _stack_heads(x, lo_mask):
    return jnp.concatenate([jnp.where(lo_mask, x, 0.0), jnp.where(lo_mask, 0.0, x)], axis=0)


def _chunk_tril(n_chunks):
    r = jnp.arange(n_chunks * CHUNK)
    return ((r[:, None] >= r[None, :]) & (r[:, None] // CHUNK == r[None, :] // CHUNK)).astype(BF16)


def _rows_of_chunk(x, row):
    n = x.shape[0] // CHUNK
    parts = [jnp.broadcast_to(x[i * CHUNK + row:i * CHUNK + row + 1], (CHUNK,) + x.shape[1:]) for i in range(n)]
    return jnp.concatenate(parts, axis=0) if n > 1 else parts[0]


def _ffn_ln_kernel(x_ref, xb_ref, win_ref, wo_ref, g_ref, b_ref, y_ref, yb_ref, acc_ref, *, tf):
    xb = xb_ref[...]
    dff = wo_ref.shape[0]
    nf = dff // tf

    def gate_up(f):
        cols = slice(f * tf, (f + 1) * tf)
        return (jnp.dot(xb, win_ref[:, cols], preferred_element_type=F32),
                jnp.dot(xb, win_ref[:, dff + f * tf:dff + (f + 1) * tf], preferred_element_type=F32))

    nxt = gate_up(0)
    for f in range(nf):
        gate, up = nxt
        if f + 1 < nf:
            nxt = gate_up(f + 1)
        act = (gate * jax.nn.sigmoid(gate) * up).astype(BF16)
        part = jnp.dot(act, wo_ref[f * tf:(f + 1) * tf, :], preferred_element_type=F32)
        if f == 0:
            acc_ref[...] = part
        else:
            acc_ref[...] += part
    y = _layernorm(ALPHA * x_ref[...] + 0.5 * acc_ref[...], g_ref[...], b_ref[...])
    y_ref[...] = y
    yb_ref[...] = y.astype(BF16)


def _ffn_ln(x, xb, w_in, w_out, g, b, *, tm=1024, tf=256):
    t, d = x.shape
    dff = w_out.shape[0]
    assert dff % tf == 0 and t % tm == 0
    row = pl.BlockSpec((tm, d), lambda i: (i, 0))
    return pl.pallas_call(
        functools.partial(_ffn_ln_kernel, tf=tf),
        grid=(t // tm,),
        in_specs=[row, row, _const_spec((d, 2 * dff)), _const_spec((dff, d)), _const_spec((1, d)), _const_spec((1, d))],
        out_specs=[row, row],
        out_shape=[jax.ShapeDtypeStruct((t, d), F32), jax.ShapeDtypeStruct((t, d), BF16)],
        scratch_shapes=[pltpu.VMEM((tm, d), F32)],
        compiler_params=_cparams(1),
        name="ffn_ln",
    )(x, xb, w_in.astype(BF16), w_out.astype(BF16), g.reshape(1, d), b.reshape(1, d))


def _mix_in_kernel(xb_ref, w_ref, rw_ref, da_ref, gd_ref, *, tn):
    xb = xb_ref[...]
    off = 0
    for out_ref in (rw_ref, da_ref, gd_ref):
        width = out_ref.shape[-1]
        for c0 in range(0, width, tn):
            c1 = min(c0 + tn, width)
            res = jnp.dot(xb, w_ref[:, off + c0:off + c1], preferred_element_type=F32)
            out_ref[:, c0:c1] = res.astype(out_ref.dtype)
        off += width


def _mix_in(xb, w_cat, *, tm=512, tn=256):
    t, d = xb.shape
    n = w_cat.shape[1]
    assert n == RW_COLS + DA_COLS + GD_IN and t % tm == 0
    return pl.pallas_call(
        functools.partial(_mix_in_kernel, tn=tn),
        grid=(t // tm,),
        in_specs=[pl.BlockSpec((tm, d), lambda i: (i, 0)), _const_spec((d, n))],
        out_specs=[pl.BlockSpec((tm, RW_COLS), lambda i: (i, 0)),
                   pl.BlockSpec((tm, DA_COLS), lambda i: (i, 0)),
                   pl.BlockSpec((tm, GD_IN), lambda i: (i, 0))],
        out_shape=[jax.ShapeDtypeStruct((t, RW_COLS), F32),
                   jax.ShapeDtypeStruct((t, DA_COLS), BF16),
                   jax.ShapeDtypeStruct((t, GD_IN), F32)],
        compiler_params=_cparams(1),
        name="mix_in",
    )(xb, w_cat)


def _rwkv_kernel(h_ref, mu_ref, w0_ref, a0_ref, kk_ref, ka_ref, rk_ref, lng_ref, lnb_ref,
                 lora_ref, gup_ref, bd_ref, tril_ref, y_ref, prev_ref, s_ref, *, nb, nc):
    rows_b = nc * CHUNK

    @pl.when(pl.program_id(1) == 0)
    def _():
        prev_ref[:, 0:SUBLANES, :] = jnp.zeros((nb, SUBLANES, RW_COLS), F32)
        s_ref[...] = jnp.zeros_like(s_ref)

    n_pairs = RW_HEADS // 2
    parts = []
    for b in range(nb):
        h = h_ref[b]
        prev_ref[b, SUBLANES:SUBLANES + rows_b, :] = h
        hprev = prev_ref[b, SUBLANES - 1:SUBLANES - 1 + rows_b, :]
        prev_ref[b, 0:SUBLANES, :] = h[rows_b - SUBLANES:rows_b]
        parts.append(h + mu_ref[...] * (hprev - h))
    hs = jnp.concatenate(parts, axis=0) if nb > 1 else parts[0]
    n_rows = nb * rows_b

    w = RW_WIDTH
    r = hs[:, 0:w]
    k = hs[:, w:2 * w]
    v = hs[:, 2 * w:3 * w]
    wa = hs[:, 3 * w:3 * w + LANES]
    gd = hs[:, 3 * w + LANES:3 * w + 2 * LANES]
    lo_n = _iota((n_rows, LANES), 1) < RW_HEAD_DIM
    lora = _mm(jnp.where(lo_n, jnp.tanh(wa), wa), lora_ref[...])
    logw = -math.exp(-0.5) * jax.nn.sigmoid(w0_ref[...] + lora[:, 0:w])
    a = jax.nn.sigmoid(a0_ref[...] + lora[:, w:2 * w])
    g = _mm(jax.nn.sigmoid(gd), gup_ref[...])

    bd = bd_ref[...]
    kkr = k * kk_ref[...]
    kk = kkr * lax.rsqrt(_split_lhs_dot(kkr * kkr, bd, 1) + 1e-6)
    k2 = k * (1.0 + (a - 1.0) * ka_ref[...])
    bvec = kk * a
    bonus = _split_lhs_dot(r * k2 * rk_ref[...], bd, 2) * v

    e_in = _split_rhs_dot(tril_ref[...], logw, 2)
    rho = _rows_of_chunk(e_in, CHUNK // 2 - 1)
    r_t = r * jnp.exp(e_in - rho)
    kk_t = kk * jnp.exp(e_in - logw - rho)
    e_neg = jnp.exp(rho - e_in)
    k_h = k2 * e_neg
    b_h = bvec * e_neg

    lane = _iota((CHUNK, LANES), 1)
    lo = lane < RW_HEAD_DIM
    tt = _iota((CHUNK, LANES), 0)
    ss = lane & (CHUNK - 1)
    strict = tt > ss
    incl = tt >= ss

    units = [(b, ci, p) for b in range(nb) for ci in range(nc) for p in range(n_pairs)]

    def tile(x, u):
        b, ci, p = u
        r0 = (b * nc + ci) * CHUNK
        return x[r0:r0 + CHUNK, p * LANES:(p + 1) * LANES]

    kks = [_stack_heads(tile(kk_t, u), lo) for u in units]
    rs = [_stack_heads(tile(r_t, u), lo) for u in units]
    vs = [_stack_heads(tile(v, u), lo) for u in units]
    khs = [_stack_heads(tile(k_h, u), lo) for u in units]
    bhs = [_stack_heads(tile(b_h, u), lo) for u in units]
    z = [_mm_nt(jnp.concatenate([tile(kk_t, u), tile(r_t, u)], axis=0), jnp.concatenate([kh, bh], axis=0))
         for u, kh, bh in zip(units, khs, bhs)]
    ak = [_stack_heads(jnp.where(strict, zz[0:CHUNK, 0:PAIR], 0.0), lo) for zz in z]
    ab = [_stack_heads(jnp.where(strict, zz[0:CHUNK, PAIR:2 * PAIR], 0.0), lo) for zz in z]
    bk = [_stack_heads(jnp.where(incl, zz[CHUNK:PAIR, 0:PAIR], 0.0), lo) for zz in z]
    bb = [_stack_heads(jnp.where(incl, zz[CHUNK:PAIR, PAIR:2 * PAIR], 0.0), lo) for zz in z]
    inv = _inv_unit_lower_many(ab)
    akv = [_mm(x, y) for x, y in zip(ak, vs)]
    bkv = [_mm(x, y) for x, y in zip(bk, vs)]
    pq = [_mm(i, jnp.concatenate([x, y], axis=1)) for i, x, y in zip(inv, kks, akv)]
    bpq = [_mm(x, y) for x, y in zip(bb, pq)]
    rp = [x - y[:, 0:PAIR] for x, y in zip(rs, bpq)]
    y0 = [x - y[:, PAIR:2 * PAIR] for x, y in zip(bkv, bpq)]
    ptb = [_mm_tn(x[:, 0:PAIR], y) for x, y in zip(pq, bhs)]
    hm = [_mm_tn(jnp.concatenate([x, y[:, PAIR:2 * PAIR]], axis=0), jnp.concatenate([kh, -bh], axis=0))
          for x, y, kh, bh in zip(vs, pq, khs, bhs)]

    s_scale = jnp.exp(rho)
    d_out = jnp.exp(_rows_of_chunk(e_in, CHUNK - 1) - rho)
    yo = {}
    chains = [(b, p) for b in range(nb) for p in range(n_pairs)]
    state = {bp: s_ref[bp[0] * n_pairs + bp[1]] for bp in chains}
    for ci in range(nc):
        for b, p in chains:
            i = units.index((b, ci, p))
            r0 = (b * nc + ci) * CHUNK
            sl = slice(p * LANES, (p + 1) * LANES)
            sp = state[(b, p)] * s_scale[r0:r0 + 1, sl]
            yo[(b, ci, p)] = _mm_nt(rp[i], sp) + y0[i]
            state[(b, p)] = (sp - _mm(sp, ptb[i]) + hm[i]) * d_out[r0:r0 + 1, sl]
    for b, p in chains:
        s_ref[b * n_pairs + p] = state[(b, p)]
    y = jnp.concatenate(
        [jnp.concatenate([yo[(b, ci, p)][0:CHUNK] + yo[(b, ci, p)][CHUNK:PAIR] for p in range(n_pairs)], axis=1)
         for b in range(nb) for ci in range(nc)], axis=0)

    inv_n = 1.0 / RW_HEAD_DIM
    ym = _split_lhs_dot(y, bd, 1) * inv_n
    yc = y - ym
    yv = _split_lhs_dot(yc * yc, bd, 1) * inv_n
    yn = yc * lax.rsqrt(yv + RW_GN_EPS) * lng_ref[...] + lnb_ref[...]
    out = ((yn + bonus) * g).astype(y_ref.dtype)
    for b in range(nb):
        y_ref[b] = out[b * rows_b:(b + 1) * rows_b]


def _rwkv7(h_rw, mu, w0, w_up, a0, a_up, g_up, k_k, k_a, r_k, ln_g, ln_b, *, nb=SCAN_BATCHES, nc=SCAN_CHUNKS):
    b, l, _ = h_rw.shape
    w = RW_WIDTH
    nb = min(nb, b)
    rows_b = nc * CHUNK
    assert l % rows_b == 0 and b % nb == 0
    lora = jnp.zeros((LANES, 2 * w), F32)
    lora = lora.at[0:RW_DECAY_LORA, 0:w].set(w_up).at[RW_DECAY_LORA:LANES, w:2 * w].set(a_up)
    hid = jnp.arange(w) // RW_HEAD_DIM
    bd = (hid[:, None] == hid[None, :]).astype(BF16)
    vec = lambda t: t.reshape(1, -1).astype(F32)
    params = [vec(mu), vec(w0), vec(a0), vec(k_k), vec(k_a), vec(r_k), vec(ln_g), vec(ln_b),
              lora.astype(BF16), g_up.astype(BF16), bd, _chunk_tril(nb * nc)]
    return pl.pallas_call(
        functools.partial(_rwkv_kernel, nb=nb, nc=nc),
        grid=(b // nb, l // rows_b),
        in_specs=[pl.BlockSpec((nb, rows_b, RW_COLS), lambda i, j: (i, j, 0))]
        + [_const_spec(p.shape) for p in params],
        out_specs=pl.BlockSpec((nb, rows_b, w), lambda i, j: (i, j, 0)),
        out_shape=jax.ShapeDtypeStruct((b, l, w), BF16),
        scratch_shapes=[pltpu.VMEM((nb, SUBLANES + rows_b, RW_COLS), F32),
                        pltpu.VMEM((nb * RW_HEADS // 2, PAIR, PAIR), F32)],
        compiler_params=_cparams(2),
        name="rwkv7",
    )(h_rw, *params)


def _attn_kernel(slope_ref, lam_ref, ng_ref, q_ref, k_ref, v_ref, o_ref, ka_ref, vt_ref, s0_ref, s1_ref, m_ref,
                 acc_ref, *, tq, tk, lam_init):
    d = DA_HEAD_DIM
    dv = 2 * d
    n_blk = vt_ref.shape[0]
    slope2 = slope_ref[0, 0:1, 0:1] * LOG2E

    lane_k = _iota((tk, LANES), 1)
    bias = slope2 * _iota((tk, LANES), 0).astype(F32)
    hi = bias.astype(BF16).astype(F32)
    mid = (bias - hi).astype(BF16).astype(F32)
    low = bias - hi - mid
    pieces = jnp.where(lane_k == 0, hi, jnp.where(lane_k == 1, mid, jnp.where(lane_k == 2, low, 0.0))).astype(BF16)
    ones_row = (_iota((DA_ONES_ROWS, tk), 0) == 0).astype(BF16)
    for jb in range(n_blk):
        rows = slice(jb * tk, (jb + 1) * tk)
        ka_ref[rows, 0:LANES] = k_ref[0, rows, :]
        ka_ref[rows, LANES:2 * LANES] = pieces
        vt_ref[jb, 0:dv, :] = v_ref[0, rows, :].astype(F32).T.astype(BF16)
        vt_ref[jb, dv:dv + DA_ONES_ROWS, :] = ones_row

    lp = lam_ref[...]
    lam = (jnp.exp(jnp.sum(lp[0:1] * lp[1:2], axis=-1, keepdims=True))
           - jnp.exp(jnp.sum(lp[2:3] * lp[3:4], axis=-1, keepdims=True)) + lam_init)
    lane_q = _iota((tq, LANES), 1)
    lo = lane_q < d
    bias_ones = (lane_q < 3).astype(BF16)

    def q_operand(qi):
        q = q_ref[0, qi * tq:(qi + 1) * tq, :].astype(F32) * (d ** -0.5 * LOG2E)
        return jnp.concatenate([
            jnp.concatenate([jnp.where(lo, q, 0.0).astype(BF16), bias_ones], axis=1),
            jnp.concatenate([jnp.where(lo, 0.0, q).astype(BF16), bias_ones], axis=1)], axis=0)

    def scores(q2, j, dst_ref):
        dst_ref[...] = lax.dot_general(ka_ref[j * tk:(j + 1) * tk, :], q2, (((1,), (1,)), ((), ())),
                                       preferred_element_type=F32)

    def consume(qi, j, src_ref):
        shift = slope2 * float(j * tk - qi * tq)
        s = src_ref[...]
        if j == qi:
            s = jnp.where(_iota((tk, 2 * tq), 0) <= (_iota((tk, 2 * tq), 1) & (tq - 1)), s, -jnp.inf)
        if j == 0:
            m_new = jnp.max(s, axis=0, keepdims=True) + shift
            p = jnp.exp2(s - (m_new - shift)).astype(BF16)
            acc_ref[...] = jnp.dot(vt_ref[j], p, preferred_element_type=F32)
        else:
            m_old = m_ref[...]
            m_new = jnp.maximum(m_old, jnp.max(s, axis=0, keepdims=True) + shift)
            p = jnp.exp2(s - (m_new - shift)).astype(BF16)
            acc_ref[...] = (jnp.exp2(m_old - m_new) * acc_ref[...]
                            + jnp.dot(vt_ref[j], p, preferred_element_type=F32))
        m_ref[...] = m_new

    def finalize(qi):
        acc = acc_ref[...]
        ot = acc[0:dv] / acc[dv:dv + 1]
        ot = ot[:, 0:tq] - lam * ot[:, tq:2 * tq]
        ot = ot * lax.rsqrt(jnp.mean(ot * ot, axis=0, keepdims=True) + 1e-5)
        o_ref[0, qi * tq:(qi + 1) * tq, :] = (ot.T * ng_ref[...] * (1.0 - lam_init)).astype(o_ref.dtype)

    pairs = [(qi, j) for qi in range(n_blk) for j in range(qi + 1)]
    bufs = (s0_ref, s1_ref)
    q2 = {0: q_operand(0)}
    scores(q2[0], 0, bufs[0])
    for idx, (qi, j) in enumerate(pairs):
        if idx + 1 < len(pairs):
            nqi, nj = pairs[idx + 1]
            if nqi not in q2:
                q2[nqi] = q_operand(nqi)
            scores(q2[nqi], nj, bufs[(idx + 1) % 2])
        consume(qi, j, bufs[idx % 2])
        if j == qi:
            finalize(qi)


def _diff_attention(h_da, lam_q1, lam_k1, lam_q2, lam_k2, norm_g, lam_init, *, tq=512):
    b, l, _ = h_da.shape
    tq = min(tq, l)
    tk = tq
    assert l % tq == 0 and tq & (tq - 1) == 0
    hh = DA_HEADS
    slopes = jnp.exp2(-8.0 * jnp.arange(1, hh + 1, dtype=F32) / hh)
    slopes = jnp.broadcast_to(slopes[:, None, None], (hh, SUBLANES, LANES))
    lam_p = jnp.stack([lam_q1, lam_k1, lam_q2, lam_k2]).astype(F32)
    ng = norm_g.reshape(1, 2 * DA_HEAD_DIM).astype(F32)
    seq = lambda col0: pl.BlockSpec((1, l, LANES), lambda i, h: (i, 0, col0 + h))
    return pl.pallas_call(
        functools.partial(_attn_kernel, tq=tq, tk=tk, lam_init=lam_init),
        grid=(b, hh),
        in_specs=[pl.BlockSpec((1, SUBLANES, LANES), lambda i, h: (h, 0, 0)),
                  pl.BlockSpec(lam_p.shape, lambda i, h: (0, 0)),
                  pl.BlockSpec(ng.shape, lambda i, h: (0, 0)),
                  seq(0), seq(hh), seq(2 * hh)],
        out_specs=seq(0),
        out_shape=jax.ShapeDtypeStruct((b, l, DA_WIDTH), BF16),
        scratch_shapes=[pltpu.VMEM((l, 2 * LANES), BF16),
                        pltpu.VMEM((l // tk, LANES + DA_ONES_ROWS, tk), BF16),
                        pltpu.VMEM((tk, 2 * tq), F32), pltpu.VMEM((tk, 2 * tq), F32),
                        pltpu.VMEM((1, 2 * tq), F32), pltpu.VMEM((LANES + DA_ONES_ROWS, 2 * tq), F32)],
        compiler_params=_cparams(2),
        name="diffattn",
    )(slopes, lam_p, ng, h_da, h_da, h_da)


def _softplus(x):
    return jnp.maximum(x, 0.0) + jnp.log(1.0 + jnp.exp(-jnp.abs(x)))


def _gdn_kernel(h_ref, cw_ref, alog_ref, dtb_ref, ng_ref, tril_ref, y_ref, prev_ref, s_ref, *, nb, nc):
    rows_b = nc * CHUNK
    wq = 3 * GD_WIDTH

    @pl.when(pl.program_id(1) == 0)
    def _():
        prev_ref[:, 0:SUBLANES, :] = jnp.zeros((nb, SUBLANES, wq), F32)
        s_ref[...] = jnp.zeros_like(s_ref)

    dh = GD_HEAD_DIM
    cw = cw_ref[...]
    parts = []
    for b in range(nb):
        x = h_ref[b, :, 0:wq]
        prev_ref[b, SUBLANES:SUBLANES + rows_b, :] = x
        conv = x * cw[GD_CONV - 1:GD_CONV]
        for j in range(GD_CONV - 1):
            r0 = SUBLANES - (GD_CONV - 1 - j)
            conv = conv + prev_ref[b, r0:r0 + rows_b, :] * cw[j:j + 1]
        prev_ref[b, 0:SUBLANES, :] = x[rows_b - SUBLANES:rows_b]
        parts.append(conv)
    conv = jnp.concatenate(parts, axis=0) if nb > 1 else parts[0]
    qkv = conv * jax.nn.sigmoid(conv)

    ab = jnp.concatenate([h_ref[b, :, wq:wq + GD_AB_PAD] for b in range(nb)], axis=0)
    gfull = -jnp.exp(alog_ref[...]) * _softplus(ab + dtb_ref[...])
    gcum = _split_rhs_dot(tril_ref[...], gfull, 3)
    glast = _rows_of_chunk(gcum, CHUNK - 1)
    beta_full = jax.nn.sigmoid(ab)

    def l2n(t):
        return t * lax.rsqrt(jnp.sum(t * t, axis=-1, keepdims=True) + 1e-6)

    qn = [l2n(qkv[:, h * dh:(h + 1) * dh]) * dh ** -0.5 for h in range(GD_HEADS)]
    kn = [l2n(qkv[:, GD_WIDTH + h * dh:GD_WIDTH + (h + 1) * dh]) for h in range(GD_HEADS)]
    vh = [qkv[:, 2 * GD_WIDTH + h * dh:2 * GD_WIDTH + (h + 1) * dh] for h in range(GD_HEADS)]

    r0 = _iota((PAIR, PAIR), 0)
    c0 = _iota((PAIR, PAIR), 1)
    same = (r0 >= CHUNK) == (c0 >= CHUNK)
    incl = same & (r0 >= c0)
    strict = same & (r0 > c0)

    n_pairs = GD_HEADS // 2
    units = [(b, ci, p) for b in range(nb) for ci in range(nc) for p in range(n_pairs)]

    def stack(per_head, u, lane_of=None):
        b, ci, p = u
        rr = slice((b * nc + ci) * CHUNK, (b * nc + ci + 1) * CHUNK)
        if lane_of is None:
            return jnp.concatenate([per_head[2 * p][rr], per_head[2 * p + 1][rr]], axis=0)
        return jnp.concatenate([per_head[rr, lane_of + 2 * p:lane_of + 2 * p + 1],
                                per_head[rr, lane_of + 2 * p + 1:lane_of + 2 * p + 2]], axis=0)

    q_s = [stack(qn, u) for u in units]
    k_s = [stack(kn, u) for u in units]
    v_s = [stack(vh, u) for u in units]
    g_s = [stack(gcum, u, 0) for u in units]
    gl_s = [stack(glast, u, 0) for u in units]
    beta_s = [stack(beta_full, u, GD_HEADS) for u in units]

    decay = []
    for gs in g_s:
        g_b = jnp.broadcast_to(gs, (PAIR, PAIR))
        decay.append(jnp.exp(jnp.where(incl, g_b - g_b.T, -jnp.inf)))
    kb = [x * y for x, y in zip(k_s, beta_s)]
    kk = [_mm_nt(jnp.concatenate([x, y], axis=0), z) for x, y, z in zip(kb, q_s, k_s)]
    low = [jnp.where(strict, x[0:PAIR] * dc, 0.0) for x, dc in zip(kk, decay)]
    intra = [x[PAIR:2 * PAIR] * dc for x, dc in zip(kk, decay)]
    inv = _inv_unit_lower_many(low)
    eg = [jnp.exp(gs) for gs in g_s]
    uw = [_mm(i, jnp.concatenate([v * bt, x * e], axis=1)) for i, v, bt, x, e in zip(inv, v_s, beta_s, kb, eg)]
    qg = [x * e for x, e in zip(q_s, eg)]
    kd = [x * jnp.exp(gl - gs) for x, gl, gs in zip(k_s, gl_s, g_s)]

    v_new = {}
    o_state = {}
    chains = [(b, h) for b in range(nb) for h in range(GD_HEADS)]
    state = {bh: s_ref[bh[0] * GD_HEADS + bh[1]] for bh in chains}
    for ci in range(nc):
        ws = {}
        for b, h in chains:
            p, i_h = divmod(h, 2)
            rows = slice(i_h * CHUNK, (i_h + 1) * CHUNK)
            i = units.index((b, ci, p))
            ws[(b, h)] = _mm(jnp.concatenate([uw[i][rows, dh:2 * dh], qg[i][rows]], axis=0), state[(b, h)])
        for b, h in chains:
            p, i_h = divmod(h, 2)
            rows = slice(i_h * CHUNK, (i_h + 1) * CHUNK)
            i = units.index((b, ci, p))
            vn = uw[i][rows, 0:dh] - ws[(b, h)][0:CHUNK]
            v_new[(b, ci, h)] = vn
            o_state[(b, ci, h)] = ws[(b, h)][CHUNK:PAIR]
            state[(b, h)] = state[(b, h)] * jnp.exp(gl_s[i][rows][0:1]) + _mm_tn(kd[i][rows], vn)
    for b, h in chains:
        s_ref[b * GD_HEADS + h] = state[(b, h)]

    for i, (b, ci, p) in enumerate(units):
        vn = jnp.concatenate([v_new[(b, ci, 2 * p)], v_new[(b, ci, 2 * p + 1)]], axis=0)
        o = jnp.concatenate([o_state[(b, ci, 2 * p)], o_state[(b, ci, 2 * p + 1)]], axis=0) + _mm(intra[i], vn)
        for i_h in range(2):
            h = 2 * p + i_h
            oh = o[i_h * CHUNK:(i_h + 1) * CHUNK]
            oh = oh * lax.rsqrt(jnp.mean(oh * oh, axis=-1, keepdims=True) + 1e-6) * ng_ref[...]
            zc = wq + GD_AB_PAD + h * dh
            z = h_ref[b, ci * CHUNK:(ci + 1) * CHUNK, zc:zc + dh]
            y_ref[b, ci * CHUNK:(ci + 1) * CHUNK, h * dh:(h + 1) * dh] = (
                oh * (z * jax.nn.sigmoid(z))).astype(y_ref.dtype)


def _gated_deltanet(h_gd, conv_w, a_log, dt_bias, norm_g, *, nb=SCAN_BATCHES, nc=SCAN_CHUNKS):
    b, l, _ = h_gd.shape
    nb = min(nb, b)
    rows_b = nc * CHUNK
    assert l % rows_b == 0 and b % nb == 0
    pad = lambda t: jnp.zeros((1, GD_AB_PAD), F32).at[0, 0:GD_HEADS].set(t.astype(F32))
    params = [conv_w.astype(F32), pad(a_log), pad(dt_bias), norm_g.reshape(1, GD_HEAD_DIM).astype(F32),
              _chunk_tril(nb * nc)]
    return pl.pallas_call(
        functools.partial(_gdn_kernel, nb=nb, nc=nc),
        grid=(b // nb, l // rows_b),
        in_specs=[pl.BlockSpec((nb, rows_b, GD_IN), lambda i, j: (i, j, 0))]
        + [_const_spec(p.shape) for p in params],
        out_specs=pl.BlockSpec((nb, rows_b, GD_WIDTH), lambda i, j: (i, j, 0)),
        out_shape=jax.ShapeDtypeStruct((b, l, GD_WIDTH), BF16),
        scratch_shapes=[pltpu.VMEM((nb, SUBLANES + rows_b, 3 * GD_WIDTH), F32),
                        pltpu.VMEM((nb * GD_HEADS, GD_HEAD_DIM, GD_HEAD_DIM), F32)],
        compiler_params=_cparams(2),
        name="gdn",
    )(h_gd, *params)


def _merge_ln_kernel(x_ref, xb_ref, ya_ref, yb_ref, yc_ref, wgate_ref, wbr_ref, wout_ref, g_ref, b_ref,
                     y_ref, y16_ref):
    xb = xb_ref[...]
    merged = None
    for n, br_ref in enumerate((ya_ref, yb_ref, yc_ref)):
        gate = jax.nn.sigmoid(jnp.dot(xb, wgate_ref[n], preferred_element_type=F32))
        term = gate * jnp.dot(br_ref[...], wbr_ref[n], preferred_element_type=F32)
        merged = term if merged is None else merged + term
    mix = jnp.dot(merged.astype(BF16), wout_ref[...], preferred_element_type=F32)
    y = _layernorm(ALPHA * x_ref[...] + mix, g_ref[...], b_ref[...])
    y_ref[...] = y
    y16_ref[...] = y.astype(BF16)


def _merge_ln(x, xb, ya, yb, yc, w_gate, w_branch, w_out, g, b, *, tm=512):
    t, d = x.shape
    bw = ya.shape[-1]
    row = pl.BlockSpec((tm, d), lambda i: (i, 0))
    brow = pl.BlockSpec((tm, bw), lambda i: (i, 0))
    return pl.pallas_call(
        _merge_ln_kernel,
        grid=(t // tm,),
        in_specs=[row, row, brow, brow, brow, _const_spec((N_BRANCH, d, d)),
                  _const_spec((N_BRANCH, bw, d)), _const_spec((d, d)), _const_spec((1, d)), _const_spec((1, d))],
        out_specs=[row, row],
        out_shape=[jax.ShapeDtypeStruct((t, d), F32), jax.ShapeDtypeStruct((t, d), BF16)],
        compiler_params=_cparams(1),
        name="merge_ln",
    )(x, xb, ya, yb, yc, w_gate, w_branch, w_out, g.reshape(1, d), b.reshape(1, d))


def _split_mix_w_in(w_in):
    d = w_in.shape[0]
    o1 = RW_COLS
    o2 = o1 + DA_COLS
    o3 = o2 + GD_COLS
    gq = o2 + 3 * GD_WIDTH
    ab = jnp.zeros((d, GD_AB_PAD), w_in.dtype).at[:, 0:2 * GD_HEADS].set(w_in[:, gq:gq + 2 * GD_HEADS])
    w_cat = jnp.concatenate([w_in[:, 0:o2], w_in[:, o2:gq], ab, w_in[:, gq + 2 * GD_HEADS:o3]], axis=1)
    w_gate = w_in[:, o3:].reshape(d, N_BRANCH, d).transpose(1, 0, 2)
    return w_cat.astype(BF16), w_gate.astype(BF16)


def kernel(x, ffn1_w_in, ffn1_w_out, ln1_g, ln1_b, mix_w_in, rw_shift_mu, rw_w0, rw_w_up, rw_a0, rw_a_up, rw_g_up, rw_k_k, rw_k_a, rw_r_k, rw_ln_g, rw_ln_b, da_lam_q1, da_lam_k1, da_lam_q2, da_lam_k2, da_norm_g, gd_conv_w, gd_a_log, gd_dt_bias, gd_norm_g, mix_w_branch, mix_w_out, ln2_g, ln2_b, ffn2_w_in, ffn2_w_out, ln3_g, ln3_b):
    b, l, d = x.shape
    t = b * l
    xf = x.reshape(t, d)
    xb = xf.astype(BF16)
    for i in range(DEPTH):
        xf, xb = _ffn_ln(xf, xb, ffn1_w_in[i], ffn1_w_out[i], ln1_g[i], ln1_b[i])
        w_cat, w_gate = _split_mix_w_in(mix_w_in[i])
        h_rw, h_da, h_gd = _mix_in(xb, w_cat)
        ya = _rwkv7(h_rw.reshape(b, l, RW_COLS), rw_shift_mu[i], rw_w0[i], rw_w_up[i], rw_a0[i], rw_a_up[i],
                    rw_g_up[i], rw_k_k[i], rw_k_a[i], rw_r_k[i].reshape(-1), rw_ln_g[i], rw_ln_b[i])
        lam_init = 0.8 - 0.6 * math.exp(-0.3 * i)
        yb = _diff_attention(h_da.reshape(b, l, DA_COLS), da_lam_q1[i], da_lam_k1[i], da_lam_q2[i],
                             da_lam_k2[i], da_norm_g[i], lam_init)
        yc = _gated_deltanet(h_gd.reshape(b, l, GD_IN), gd_conv_w[i], gd_a_log[i], gd_dt_bias[i], gd_norm_g[i])
        xf, xb = _merge_ln(xf, xb, ya.reshape(t, RW_WIDTH), yb.reshape(t, DA_WIDTH), yc.reshape(t, GD_WIDTH),
                           w_gate, mix_w_branch[i].astype(BF16), mix_w_out[i].astype(BF16), ln2_g[i], ln2_b[i])
        xf, xb = _ffn_ln(xf, xb, ffn2_w_in[i], ffn2_w_out[i], ln3_g[i], ln3_b[i])
    return xf.reshape(b, l, d)
```

```python
import functools
import math

import jax
import jax.numpy as jnp
from jax import lax
from jax.experimental import pallas as pl
from jax.experimental.pallas import tpu as pltpu

F32 = jnp.float32
BF16 = jnp.bfloat16

D_MODEL = 1024
DEPTH = 2
D_FF = 2816
RW_HEADS = 8
RW_HEAD_DIM = 64
RW_WIDTH = 512
RW_DECAY_LORA = 64
RW_ICLR_LORA = 64
RW_GATE_LORA = 128
RW_COLS = 3 * RW_WIDTH + RW_DECAY_LORA + RW_ICLR_LORA + RW_GATE_LORA
RW_GN_EPS = 64e-5
DA_HEADS = 4
DA_HEAD_DIM = 64
DA_WIDTH = 512
DA_COLS = 3 * DA_WIDTH
GD_HEADS = 4
GD_HEAD_DIM = 128
GD_WIDTH = 512
GD_CONV = 4
GD_COLS = 3 * GD_WIDTH + 2 * GD_HEADS + GD_WIDTH
N_BRANCH = 3
ALPHA = (2.0 * DEPTH) ** 0.25

LANES = 128
SUBLANES = 8
CHUNK = 64
PAIR = 2 * CHUNK
DA_ONES_ROWS = 16
LOG2E = math.log2(math.e)
GD_AB_PAD = LANES
GD_IN = 3 * GD_WIDTH + GD_AB_PAD + GD_WIDTH
VMEM_LIMIT = 56 * 1024 * 1024
SCAN_BATCHES = 2
SCAN_CHUNKS = 2


def _cparams(n_grid):
    return pltpu.CompilerParams(dimension_semantics=("arbitrary",) * n_grid,
                                vmem_limit_bytes=VMEM_LIMIT)


def _const_spec(shape):
    nd = len(shape)
    return pl.BlockSpec(shape, lambda *_: (0,) * nd, pipeline_mode=pl.Buffered(1))


def _mm(a, b):
    return jnp.dot(a.astype(BF16), b.astype(BF16), preferred_element_type=F32)


def _mm_nt(a, b):
    return lax.dot_general(a.astype(BF16), b.astype(BF16), (((1,), (1,)), ((), ())),
                           preferred_element_type=F32)


def _mm_tn(a, b):
    return lax.dot_general(a.astype(BF16), b.astype(BF16), (((0,), (0,)), ((), ())),
                           preferred_element_type=F32)


def _split_lhs_dot(x, exact_rhs, terms):
    acc = None
    rem = x
    for _ in range(terms):
        hi = rem.astype(BF16)
        part = jnp.dot(hi, exact_rhs, preferred_element_type=F32)
        acc = part if acc is None else acc + part
        rem = rem - hi.astype(F32)
    return acc


def _split_rhs_dot(exact_lhs, x, terms):
    acc = None
    rem = x
    for _ in range(terms):
        hi = rem.astype(BF16)
        part = jnp.dot(exact_lhs, hi, preferred_element_type=F32)
        acc = part if acc is None else acc + part
        rem = rem - hi.astype(F32)
    return acc


def _layernorm(z, g, b, eps=1e-5):
    mu = jnp.mean(z, axis=-1, keepdims=True)
    zc = z - mu
    var = jnp.mean(zc * zc, axis=-1, keepdims=True)
    return zc * lax.rsqrt(var + eps) * g + b


def _iota(shape, dim):
    return lax.broadcasted_iota(jnp.int32, shape, dim)


def _inv_unit_lower_many(lows):
    eye = (_iota((PAIR, PAIR), 0) == _iota((PAIR, PAIR), 1)).astype(F32)
    pw = [-low for low in lows]
    inv = [eye + n for n in pw]
    for _ in range(int(math.log2(CHUNK)) - 1):
        pw = [_mm(p, p) for p in pw]
        inv = [i + _mm(i, p) for i, p in zip(inv, pw)]
    return inv


def _stack_heads(x, lo_mask):
    return jnp.concatenate([jnp.where(lo_mask, x, 0.0), jnp.where(lo_mask, 0.0, x)], axis=0)


def _chunk_tril(n_chunks):
    r = jnp.arange(n_chunks * CHUNK)
    return ((r[:, None] >= r[None, :]) & (r[:, None] // CHUNK == r[None, :] // CHUNK)).astype(BF16)


def _rows_of_chunk(x, row):
    n = x.shape[0] // CHUNK
    parts = [jnp.broadcast_to(x[i * CHUNK + row:i * CHUNK + row + 1], (CHUNK,) + x.shape[1:]) for i in range(n)]
    return jnp.concatenate(parts, axis=0) if n > 1 else parts[0]


def _ffn_ln_kernel(x_ref, *refs, tf):
    if len(refs) == 8:
        xb_ref, win_ref, wo_ref, g_ref, b_ref, y_ref, yb_ref, acc_ref = refs
        xb = xb_ref[...]
    else:
        win_ref, wo_ref, g_ref, b_ref, y_ref, yb_ref, acc_ref = refs
        xb = x_ref[...].astype(BF16)
    dff = wo_ref.shape[0]
    nf = dff // tf

    def gate_up(f):
        cols = slice(f * tf, (f + 1) * tf)
        return (jnp.dot(xb, win_ref[:, cols], preferred_element_type=F32),
                jnp.dot(xb, win_ref[:, dff + f * tf:dff + (f + 1) * tf], preferred_element_type=F32))

    nxt = gate_up(0)
    for f in range(nf):
        gate, up = nxt
        if f + 1 < nf:
            nxt = gate_up(f + 1)
        act = (gate * jax.nn.sigmoid(gate) * up).astype(BF16)
        part = jnp.dot(act, wo_ref[f * tf:(f + 1) * tf, :], preferred_element_type=F32)
        if f == 0:
            acc_ref[...] = part
        else:
            acc_ref[...] += part
    y = _layernorm(ALPHA * x_ref[...] + 0.5 * acc_ref[...], g_ref[...], b_ref[...])
    y_ref[...] = y
    yb_ref[...] = y.astype(BF16)


def _ffn_ln(x, xb, w_in, w_out, g, b, *, tm=1024, tf=256):
    t, d = x.shape
    dff = w_out.shape[0]
    assert dff % tf == 0 and t % tm == 0
    row = pl.BlockSpec((tm, d), lambda i: (i, 0))
    acts = [x] if xb is None else [x, xb]
    return pl.pallas_call(
        functools.partial(_ffn_ln_kernel, tf=tf),
        grid=(t // tm,),
        in_specs=[row] * len(acts) + [_const_spec((d, 2 * dff)), _const_spec((dff, d)), _const_spec((1, d)),
                                      _const_spec((1, d))],
        out_specs=[row, row],
        out_shape=[jax.ShapeDtypeStruct((t, d), F32), jax.ShapeDtypeStruct((t, d), BF16)],
        scratch_shapes=[pltpu.VMEM((tm, d), F32)],
        compiler_params=_cparams(1),
        name="ffn_ln",
    )(*acts, w_in.astype(BF16), w_out.astype(BF16), g.reshape(1, d), b.reshape(1, d))


def _mix_in_kernel(xb_ref, w_ref, cw_ref, rw_ref, da_ref, gd_ref, stage_ref, *, tn, tiles_per_seq):
    @pl.when(pl.program_id(0) % tiles_per_seq == 0)
    def _():
        stage_ref[:, 0:SUBLANES, :] = jnp.zeros((stage_ref.shape[0], SUBLANES, tn), F32)

    xb = xb_ref[...]
    tm = xb.shape[0]
    dh = GD_HEAD_DIM
    chunks = []
    off = 0
    for out_ref in (rw_ref, da_ref, gd_ref):
        width = out_ref.shape[-1]
        chunks += [(out_ref, off, c0, min(c0 + tn, width)) for c0 in range(0, width, tn)]
        off += width
    project = lambda ch: jnp.dot(xb, w_ref[:, ch[1] + ch[2]:ch[1] + ch[3]], preferred_element_type=F32)
    nxt = project(chunks[0])
    for n, (out_ref, off, c0, c1) in enumerate(chunks):
        res = nxt
        if n + 1 < len(chunks):
            nxt = project(chunks[n + 1])
        if out_ref is gd_ref and c1 <= 3 * GD_WIDTH:
            c = c0 // tn
            stage_ref[c, SUBLANES:SUBLANES + tm, :] = res
            cw = cw_ref[:, c0:c1]
            conv = res * cw[GD_CONV - 1:GD_CONV]
            for j in range(GD_CONV - 1):
                r0 = SUBLANES - (GD_CONV - 1 - j)
                conv = conv + stage_ref[c, r0:r0 + tm, :] * cw[j:j + 1]
            stage_ref[c, 0:SUBLANES, :] = res[tm - SUBLANES:tm]
            res = conv * jax.nn.sigmoid(conv)
            if c1 <= 2 * GD_WIDTH:
                heads = []
                for h0 in range(0, tn, dh):
                    t = res[:, h0:h0 + dh]
                    t = t * lax.rsqrt(jnp.sum(t * t, axis=-1, keepdims=True) + 1e-6)
                    heads.append(t * dh ** -0.5 if c1 <= GD_WIDTH else t)
                res = jnp.concatenate(heads, axis=1)
        out_ref[:, c0:c1] = res.astype(out_ref.dtype)


def _mix_in(xb, w_cat, conv_w, seq_len, *, tm=512, tn=256):
    t, d = xb.shape
    n = w_cat.shape[1]
    assert n == RW_COLS + DA_COLS + GD_IN and t % tm == 0 and seq_len % tm == 0 and GD_WIDTH % tn == 0
    return pl.pallas_call(
        functools.partial(_mix_in_kernel, tn=tn, tiles_per_seq=seq_len // tm),
        grid=(t // tm,),
        in_specs=[pl.BlockSpec((tm, d), lambda i: (i, 0)), _const_spec((d, n)), _const_spec(conv_w.shape)],
        out_specs=[pl.BlockSpec((tm, RW_COLS), lambda i: (i, 0)),
                   pl.BlockSpec((tm, DA_COLS), lambda i: (i, 0)),
                   pl.BlockSpec((tm, GD_IN), lambda i: (i, 0))],
        out_shape=[jax.ShapeDtypeStruct((t, RW_COLS), F32),
                   jax.ShapeDtypeStruct((t, DA_COLS), BF16),
                   jax.ShapeDtypeStruct((t, GD_IN), F32)],
        scratch_shapes=[pltpu.VMEM((3 * GD_WIDTH // tn, SUBLANES + tm, tn), F32)],
        compiler_params=_cparams(1),
        name="mix_in",
    )(xb, w_cat, conv_w.astype(F32))


def _rwkv_kernel(h_ref, mu_ref, w0_ref, a0_ref, kk_ref, ka_ref, rk_ref, lng_ref, lnb_ref,
                 lora_ref, gup_ref, bd_ref, tril_ref, y_ref, prev_ref, s_ref, *, nb, nc):
    rows_b = nc * CHUNK

    @pl.when(pl.program_id(1) == 0)
    def _():
        prev_ref[:, 0:SUBLANES, :] = jnp.zeros((nb, SUBLANES, RW_COLS), F32)
        s_ref[...] = jnp.zeros_like(s_ref)

    n_pairs = RW_HEADS // 2
    parts = []
    for b in range(nb):
        h = h_ref[b]
        prev_ref[b, SUBLANES:SUBLANES + rows_b, :] = h
        hprev = prev_ref[b, SUBLANES - 1:SUBLANES - 1 + rows_b, :]
        prev_ref[b, 0:SUBLANES, :] = h[rows_b - SUBLANES:rows_b]
        parts.append(h + mu_ref[...] * (hprev - h))
    hs = jnp.concatenate(parts, axis=0) if nb > 1 else parts[0]
    n_rows = nb * rows_b

    w = RW_WIDTH
    r = hs[:, 0:w]
    k = hs[:, w:2 * w]
    v = hs[:, 2 * w:3 * w]
    wa = hs[:, 3 * w:3 * w + LANES]
    gd = hs[:, 3 * w + LANES:3 * w + 2 * LANES]
    lo_n = _iota((n_rows, LANES), 1) < RW_HEAD_DIM
    lora = _mm(jnp.where(lo_n, jnp.tanh(wa), wa), lora_ref[...])
    logw = -math.exp(-0.5) * jax.nn.sigmoid(w0_ref[...] + lora[:, 0:w])
    a = jax.nn.sigmoid(a0_ref[...] + lora[:, w:2 * w])
    g = _mm(jax.nn.sigmoid(gd), gup_ref[...])

    bd = bd_ref[...]
    kkr = k * kk_ref[...]
    kk = kkr * lax.rsqrt(_split_lhs_dot(kkr * kkr, bd, 1) + 1e-6)
    k2 = k * (1.0 + (a - 1.0) * ka_ref[...])
    bvec = kk * a
    bonus = _split_lhs_dot(r * k2 * rk_ref[...], bd, 2) * v

    e_in = _split_rhs_dot(tril_ref[...], logw, 2)
    rho = _rows_of_chunk(e_in, CHUNK // 2 - 1)
    r_t = r * jnp.exp(e_in - rho)
    kk_t = kk * jnp.exp(e_in - logw - rho)
    e_neg = jnp.exp(rho - e_in)
    k_h = k2 * e_neg
    b_h = bvec * e_neg

    lane = _iota((CHUNK, LANES), 1)
    lo = lane < RW_HEAD_DIM
    tt = _iota((CHUNK, LANES), 0)
    ss = lane & (CHUNK - 1)
    strict = tt > ss
    incl = tt >= ss

    units = [(b, ci, p) for b in range(nb) for ci in range(nc) for p in range(n_pairs)]

    def tile(x, u):
        b, ci, p = u
        r0 = (b * nc + ci) * CHUNK
        return x[r0:r0 + CHUNK, p * LANES:(p + 1) * LANES]

    kks = [_stack_heads(tile(kk_t, u), lo) for u in units]
    rs = [_stack_heads(tile(r_t, u), lo) for u in units]
    vs = [_stack_heads(tile(v, u), lo) for u in units]
    khs = [_stack_heads(tile(k_h, u), lo) for u in units]
    bhs = [_stack_heads(tile(b_h, u), lo) for u in units]
    z = [_mm_nt(jnp.concatenate([tile(kk_t, u), tile(r_t, u)], axis=0), jnp.concatenate([kh, bh], axis=0))
         for u, kh, bh in zip(units, khs, bhs)]
    ak = [_stack_heads(jnp.where(strict, zz[0:CHUNK, 0:PAIR], 0.0), lo) for zz in z]
    ab = [_stack_heads(jnp.where(strict, zz[0:CHUNK, PAIR:2 * PAIR], 0.0), lo) for zz in z]
    bk = [_stack_heads(jnp.where(incl, zz[CHUNK:PAIR, 0:PAIR], 0.0), lo) for zz in z]
    bb = [_stack_heads(jnp.where(incl, zz[CHUNK:PAIR, PAIR:2 * PAIR], 0.0), lo) for zz in z]
    inv = _inv_unit_lower_many(ab)
    akv = [_mm(x, y) for x, y in zip(ak, vs)]
    bkv = [_mm(x, y) for x, y in zip(bk, vs)]
    pq = [_mm(i, jnp.concatenate([x, y], axis=1)) for i, x, y in zip(inv, kks, akv)]
    bpq = [_mm(x, y) for x, y in zip(bb, pq)]
    rp = [x - y[:, 0:PAIR] for x, y in zip(rs, bpq)]
    y0 = [x - y[:, PAIR:2 * PAIR] for x, y in zip(bkv, bpq)]
    ptb = [_mm_tn(x[:, 0:PAIR], y) for x, y in zip(pq, bhs)]
    hm = [_mm_tn(jnp.concatenate([x, y[:, PAIR:2 * PAIR]], axis=0), jnp.concatenate([kh, -bh], axis=0))
          for x, y, kh, bh in zip(vs, pq, khs, bhs)]

    s_scale = jnp.exp(rho)
    d_out = jnp.exp(_rows_of_chunk(e_in, CHUNK - 1) - rho)
    yo = {}
    chains = [(b, p) for b in range(nb) for p in range(n_pairs)]
    state = {bp: s_ref[bp[0] * n_pairs + bp[1]] for bp in chains}
    for ci in range(nc):
        for b, p in chains:
            i = units.index((b, ci, p))
            r0 = (b * nc + ci) * CHUNK
            sl = slice(p * LANES, (p + 1) * LANES)
            sp = state[(b, p)] * s_scale[r0:r0 + 1, sl]
            yo[(b, ci, p)] = _mm_nt(rp[i], sp) + y0[i]
            state[(b, p)] = (sp - _mm(sp, ptb[i]) + hm[i]) * d_out[r0:r0 + 1, sl]
    for b, p in chains:
        s_ref[b * n_pairs + p] = state[(b, p)]
    y = jnp.concatenate(
        [jnp.concatenate([yo[(b, ci, p)][0:CHUNK] + yo[(b, ci, p)][CHUNK:PAIR] for p in range(n_pairs)], axis=1)
         for b in range(nb) for ci in range(nc)], axis=0)

    inv_n = 1.0 / RW_HEAD_DIM
    ym = _split_lhs_dot(y, bd, 1) * inv_n
    yc = y - ym
    yv = _split_lhs_dot(yc * yc, bd, 1) * inv_n
    yn = yc * lax.rsqrt(yv + RW_GN_EPS) * lng_ref[...] + lnb_ref[...]
    out = ((yn + bonus) * g).astype(y_ref.dtype)
    for b in range(nb):
        y_ref[b] = out[b * rows_b:(b + 1) * rows_b]


def _rwkv7(h_rw, mu, w0, w_up, a0, a_up, g_up, k_k, k_a, r_k, ln_g, ln_b, *, nb=SCAN_BATCHES, nc=SCAN_CHUNKS):
    b, l, _ = h_rw.shape
    w = RW_WIDTH
    nb = min(nb, b)
    rows_b = nc * CHUNK
    assert l % rows_b == 0 and b % nb == 0
    lora = jnp.zeros((LANES, 2 * w), F32)
    lora = lora.at[0:RW_DECAY_LORA, 0:w].set(w_up).at[RW_DECAY_LORA:LANES, w:2 * w].set(a_up)
    hid = jnp.arange(w) // RW_HEAD_DIM
    bd = (hid[:, None] == hid[None, :]).astype(BF16)
    vec = lambda t: t.reshape(1, -1).astype(F32)
    params = [vec(mu), vec(w0), vec(a0), vec(k_k), vec(k_a), vec(r_k), vec(ln_g), vec(ln_b),
              lora.astype(BF16), g_up.astype(BF16), bd, _chunk_tril(nb * nc)]
    return pl.pallas_call(
        functools.partial(_rwkv_kernel, nb=nb, nc=nc),
        grid=(b // nb, l // rows_b),
        in_specs=[pl.BlockSpec((nb, rows_b, RW_COLS), lambda i, j: (i, j, 0))]
        + [_const_spec(p.shape) for p in params],
        out_specs=pl.BlockSpec((nb, rows_b, w), lambda i, j: (i, j, 0)),
        out_shape=jax.ShapeDtypeStruct((b, l, w), BF16),
        scratch_shapes=[pltpu.VMEM((nb, SUBLANES + rows_b, RW_COLS), F32),
                        pltpu.VMEM((nb * RW_HEADS // 2, PAIR, PAIR), F32)],
        compiler_params=_cparams(2),
        name="rwkv7",
    )(h_rw, *params)


def _attn_kernel(slope_ref, lam_ref, ng_ref, q_ref, k_ref, v_ref, o_ref, ka_ref, vt_ref, s0_ref, s1_ref, m_ref,
                 acc_ref, *, tq, tk, lam_init):
    d = DA_HEAD_DIM
    dv = 2 * d
    n_blk = vt_ref.shape[0]
    slope2 = slope_ref[0, 0:1, 0:1] * LOG2E

    lane_k = _iota((tk, LANES), 1)
    bias = slope2 * _iota((tk, LANES), 0).astype(F32)
    hi = bias.astype(BF16).astype(F32)
    mid = (bias - hi).astype(BF16).astype(F32)
    low = bias - hi - mid
    pieces = jnp.where(lane_k == 0, hi, jnp.where(lane_k == 1, mid, jnp.where(lane_k == 2, low, 0.0))).astype(BF16)
    ones_row = (_iota((DA_ONES_ROWS, tk), 0) == 0).astype(BF16)
    for jb in range(n_blk):
        rows = slice(jb * tk, (jb + 1) * tk)
        ka_ref[rows, 0:LANES] = k_ref[0, rows, :]
        ka_ref[rows, LANES:2 * LANES] = pieces
        vt_ref[jb, 0:dv, :] = v_ref[0, rows, :].astype(F32).T.astype(BF16)
        vt_ref[jb, dv:dv + DA_ONES_ROWS, :] = ones_row

    lp = lam_ref[...]
    lam = (jnp.exp(jnp.sum(lp[0:1] * lp[1:2], axis=-1, keepdims=True))
           - jnp.exp(jnp.sum(lp[2:3] * lp[3:4], axis=-1, keepdims=True)) + lam_init)
    lane_q = _iota((tq, LANES), 1)
    lo = lane_q < d
    bias_ones = (lane_q < 3).astype(BF16)

    def q_operand(qi):
        q = q_ref[0, qi * tq:(qi + 1) * tq, :].astype(F32) * (d ** -0.5 * LOG2E)
        return jnp.concatenate([
            jnp.concatenate([jnp.where(lo, q, 0.0).astype(BF16), bias_ones], axis=1),
            jnp.concatenate([jnp.where(lo, 0.0, q).astype(BF16), bias_ones], axis=1)], axis=0)

    def scores(q2, j, dst_ref):
        dst_ref[...] = lax.dot_general(ka_ref[j * tk:(j + 1) * tk, :], q2, (((1,), (1,)), ((), ())),
                                       preferred_element_type=F32)

    def consume(qi, j, src_ref):
        shift = slope2 * float(j * tk - qi * tq)
        s = src_ref[...]
        if j == qi:
            s = jnp.where(_iota((tk, 2 * tq), 0) <= (_iota((tk, 2 * tq), 1) & (tq - 1)), s, -jnp.inf)
        if j == 0:
            m_new = jnp.max(s, axis=0, keepdims=True) + shift
            p = jnp.exp2(s - (m_new - shift)).astype(BF16)
            acc_ref[...] = jnp.dot(vt_ref[j], p, preferred_element_type=F32)
        else:
            m_old = m_ref[...]
            m_new = jnp.maximum(m_old, jnp.max(s, axis=0, keepdims=True) + shift)
            p = jnp.exp2(s - (m_new - shift)).astype(BF16)
            acc_ref[...] = (jnp.exp2(m_old - m_new) * acc_ref[...]
                            + jnp.dot(vt_ref[j], p, preferred_element_type=F32))
        m_ref[...] = m_new

    def finalize(qi):
        acc = acc_ref[...]
        ot = acc[0:dv] / acc[dv:dv + 1]
        ot = ot[:, 0:tq] - lam * ot[:, tq:2 * tq]
        ot = ot * lax.rsqrt(jnp.mean(ot * ot, axis=0, keepdims=True) + 1e-5)
        o_ref[0, qi * tq:(qi + 1) * tq, :] = (ot.T * ng_ref[...] * (1.0 - lam_init)).astype(o_ref.dtype)

    pairs = [(qi, j) for qi in range(n_blk) for j in range(qi + 1)]
    bufs = (s0_ref, s1_ref)
    q2 = {0: q_operand(0)}
    scores(q2[0], 0, bufs[0])
    for idx, (qi, j) in enumerate(pairs):
        if idx + 1 < len(pairs):
            nqi, nj = pairs[idx + 1]
            if nqi not in q2:
                q2[nqi] = q_operand(nqi)
            scores(q2[nqi], nj, bufs[(idx + 1) % 2])
        consume(qi, j, bufs[idx % 2])
        if j == qi:
            finalize(qi)


def _diff_attention(h_da, lam_q1, lam_k1, lam_q2, lam_k2, norm_g, lam_init, *, tq=512):
    b, l, _ = h_da.shape
    tq = min(tq, l)
    tk = tq
    assert l % tq == 0 and tq & (tq - 1) == 0
    hh = DA_HEADS
    slopes = jnp.exp2(-8.0 * jnp.arange(1, hh + 1, dtype=F32) / hh)
    slopes = jnp.broadcast_to(slopes[:, None, None], (hh, SUBLANES, LANES))
    lam_p = jnp.stack([lam_q1, lam_k1, lam_q2, lam_k2]).astype(F32)
    ng = norm_g.reshape(1, 2 * DA_HEAD_DIM).astype(F32)
    seq = lambda col0: pl.BlockSpec((1, l, LANES), lambda i, h: (i, 0, col0 + h))
    return pl.pallas_call(
        functools.partial(_attn_kernel, tq=tq, tk=tk, lam_init=lam_init),
        grid=(b, hh),
        in_specs=[pl.BlockSpec((1, SUBLANES, LANES), lambda i, h: (h, 0, 0)),
                  pl.BlockSpec(lam_p.shape, lambda i, h: (0, 0)),
                  pl.BlockSpec(ng.shape, lambda i, h: (0, 0)),
                  seq(0), seq(hh), seq(2 * hh)],
        out_specs=seq(0),
        out_shape=jax.ShapeDtypeStruct((b, l, DA_WIDTH), BF16),
        scratch_shapes=[pltpu.VMEM((l, 2 * LANES), BF16),
                        pltpu.VMEM((l // tk, LANES + DA_ONES_ROWS, tk), BF16),
                        pltpu.VMEM((tk, 2 * tq), F32), pltpu.VMEM((tk, 2 * tq), F32),
                        pltpu.VMEM((1, 2 * tq), F32), pltpu.VMEM((LANES + DA_ONES_ROWS, 2 * tq), F32)],
        compiler_params=_cparams(2),
        name="diffattn",
    )(slopes, lam_p, ng, h_da, h_da, h_da)


def _softplus(x):
    return jnp.maximum(x, 0.0) + jnp.log(1.0 + jnp.exp(-jnp.abs(x)))


def _gdn_kernel(h_ref, alog_ref, dtb_ref, ng_ref, tril_ref, y_ref, s_ref, *, nb, nc):
    @pl.when(pl.program_id(1) == 0)
    def _():
        s_ref[...] = jnp.zeros_like(s_ref)

    wq = 3 * GD_WIDTH
    dh = GD_HEAD_DIM
    qkv = jnp.concatenate([h_ref[b, :, 0:wq] for b in range(nb)], axis=0)
    ab = jnp.concatenate([h_ref[b, :, wq:wq + GD_AB_PAD] for b in range(nb)], axis=0)
    gfull = -jnp.exp(alog_ref[...]) * _softplus(ab + dtb_ref[...])
    gcum = _split_rhs_dot(tril_ref[...], gfull, 3)
    glast = _rows_of_chunk(gcum, CHUNK - 1)
    beta_full = jax.nn.sigmoid(ab)

    qn = [qkv[:, h * dh:(h + 1) * dh] for h in range(GD_HEADS)]
    kn = [qkv[:, GD_WIDTH + h * dh:GD_WIDTH + (h + 1) * dh] for h in range(GD_HEADS)]
    vh = [qkv[:, 2 * GD_WIDTH + h * dh:2 * GD_WIDTH + (h + 1) * dh] for h in range(GD_HEADS)]

    r0 = _iota((PAIR, PAIR), 0)
    c0 = _iota((PAIR, PAIR), 1)
    same = (r0 >= CHUNK) == (c0 >= CHUNK)
    incl = same & (r0 >= c0)
    strict = same & (r0 > c0)

    n_pairs = GD_HEADS // 2
    units = [(b, ci, p) for b in range(nb) for ci in range(nc) for p in range(n_pairs)]

    def stack(per_head, u, lane_of=None):
        b, ci, p = u
        rr = slice((b * nc + ci) * CHUNK, (b * nc + ci + 1) * CHUNK)
        if lane_of is None:
            return jnp.concatenate([per_head[2 * p][rr], per_head[2 * p + 1][rr]], axis=0)
        return jnp.concatenate([per_head[rr, lane_of + 2 * p:lane_of + 2 * p + 1],
                                per_head[rr, lane_of + 2 * p + 1:lane_of + 2 * p + 2]], axis=0)

    q_s = [stack(qn, u) for u in units]
    k_s = [stack(kn, u) for u in units]
    v_s = [stack(vh, u) for u in units]
    g_s = [stack(gcum, u, 0) for u in units]
    gl_s = [stack(glast, u, 0) for u in units]
    beta_s = [stack(beta_full, u, GD_HEADS) for u in units]

    decay = []
    for gs in g_s:
        g_b = jnp.broadcast_to(gs, (PAIR, PAIR))
        decay.append(jnp.exp(jnp.where(incl, g_b - g_b.T, -jnp.inf)))
    kb = [x * y for x, y in zip(k_s, beta_s)]
    kk = [_mm_nt(jnp.concatenate([x, y], axis=0), z) for x, y, z in zip(kb, q_s, k_s)]
    low = [jnp.where(strict, x[0:PAIR] * dc, 0.0) for x, dc in zip(kk, decay)]
    intra = [x[PAIR:2 * PAIR] * dc for x, dc in zip(kk, decay)]
    inv = _inv_unit_lower_many(low)
    eg = [jnp.exp(gs) for gs in g_s]
    uw = [_mm(i, jnp.concatenate([v * bt, x * e], axis=1)) for i, v, bt, x, e in zip(inv, v_s, beta_s, kb, eg)]
    qg = [x * e for x, e in zip(q_s, eg)]
    kd = [x * jnp.exp(gl - gs) for x, gl, gs in zip(k_s, gl_s, g_s)]

    v_new = {}
    o_state = {}
    chains = [(b, h) for b in range(nb) for h in range(GD_HEADS)]
    state = {bh: s_ref[bh[0] * GD_HEADS + bh[1]] for bh in chains}
    for ci in range(nc):
        ws = {}
        for b, h in chains:
            p, i_h = divmod(h, 2)
            rows = slice(i_h * CHUNK, (i_h + 1) * CHUNK)
            i = units.index((b, ci, p))
            ws[(b, h)] = _mm(jnp.concatenate([uw[i][rows, dh:2 * dh], qg[i][rows]], axis=0), state[(b, h)])
        for b, h in chains:
            p, i_h = divmod(h, 2)
            rows = slice(i_h * CHUNK, (i_h + 1) * CHUNK)
            i = units.index((b, ci, p))
            vn = uw[i][rows, 0:dh] - ws[(b, h)][0:CHUNK]
            v_new[(b, ci, h)] = vn
            o_state[(b, ci, h)] = ws[(b, h)][CHUNK:PAIR]
            state[(b, h)] = state[(b, h)] * jnp.exp(gl_s[i][rows][0:1]) + _mm_tn(kd[i][rows], vn)
    for b, h in chains:
        s_ref[b * GD_HEADS + h] = state[(b, h)]

    for i, (b, ci, p) in enumerate(units):
        vn = jnp.concatenate([v_new[(b, ci, 2 * p)], v_new[(b, ci, 2 * p + 1)]], axis=0)
        o = jnp.concatenate([o_state[(b, ci, 2 * p)], o_state[(b, ci, 2 * p + 1)]], axis=0) + _mm(intra[i], vn)
        for i_h in range(2):
            h = 2 * p + i_h
            oh = o[i_h * CHUNK:(i_h + 1) * CHUNK]
            oh = oh * lax.rsqrt(jnp.mean(oh * oh, axis=-1, keepdims=True) + 1e-6) * ng_ref[...]
            zc = wq + GD_AB_PAD + h * dh
            z = h_ref[b, ci * CHUNK:(ci + 1) * CHUNK, zc:zc + dh]
            y_ref[b, ci * CHUNK:(ci + 1) * CHUNK, h * dh:(h + 1) * dh] = (
                oh * (z * jax.nn.sigmoid(z))).astype(y_ref.dtype)


def _gated_deltanet(h_gd, a_log, dt_bias, norm_g, *, nb=2 * SCAN_BATCHES, nc=SCAN_CHUNKS):
    b, l, _ = h_gd.shape
    nb = min(nb, b)
    rows_b = nc * CHUNK
    assert l % rows_b == 0 and b % nb == 0
    pad = lambda t: jnp.zeros((1, GD_AB_PAD), F32).at[0, 0:GD_HEADS].set(t.astype(F32))
    params = [pad(a_log), pad(dt_bias), norm_g.reshape(1, GD_HEAD_DIM).astype(F32), _chunk_tril(nb * nc)]
    return pl.pallas_call(
        functools.partial(_gdn_kernel, nb=nb, nc=nc),
        grid=(b // nb, l // rows_b),
        in_specs=[pl.BlockSpec((nb, rows_b, GD_IN), lambda i, j: (i, j, 0))]
        + [_const_spec(p.shape) for p in params],
        out_specs=pl.BlockSpec((nb, rows_b, GD_WIDTH), lambda i, j: (i, j, 0)),
        out_shape=jax.ShapeDtypeStruct((b, l, GD_WIDTH), BF16),
        scratch_shapes=[pltpu.VMEM((nb * GD_HEADS, GD_HEAD_DIM, GD_HEAD_DIM), F32)],
        compiler_params=_cparams(2),
        name="gdn",
    )(h_gd, *params)


def _merge_ln_kernel(x_ref, xb_ref, ya_ref, yb_ref, yc_ref, wgate_ref, wbr_ref, wout_ref, g_ref, b_ref,
                     y_ref, y16_ref):
    xb = xb_ref[...]
    merged = None
    for n, br_ref in enumerate((ya_ref, yb_ref, yc_ref)):
        gate = jax.nn.sigmoid(jnp.dot(xb, wgate_ref[n], preferred_element_type=F32))
        term = gate * jnp.dot(br_ref[...], wbr_ref[n], preferred_element_type=F32)
        merged = term if merged is None else merged + term
    mix = jnp.dot(merged.astype(BF16), wout_ref[...], preferred_element_type=F32)
    y = _layernorm(ALPHA * x_ref[...] + mix, g_ref[...], b_ref[...])
    y_ref[...] = y
    y16_ref[...] = y.astype(BF16)


def _merge_ln(x, xb, ya, yb, yc, w_gate, w_branch, w_out, g, b, *, tm=512):
    t, d = x.shape
    bw = ya.shape[-1]
    row = pl.BlockSpec((tm, d), lambda i: (i, 0))
    brow = pl.BlockSpec((tm, bw), lambda i: (i, 0))
    return pl.pallas_call(
        _merge_ln_kernel,
        grid=(t // tm,),
        in_specs=[row, row, brow, brow, brow, _const_spec((N_BRANCH, d, d)),
                  _const_spec((N_BRANCH, bw, d)), _const_spec((d, d)), _const_spec((1, d)), _const_spec((1, d))],
        out_specs=[row, row],
        out_shape=[jax.ShapeDtypeStruct((t, d), F32), jax.ShapeDtypeStruct((t, d), BF16)],
        compiler_params=_cparams(1),
        name="merge_ln",
    )(x, xb, ya, yb, yc, w_gate, w_branch, w_out, g.reshape(1, d), b.reshape(1, d))


def _split_mix_w_in(w_in):
    d = w_in.shape[0]
    o1 = RW_COLS
    o2 = o1 + DA_COLS
    o3 = o2 + GD_COLS
    gq = o2 + 3 * GD_WIDTH
    ab = jnp.zeros((d, GD_AB_PAD), w_in.dtype).at[:, 0:2 * GD_HEADS].set(w_in[:, gq:gq + 2 * GD_HEADS])
    w_cat = jnp.concatenate([w_in[:, 0:o2], w_in[:, o2:gq], ab, w_in[:, gq + 2 * GD_HEADS:o3]], axis=1)
    w_gate = w_in[:, o3:].reshape(d, N_BRANCH, d).transpose(1, 0, 2)
    return w_cat.astype(BF16), w_gate.astype(BF16)


def kernel(x, ffn1_w_in, ffn1_w_out, ln1_g, ln1_b, mix_w_in, rw_shift_mu, rw_w0, rw_w_up, rw_a0, rw_a_up, rw_g_up, rw_k_k, rw_k_a, rw_r_k, rw_ln_g, rw_ln_b, da_lam_q1, da_lam_k1, da_lam_q2, da_lam_k2, da_norm_g, gd_conv_w, gd_a_log, gd_dt_bias, gd_norm_g, mix_w_branch, mix_w_out, ln2_g, ln2_b, ffn2_w_in, ffn2_w_out, ln3_g, ln3_b):
    b, l, d = x.shape
    t = b * l
    xf = x.reshape(t, d)
    xb = None
    for i in range(DEPTH):
        xf, xb = _ffn_ln(xf, xb, ffn1_w_in[i], ffn1_w_out[i], ln1_g[i], ln1_b[i])
        w_cat, w_gate = _split_mix_w_in(mix_w_in[i])
        h_rw, h_da, h_gd = _mix_in(xb, w_cat, gd_conv_w[i], l)
        ya = _rwkv7(h_rw.reshape(b, l, RW_COLS), rw_shift_mu[i], rw_w0[i], rw_w_up[i], rw_a0[i], rw_a_up[i],
                    rw_g_up[i], rw_k_k[i], rw_k_a[i], rw_r_k[i].reshape(-1), rw_ln_g[i], rw_ln_b[i])
        lam_init = 0.8 - 0.6 * math.exp(-0.3 * i)
        yb = _diff_attention(h_da.reshape(b, l, DA_COLS), da_lam_q1[i], da_lam_k1[i], da_lam_q2[i],
                             da_lam_k2[i], da_norm_g[i], lam_init)
        yc = _gated_deltanet(h_gd.reshape(b, l, GD_IN), gd_a_log[i], gd_dt_bias[i], gd_norm_g[i])
        xf, xb = _merge_ln(xf, xb, ya.reshape(t, RW_WIDTH), yb.reshape(t, DA_WIDTH), yc.reshape(t, GD_WIDTH),
                           w_gate, mix_w_branch[i].astype(BF16), mix_w_out[i].astype(BF16), ln2_g[i], ln2_b[i])
        xf, xb = _ffn_ln(xf, xb, ffn2_w_in[i], ffn2_w_out[i], ln3_g[i], ln3_b[i])
    return xf.reshape(b, l, d)
```

```python
import functools
import math

import jax
import jax.numpy as jnp
from jax import lax
from jax.experimental import pallas as pl
from jax.experimental.pallas import tpu as pltpu

F32 = jnp.float32
BF16 = jnp.bfloat16

D_MODEL = 1024
DEPTH = 2
D_FF = 2816
RW_HEADS = 8
RW_HEAD_DIM = 64
RW_WIDTH = 512
RW_DECAY_LORA = 64
RW_ICLR_LORA = 64
RW_GATE_LORA = 128
RW_COLS = 3 * RW_WIDTH + RW_DECAY_LORA + RW_ICLR_LORA + RW_GATE_LORA
RW_GN_EPS = 64e-5
DA_HEADS = 4
DA_HEAD_DIM = 64
DA_WIDTH = 512
DA_COLS = 3 * DA_WIDTH
GD_HEADS = 4
GD_HEAD_DIM = 128
GD_WIDTH = 512
GD_CONV = 4
GD_COLS = 3 * GD_WIDTH + 2 * GD_HEADS + GD_WIDTH
N_BRANCH = 3
ALPHA = (2.0 * DEPTH) ** 0.25

LANES = 128
SUBLANES = 8
CHUNK = 64
PAIR = 2 * CHUNK
DA_QUERY_GROUP = 256
DA_ONES_ROWS = 16
LOG2E = math.log2(math.e)
GD_AB_PAD = LANES
GD_IN = 3 * GD_WIDTH + GD_AB_PAD + GD_WIDTH
VMEM_LIMIT = 56 * 1024 * 1024
SCAN_BATCHES = 2
SCAN_CHUNKS = 2


def _cparams(n_grid):
    return pltpu.CompilerParams(dimension_semantics=("arbitrary",) * n_grid,
                                vmem_limit_bytes=VMEM_LIMIT)


def _const_spec(shape):
    nd = len(shape)
    return pl.BlockSpec(shape, lambda *_: (0,) * nd, pipeline_mode=pl.Buffered(1))


def _mm(a, b):
    return jnp.dot(a.astype(BF16), b.astype(BF16), preferred_element_type=F32)


def _mm_nt(a, b):
    return lax.dot_general(a.astype(BF16), b.astype(BF16), (((1,), (1,)), ((), ())),
                           preferred_element_type=F32)


def _mm_tn(a, b):
    return lax.dot_general(a.astype(BF16), b.astype(BF16), (((0,), (0,)), ((), ())),
                           preferred_element_type=F32)


def _split_lhs_dot(x, exact_rhs, terms):
    acc = None
    rem = x
    for _ in range(terms):
        hi = rem.astype(BF16)
        part = jnp.dot(hi, exact_rhs, preferred_element_type=F32)
        acc = part if acc is None else acc + part
        rem = rem - hi.astype(F32)
    return acc


def _split_rhs_dot(exact_lhs, x, terms):
    acc = None
    rem = x
    for _ in range(terms):
        hi = rem.astype(BF16)
        part = jnp.dot(exact_lhs, hi, preferred_element_type=F32)
        acc = part if acc is None else acc + part
        rem = rem - hi.astype(F32)
    return acc


def _layernorm(z, g, b, eps=1e-5):
    mu = jnp.mean(z, axis=-1, keepdims=True)
    zc = z - mu
    var = jnp.mean(zc * zc, axis=-1, keepdims=True)
    return zc * lax.rsqrt(var + eps) * g + b


def _iota(shape, dim):
    return lax.broadcasted_iota(jnp.int32, shape, dim)


def _inv_unit_lower_many(lows):
    eye = (_iota((PAIR, PAIR), 0) == _iota((PAIR, PAIR), 1)).astype(F32)
    pw = [-low for low in lows]
    inv = [eye + n for n in pw]
    for _ in range(int(math.log2(CHUNK)) - 1):
        pw = [_mm(p, p) for p in pw]
        inv = [i + _mm(i, p) for i, p in zip(inv, pw)]
    return inv


def _stack_heads(x, lo_mask):
    return jnp.concatenate([jnp.where(lo_mask, x, 0.0), jnp.where(lo_mask, 0.0, x)], axis=0)


def _chunk_tril(n_chunks):
    r = jnp.arange(n_chunks * CHUNK)
    return ((r[:, None] >= r[None, :]) & (r[:, None] // CHUNK == r[None, :] // CHUNK)).astype(BF16)


def _rows_of_chunk(x, row):
    n = x.shape[0] // CHUNK
    parts = [jnp.broadcast_to(x[i * CHUNK + row:i * CHUNK + row + 1], (CHUNK,) + x.shape[1:]) for i in range(n)]
    return jnp.concatenate(parts, axis=0) if n > 1 else parts[0]


def _ffn_ln_kernel(x_ref, *refs, tf):
    if len(refs) == 8:
        xb_ref, win_ref, wo_ref, g_ref, b_ref, y_ref, yb_ref, acc_ref = refs
        xb = xb_ref[...]
    else:
        win_ref, wo_ref, g_ref, b_ref, y_ref, yb_ref, acc_ref = refs
        xb = x_ref[...].astype(BF16)
    dff = wo_ref.shape[0]
    nf = dff // tf

    def gate_up(f):
        cols = slice(f * tf, (f + 1) * tf)
        return (jnp.dot(xb, win_ref[:, cols], preferred_element_type=F32),
                jnp.dot(xb, win_ref[:, dff + f * tf:dff + (f + 1) * tf], preferred_element_type=F32))

    nxt = gate_up(0)
    for f in range(nf):
        gate, up = nxt
        if f + 1 < nf:
            nxt = gate_up(f + 1)
        act = (gate * jax.nn.sigmoid(gate) * up).astype(BF16)
        part = jnp.dot(act, wo_ref[f * tf:(f + 1) * tf, :], preferred_element_type=F32)
        if f == 0:
            acc_ref[...] = part
        else:
            acc_ref[...] += part
    y = _layernorm(ALPHA * x_ref[...] + 0.5 * acc_ref[...], g_ref[...], b_ref[...])
    y_ref[...] = y
    yb_ref[...] = y.astype(BF16)


def _ffn_ln(x, xb, w_in, w_out, g, b, *, tm=1024, tf=256):
    t, d = x.shape
    dff = w_out.shape[0]
    assert dff % tf == 0 and t % tm == 0
    row = pl.BlockSpec((tm, d), lambda i: (i, 0))
    acts = [x] if xb is None else [x, xb]
    return pl.pallas_call(
        functools.partial(_ffn_ln_kernel, tf=tf),
        grid=(t // tm,),
        in_specs=[row] * len(acts) + [_const_spec((d, 2 * dff)), _const_spec((dff, d)), _const_spec((1, d)),
                                      _const_spec((1, d))],
        out_specs=[row, row],
        out_shape=[jax.ShapeDtypeStruct((t, d), F32), jax.ShapeDtypeStruct((t, d), BF16)],
        scratch_shapes=[pltpu.VMEM((tm, d), F32)],
        compiler_params=_cparams(1),
        name="ffn_ln",
    )(*acts, w_in.astype(BF16), w_out.astype(BF16), g.reshape(1, d), b.reshape(1, d))


def _mix_in_kernel(xb_ref, w_ref, cw_ref, rw_ref, da_ref, gd_ref, stage_ref, *, tn, tiles_per_seq):
    dh = GD_HEAD_DIM

    @pl.when(pl.program_id(0) % tiles_per_seq == 0)
    def _():
        stage_ref[:, 0:SUBLANES, :] = jnp.zeros((stage_ref.shape[0], SUBLANES, dh), F32)

    xb = xb_ref[...]
    tm = xb.shape[0]

    def conv_act(res, h0):
        c = h0 // dh
        stage_ref[c, SUBLANES:SUBLANES + tm, :] = res
        cw = cw_ref[:, h0:h0 + dh]
        conv = res * cw[GD_CONV - 1:GD_CONV]
        for j in range(GD_CONV - 1):
            r0 = SUBLANES - (GD_CONV - 1 - j)
            conv = conv + stage_ref[c, r0:r0 + tm, :] * cw[j:j + 1]
        stage_ref[c, 0:SUBLANES, :] = res[tm - SUBLANES:tm]
        act = conv * jax.nn.sigmoid(conv)
        if h0 < 2 * GD_WIDTH:
            act = act * lax.rsqrt(jnp.sum(act * act, axis=-1, keepdims=True) + 1e-6)
            if h0 < GD_WIDTH:
                act = act * dh ** -0.5
        return act

    off = 0
    for out_ref in (rw_ref, da_ref, gd_ref):
        width = out_ref.shape[-1]
        for c0 in range(0, width, tn):
            c1 = min(c0 + tn, width)
            res = jnp.dot(xb, w_ref[:, off + c0:off + c1], preferred_element_type=F32)
            if out_ref is gd_ref and c0 < 3 * GD_WIDTH:
                for h0 in range(c0, c1, dh):
                    out_ref[:, h0:h0 + dh] = conv_act(res[:, h0 - c0:h0 - c0 + dh], h0)
            else:
                out_ref[:, c0:c1] = res.astype(out_ref.dtype)
        off += width


def _mix_in(xb, w_cat, conv_w, seq_len, *, tm=512, tn=256):
    t, d = xb.shape
    n = w_cat.shape[1]
    assert n == RW_COLS + DA_COLS + GD_IN and t % tm == 0 and seq_len % tm == 0 and tn % GD_HEAD_DIM == 0
    return pl.pallas_call(
        functools.partial(_mix_in_kernel, tn=tn, tiles_per_seq=seq_len // tm),
        grid=(t // tm,),
        in_specs=[pl.BlockSpec((tm, d), lambda i: (i, 0)), _const_spec((d, n)), _const_spec(conv_w.shape)],
        out_specs=[pl.BlockSpec((tm, RW_COLS), lambda i: (i, 0)),
                   pl.BlockSpec((tm, DA_COLS), lambda i: (i, 0)),
                   pl.BlockSpec((tm, GD_IN), lambda i: (i, 0))],
        out_shape=[jax.ShapeDtypeStruct((t, RW_COLS), F32),
                   jax.ShapeDtypeStruct((t, DA_COLS), BF16),
                   jax.ShapeDtypeStruct((t, GD_IN), F32)],
        scratch_shapes=[pltpu.VMEM((3 * GD_HEADS, SUBLANES + tm, GD_HEAD_DIM), F32)],
        compiler_params=_cparams(1),
        name="mix_in",
    )(xb, w_cat, conv_w.astype(F32))


def _rwkv_kernel(h_ref, mu_ref, w0_ref, a0_ref, kk_ref, ka_ref, rk_ref, lng_ref, lnb_ref,
                 lora_ref, gup_ref, bd_ref, tril_ref, y_ref, prev_ref, s_ref, *, nb, nc):
    rows_b = nc * CHUNK

    @pl.when(pl.program_id(1) == 0)
    def _():
        prev_ref[:, 0:SUBLANES, :] = jnp.zeros((nb, SUBLANES, RW_COLS), F32)
        s_ref[...] = jnp.zeros_like(s_ref)

    n_pairs = RW_HEADS // 2
    parts = []
    for b in range(nb):
        h = h_ref[b]
        prev_ref[b, SUBLANES:SUBLANES + rows_b, :] = h
        hprev = prev_ref[b, SUBLANES - 1:SUBLANES - 1 + rows_b, :]
        prev_ref[b, 0:SUBLANES, :] = h[rows_b - SUBLANES:rows_b]
        parts.append(h + mu_ref[...] * (hprev - h))
    hs = jnp.concatenate(parts, axis=0) if nb > 1 else parts[0]
    n_rows = nb * rows_b

    w = RW_WIDTH
    r = hs[:, 0:w]
    k = hs[:, w:2 * w]
    v = hs[:, 2 * w:3 * w]
    wa = hs[:, 3 * w:3 * w + LANES]
    gd = hs[:, 3 * w + LANES:3 * w + 2 * LANES]
    lo_n = _iota((n_rows, LANES), 1) < RW_HEAD_DIM
    lora = _mm(jnp.where(lo_n, jnp.tanh(wa), wa), lora_ref[...])
    logw = -math.exp(-0.5) * jax.nn.sigmoid(w0_ref[...] + lora[:, 0:w])
    a = jax.nn.sigmoid(a0_ref[...] + lora[:, w:2 * w])
    g = _mm(jax.nn.sigmoid(gd), gup_ref[...])

    bd = bd_ref[...]
    kkr = k * kk_ref[...]
    kk = kkr * lax.rsqrt(_split_lhs_dot(kkr * kkr, bd, 1) + 1e-6)
    k2 = k * (1.0 + (a - 1.0) * ka_ref[...])
    bvec = kk * a
    bonus = _split_lhs_dot(r * k2 * rk_ref[...], bd, 2) * v

    e_in = _split_rhs_dot(tril_ref[...], logw, 2)
    rho = _rows_of_chunk(e_in, CHUNK // 2 - 1)
    r_t = r * jnp.exp(e_in - rho)
    kk_t = kk * jnp.exp(e_in - logw - rho)
    e_neg = jnp.exp(rho - e_in)
    k_h = k2 * e_neg
    b_h = bvec * e_neg

    lane = _iota((CHUNK, LANES), 1)
    lo = lane < RW_HEAD_DIM
    tt = _iota((CHUNK, LANES), 0)
    ss = lane & (CHUNK - 1)
    strict = tt > ss
    incl = tt >= ss

    units = [(b, ci, p) for b in range(nb) for ci in range(nc) for p in range(n_pairs)]

    def tile(x, u):
        b, ci, p = u
        r0 = (b * nc + ci) * CHUNK
        return x[r0:r0 + CHUNK, p * LANES:(p + 1) * LANES]

    kks = [_stack_heads(tile(kk_t, u), lo) for u in units]
    rs = [_stack_heads(tile(r_t, u), lo) for u in units]
    vs = [_stack_heads(tile(v, u), lo) for u in units]
    khs = [_stack_heads(tile(k_h, u), lo) for u in units]
    bhs = [_stack_heads(tile(b_h, u), lo) for u in units]
    z = [_mm_nt(jnp.concatenate([tile(kk_t, u), tile(r_t, u)], axis=0), jnp.concatenate([kh, bh], axis=0))
         for u, kh, bh in zip(units, khs, bhs)]
    ak = [_stack_heads(jnp.where(strict, zz[0:CHUNK, 0:PAIR], 0.0), lo) for zz in z]
    ab = [_stack_heads(jnp.where(strict, zz[0:CHUNK, PAIR:2 * PAIR], 0.0), lo) for zz in z]
    bk = [_stack_heads(jnp.where(incl, zz[CHUNK:PAIR, 0:PAIR], 0.0), lo) for zz in z]
    bb = [_stack_heads(jnp.where(incl, zz[CHUNK:PAIR, PAIR:2 * PAIR], 0.0), lo) for zz in z]
    inv = _inv_unit_lower_many(ab)
    akv = [_mm(x, y) for x, y in zip(ak, vs)]
    bkv = [_mm(x, y) for x, y in zip(bk, vs)]
    pq = [_mm(i, jnp.concatenate([x, y], axis=1)) for i, x, y in zip(inv, kks, akv)]
    bpq = [_mm(x, y) for x, y in zip(bb, pq)]
    rp = [x - y[:, 0:PAIR] for x, y in zip(rs, bpq)]
    y0 = [x - y[:, PAIR:2 * PAIR] for x, y in zip(bkv, bpq)]
    ptb = [_mm_tn(x[:, 0:PAIR], y) for x, y in zip(pq, bhs)]
    hm = [_mm_tn(jnp.concatenate([x, y[:, PAIR:2 * PAIR]], axis=0), jnp.concatenate([kh, -bh], axis=0))
          for x, y, kh, bh in zip(vs, pq, khs, bhs)]

    s_scale = jnp.exp(rho)
    d_out = jnp.exp(_rows_of_chunk(e_in, CHUNK - 1) - rho)
    yo = {}
    chains = [(b, p) for b in range(nb) for p in range(n_pairs)]
    state = {bp: s_ref[bp[0] * n_pairs + bp[1]] for bp in chains}
    for ci in range(nc):
        for b, p in chains:
            i = units.index((b, ci, p))
            r0 = (b * nc + ci) * CHUNK
            sl = slice(p * LANES, (p + 1) * LANES)
            sp = state[(b, p)] * s_scale[r0:r0 + 1, sl]
            yo[(b, ci, p)] = _mm_nt(rp[i], sp) + y0[i]
            state[(b, p)] = (sp - _mm(sp, ptb[i]) + hm[i]) * d_out[r0:r0 + 1, sl]
    for b, p in chains:
        s_ref[b * n_pairs + p] = state[(b, p)]
    y = jnp.concatenate(
        [jnp.concatenate([yo[(b, ci, p)][0:CHUNK] + yo[(b, ci, p)][CHUNK:PAIR] for p in range(n_pairs)], axis=1)
         for b in range(nb) for ci in range(nc)], axis=0)

    inv_n = 1.0 / RW_HEAD_DIM
    ym = _split_lhs_dot(y, bd, 1) * inv_n
    yc = y - ym
    yv = _split_lhs_dot(yc * yc, bd, 1) * inv_n
    yn = yc * lax.rsqrt(yv + RW_GN_EPS) * lng_ref[...] + lnb_ref[...]
    out = ((yn + bonus) * g).astype(y_ref.dtype)
    for b in range(nb):
        y_ref[b] = out[b * rows_b:(b + 1) * rows_b]


def _rwkv7(h_rw, mu, w0, w_up, a0, a_up, g_up, k_k, k_a, r_k, ln_g, ln_b, *, nb=SCAN_BATCHES, nc=SCAN_CHUNKS):
    b, l, _ = h_rw.shape
    w = RW_WIDTH
    nb = min(nb, b)
    rows_b = nc * CHUNK
    assert l % rows_b == 0 and b % nb == 0
    lora = jnp.zeros((LANES, 2 * w), F32)
    lora = lora.at[0:RW_DECAY_LORA, 0:w].set(w_up).at[RW_DECAY_LORA:LANES, w:2 * w].set(a_up)
    hid = jnp.arange(w) // RW_HEAD_DIM
    bd = (hid[:, None] == hid[None, :]).astype(BF16)
    vec = lambda t: t.reshape(1, -1).astype(F32)
    params = [vec(mu), vec(w0), vec(a0), vec(k_k), vec(k_a), vec(r_k), vec(ln_g), vec(ln_b),
              lora.astype(BF16), g_up.astype(BF16), bd, _chunk_tril(nb * nc)]
    return pl.pallas_call(
        functools.partial(_rwkv_kernel, nb=nb, nc=nc),
        grid=(b // nb, l // rows_b),
        in_specs=[pl.BlockSpec((nb, rows_b, RW_COLS), lambda i, j: (i, j, 0))]
        + [_const_spec(p.shape) for p in params],
        out_specs=pl.BlockSpec((nb, rows_b, w), lambda i, j: (i, j, 0)),
        out_shape=jax.ShapeDtypeStruct((b, l, w), BF16),
        scratch_shapes=[pltpu.VMEM((nb, SUBLANES + rows_b, RW_COLS), F32),
                        pltpu.VMEM((nb * RW_HEADS // 2, PAIR, PAIR), F32)],
        compiler_params=_cparams(2),
        name="rwkv7",
    )(h_rw, *params)


def _attn_kernel(slope_ref, lam_ref, ng_ref, q_ref, k_ref, v_ref, o_ref, ka_ref, vt_ref, s0_ref, s1_ref, m_ref,
                 acc_ref, *, tq, tk, lam_init):
    d = DA_HEAD_DIM
    dv = 2 * d
    n_blk = vt_ref.shape[0]
    slope2 = slope_ref[0, 0:1, 0:1] * LOG2E

    lane_k = _iota((tk, LANES), 1)
    bias = slope2 * _iota((tk, LANES), 0).astype(F32)
    hi = bias.astype(BF16).astype(F32)
    mid = (bias - hi).astype(BF16).astype(F32)
    low = bias - hi - mid
    pieces = jnp.where(lane_k == 0, hi, jnp.where(lane_k == 1, mid, jnp.where(lane_k == 2, low, 0.0))).astype(BF16)
    ones_row = (_iota((DA_ONES_ROWS, tk), 0) == 0).astype(BF16)
    for jb in range(n_blk):
        rows = slice(jb * tk, (jb + 1) * tk)
        ka_ref[rows, 0:LANES] = k_ref[0, rows, :]
        ka_ref[rows, LANES:2 * LANES] = pieces
        vt_ref[jb, 0:dv, :] = v_ref[0, rows, :].astype(F32).T.astype(BF16)
        vt_ref[jb, dv:dv + DA_ONES_ROWS, :] = ones_row

    lp = lam_ref[...]
    lam = (jnp.exp(jnp.sum(lp[0:1] * lp[1:2], axis=-1, keepdims=True))
           - jnp.exp(jnp.sum(lp[2:3] * lp[3:4], axis=-1, keepdims=True)) + lam_init)
    lane_q = _iota((tq, LANES), 1)
    lo = lane_q < d
    bias_ones = (lane_q < 3).astype(BF16)

    def q_operand(qi):
        q = q_ref[0, qi * tq:(qi + 1) * tq, :].astype(F32) * (d ** -0.5 * LOG2E)
        return jnp.concatenate([
            jnp.concatenate([jnp.where(lo, q, 0.0).astype(BF16), bias_ones], axis=1),
            jnp.concatenate([jnp.where(lo, 0.0, q).astype(BF16), bias_ones], axis=1)], axis=0)

    def scores(q2, j, dst_ref):
        dst_ref[...] = lax.dot_general(ka_ref[j * tk:(j + 1) * tk, :], q2, (((1,), (1,)), ((), ())),
                                       preferred_element_type=F32)

    def consume(qi, j, src_ref):
        shift = slope2 * float(j * tk - qi * tq)
        for c0 in range(0, 2 * tq, DA_QUERY_GROUP):
            cols = slice(c0, c0 + DA_QUERY_GROUP)
            s = src_ref[:, cols]
            if j == qi:
                key = _iota((tk, DA_QUERY_GROUP), 0)
                query = (_iota((tk, DA_QUERY_GROUP), 1) + c0) & (tq - 1)
                s = jnp.where(key <= query, s, -jnp.inf)
            if j == 0:
                m_new = jnp.max(s, axis=0, keepdims=True) + shift
                p = jnp.exp2(s - (m_new - shift)).astype(BF16)
                acc_ref[:, cols] = jnp.dot(vt_ref[j], p, preferred_element_type=F32)
            else:
                m_old = m_ref[:, cols]
                m_new = jnp.maximum(m_old, jnp.max(s, axis=0, keepdims=True) + shift)
                p = jnp.exp2(s - (m_new - shift)).astype(BF16)
                acc_ref[:, cols] = (jnp.exp2(m_old - m_new) * acc_ref[:, cols]
                                    + jnp.dot(vt_ref[j], p, preferred_element_type=F32))
            m_ref[:, cols] = m_new

    def finalize(qi):
        acc = acc_ref[...]
        ot = acc[0:dv] / acc[dv:dv + 1]
        ot = ot[:, 0:tq] - lam * ot[:, tq:2 * tq]
        ot = ot * lax.rsqrt(jnp.mean(ot * ot, axis=0, keepdims=True) + 1e-5)
        o_ref[0, qi * tq:(qi + 1) * tq, :] = (ot.T * ng_ref[...] * (1.0 - lam_init)).astype(o_ref.dtype)

    pairs = [(qi, j) for qi in range(n_blk) for j in range(qi + 1)]
    bufs = (s0_ref, s1_ref)
    q2 = {0: q_operand(0)}
    scores(q2[0], 0, bufs[0])
    for idx, (qi, j) in enumerate(pairs):
        if idx + 1 < len(pairs):
            nqi, nj = pairs[idx + 1]
            if nqi not in q2:
                q2[nqi] = q_operand(nqi)
            scores(q2[nqi], nj, bufs[(idx + 1) % 2])
        consume(qi, j, bufs[idx % 2])
        if j == qi:
            finalize(qi)


def _diff_attention(h_da, lam_q1, lam_k1, lam_q2, lam_k2, norm_g, lam_init, *, tq=512):
    b, l, _ = h_da.shape
    tq = min(tq, l)
    tk = tq
    assert l % tq == 0 and tq & (tq - 1) == 0
    hh = DA_HEADS
    slopes = jnp.exp2(-8.0 * jnp.arange(1, hh + 1, dtype=F32) / hh)
    slopes = jnp.broadcast_to(slopes[:, None, None], (hh, SUBLANES, LANES))
    lam_p = jnp.stack([lam_q1, lam_k1, lam_q2, lam_k2]).astype(F32)
    ng = norm_g.reshape(1, 2 * DA_HEAD_DIM).astype(F32)
    seq = lambda col0: pl.BlockSpec((1, l, LANES), lambda i, h: (i, 0, col0 + h))
    return pl.pallas_call(
        functools.partial(_attn_kernel, tq=tq, tk=tk, lam_init=lam_init),
        grid=(b, hh),
        in_specs=[pl.BlockSpec((1, SUBLANES, LANES), lambda i, h: (h, 0, 0)),
                  pl.BlockSpec(lam_p.shape, lambda i, h: (0, 0)),
                  pl.BlockSpec(ng.shape, lambda i, h: (0, 0)),
                  seq(0), seq(hh), seq(2 * hh)],
        out_specs=seq(0),
        out_shape=jax.ShapeDtypeStruct((b, l, DA_WIDTH), BF16),
        scratch_shapes=[pltpu.VMEM((l, 2 * LANES), BF16),
                        pltpu.VMEM((l // tk, LANES + DA_ONES_ROWS, tk), BF16),
                        pltpu.VMEM((tk, 2 * tq), F32), pltpu.VMEM((tk, 2 * tq), F32),
                        pltpu.VMEM((1, 2 * tq), F32), pltpu.VMEM((LANES + DA_ONES_ROWS, 2 * tq), F32)],
        compiler_params=_cparams(2),
        name="diffattn",
    )(slopes, lam_p, ng, h_da, h_da, h_da)


def _softplus(x):
    return jnp.maximum(x, 0.0) + jnp.log(1.0 + jnp.exp(-jnp.abs(x)))


def _gdn_kernel(h_ref, alog_ref, dtb_ref, ng_ref, tril_ref, y_ref, s_ref, *, nb, nc):
    @pl.when(pl.program_id(1) == 0)
    def _():
        s_ref[...] = jnp.zeros_like(s_ref)

    wq = 3 * GD_WIDTH
    dh = GD_HEAD_DIM
    qkv = jnp.concatenate([h_ref[b, :, 0:wq] for b in range(nb)], axis=0)
    ab = jnp.concatenate([h_ref[b, :, wq:wq + GD_AB_PAD] for b in range(nb)], axis=0)
    gfull = -jnp.exp(alog_ref[...]) * _softplus(ab + dtb_ref[...])
    gcum = _split_rhs_dot(tril_ref[...], gfull, 3)
    glast = _rows_of_chunk(gcum, CHUNK - 1)
    beta_full = jax.nn.sigmoid(ab)

    qn = [qkv[:, h * dh:(h + 1) * dh] for h in range(GD_HEADS)]
    kn = [qkv[:, GD_WIDTH + h * dh:GD_WIDTH + (h + 1) * dh] for h in range(GD_HEADS)]
    vh = [qkv[:, 2 * GD_WIDTH + h * dh:2 * GD_WIDTH + (h + 1) * dh] for h in range(GD_HEADS)]

    r0 = _iota((PAIR, PAIR), 0)
    c0 = _iota((PAIR, PAIR), 1)
    same = (r0 >= CHUNK) == (c0 >= CHUNK)
    incl = same & (r0 >= c0)
    strict = same & (r0 > c0)

    n_pairs = GD_HEADS // 2
    units = [(b, ci, p) for b in range(nb) for ci in range(nc) for p in range(n_pairs)]

    def stack(per_head, u, lane_of=None):
        b, ci, p = u
        rr = slice((b * nc + ci) * CHUNK, (b * nc + ci + 1) * CHUNK)
        if lane_of is None:
            return jnp.concatenate([per_head[2 * p][rr], per_head[2 * p + 1][rr]], axis=0)
        return jnp.concatenate([per_head[rr, lane_of + 2 * p:lane_of + 2 * p + 1],
                                per_head[rr, lane_of + 2 * p + 1:lane_of + 2 * p + 2]], axis=0)

    q_s = [stack(qn, u) for u in units]
    k_s = [stack(kn, u) for u in units]
    v_s = [stack(vh, u) for u in units]
    g_s = [stack(gcum, u, 0) for u in units]
    gl_s = [stack(glast, u, 0) for u in units]
    beta_s = [stack(beta_full, u, GD_HEADS) for u in units]

    decay = []
    for gs in g_s:
        g_b = jnp.broadcast_to(gs, (PAIR, PAIR))
        decay.append(jnp.exp(jnp.where(incl, g_b - g_b.T, -jnp.inf)))
    kb = [x * y for x, y in zip(k_s, beta_s)]
    kk = [_mm_nt(jnp.concatenate([x, y], axis=0), z) for x, y, z in zip(kb, q_s, k_s)]
    low = [jnp.where(strict, x[0:PAIR] * dc, 0.0) for x, dc in zip(kk, decay)]
    intra = [x[PAIR:2 * PAIR] * dc for x, dc in zip(kk, decay)]
    inv = _inv_unit_lower_many(low)
    eg = [jnp.exp(gs) for gs in g_s]
    uw = [_mm(i, jnp.concatenate([v * bt, x * e], axis=1)) for i, v, bt, x, e in zip(inv, v_s, beta_s, kb, eg)]
    qg = [x * e for x, e in zip(q_s, eg)]
    kd = [x * jnp.exp(gl - gs) for x, gl, gs in zip(k_s, gl_s, g_s)]

    v_new = {}
    o_state = {}
    chains = [(b, h) for b in range(nb) for h in range(GD_HEADS)]
    state = {bh: s_ref[bh[0] * GD_HEADS + bh[1]] for bh in chains}
    for ci in range(nc):
        ws = {}
        for b, h in chains:
            p, i_h = divmod(h, 2)
            rows = slice(i_h * CHUNK, (i_h + 1) * CHUNK)
            i = units.index((b, ci, p))
            ws[(b, h)] = _mm(jnp.concatenate([uw[i][rows, dh:2 * dh], qg[i][rows]], axis=0), state[(b, h)])
        for b, h in chains:
            p, i_h = divmod(h, 2)
            rows = slice(i_h * CHUNK, (i_h + 1) * CHUNK)
            i = units.index((b, ci, p))
            vn = uw[i][rows, 0:dh] - ws[(b, h)][0:CHUNK]
            v_new[(b, ci, h)] = vn
            o_state[(b, ci, h)] = ws[(b, h)][CHUNK:PAIR]
            state[(b, h)] = state[(b, h)] * jnp.exp(gl_s[i][rows][0:1]) + _mm_tn(kd[i][rows], vn)
    for b, h in chains:
        s_ref[b * GD_HEADS + h] = state[(b, h)]

    for i, (b, ci, p) in enumerate(units):
        vn = jnp.concatenate([v_new[(b, ci, 2 * p)], v_new[(b, ci, 2 * p + 1)]], axis=0)
        o = jnp.concatenate([o_state[(b, ci, 2 * p)], o_state[(b, ci, 2 * p + 1)]], axis=0) + _mm(intra[i], vn)
        for i_h in range(2):
            h = 2 * p + i_h
            oh = o[i_h * CHUNK:(i_h + 1) * CHUNK]
            oh = oh * lax.rsqrt(jnp.mean(oh * oh, axis=-1, keepdims=True) + 1e-6) * ng_ref[...]
            zc = wq + GD_AB_PAD + h * dh
            z = h_ref[b, ci * CHUNK:(ci + 1) * CHUNK, zc:zc + dh]
            y_ref[b, ci * CHUNK:(ci + 1) * CHUNK, h * dh:(h + 1) * dh] = (
                oh * (z * jax.nn.sigmoid(z))).astype(y_ref.dtype)


def _gated_deltanet(h_gd, a_log, dt_bias, norm_g, *, nb=2 * SCAN_BATCHES, nc=SCAN_CHUNKS):
    b, l, _ = h_gd.shape
    nb = min(nb, b)
    rows_b = nc * CHUNK
    assert l % rows_b == 0 and b % nb == 0
    pad = lambda t: jnp.zeros((1, GD_AB_PAD), F32).at[0, 0:GD_HEADS].set(t.astype(F32))
    params = [pad(a_log), pad(dt_bias), norm_g.reshape(1, GD_HEAD_DIM).astype(F32), _chunk_tril(nb * nc)]
    return pl.pallas_call(
        functools.partial(_gdn_kernel, nb=nb, nc=nc),
        grid=(b // nb, l // rows_b),
        in_specs=[pl.BlockSpec((nb, rows_b, GD_IN), lambda i, j: (i, j, 0))]
        + [_const_spec(p.shape) for p in params],
        out_specs=pl.BlockSpec((nb, rows_b, GD_WIDTH), lambda i, j: (i, j, 0)),
        out_shape=jax.ShapeDtypeStruct((b, l, GD_WIDTH), BF16),
        scratch_shapes=[pltpu.VMEM((nb * GD_HEADS, GD_HEAD_DIM, GD_HEAD_DIM), F32)],
        compiler_params=_cparams(2),
        name="gdn",
    )(h_gd, *params)


def _merge_ln_kernel(x_ref, xb_ref, ya_ref, yb_ref, yc_ref, wgate_ref, wbr_ref, wout_ref, g_ref, b_ref,
                     y_ref, y16_ref):
    xb = xb_ref[...]
    merged = None
    for n, br_ref in enumerate((ya_ref, yb_ref, yc_ref)):
        gate = jax.nn.sigmoid(jnp.dot(xb, wgate_ref[n], preferred_element_type=F32))
        term = gate * jnp.dot(br_ref[...], wbr_ref[n], preferred_element_type=F32)
        merged = term if merged is None else merged + term
    mix = jnp.dot(merged.astype(BF16), wout_ref[...], preferred_element_type=F32)
    y = _layernorm(ALPHA * x_ref[...] + mix, g_ref[...], b_ref[...])
    y_ref[...] = y
    y16_ref[...] = y.astype(BF16)


def _merge_ln(x, xb, ya, yb, yc, w_gate, w_branch, w_out, g, b, *, tm=512):
    t, d = x.shape
    bw = ya.shape[-1]
    row = pl.BlockSpec((tm, d), lambda i: (i, 0))
    brow = pl.BlockSpec((tm, bw), lambda i: (i, 0))
    return pl.pallas_call(
        _merge_ln_kernel,
        grid=(t // tm,),
        in_specs=[row, row, brow, brow, brow, _const_spec((N_BRANCH, d, d)),
                  _const_spec((N_BRANCH, bw, d)), _const_spec((d, d)), _const_spec((1, d)), _const_spec((1, d))],
        out_specs=[row, row],
        out_shape=[jax.ShapeDtypeStruct((t, d), F32), jax.ShapeDtypeStruct((t, d), BF16)],
        compiler_params=_cparams(1),
        name="merge_ln",
    )(x, xb, ya, yb, yc, w_gate, w_branch, w_out, g.reshape(1, d), b.reshape(1, d))


def _split_mix_w_in(w_in):
    d = w_in.shape[0]
    o1 = RW_COLS
    o2 = o1 + DA_COLS
    o3 = o2 + GD_COLS
    gq = o2 + 3 * GD_WIDTH
    ab = jnp.zeros((d, GD_AB_PAD), w_in.dtype).at[:, 0:2 * GD_HEADS].set(w_in[:, gq:gq + 2 * GD_HEADS])
    w_cat = jnp.concatenate([w_in[:, 0:o2], w_in[:, o2:gq], ab, w_in[:, gq + 2 * GD_HEADS:o3]], axis=1)
    w_gate = w_in[:, o3:].reshape(d, N_BRANCH, d).transpose(1, 0, 2)
    return w_cat.astype(BF16), w_gate.astype(BF16)


def kernel(x, ffn1_w_in, ffn1_w_out, ln1_g, ln1_b, mix_w_in, rw_shift_mu, rw_w0, rw_w_up, rw_a0, rw_a_up, rw_g_up, rw_k_k, rw_k_a, rw_r_k, rw_ln_g, rw_ln_b, da_lam_q1, da_lam_k1, da_lam_q2, da_lam_k2, da_norm_g, gd_conv_w, gd_a_log, gd_dt_bias, gd_norm_g, mix_w_branch, mix_w_out, ln2_g, ln2_b, ffn2_w_in, ffn2_w_out, ln3_g, ln3_b):
    b, l, d = x.shape
    t = b * l
    xf = x.reshape(t, d)
    xb = None
    for i in range(DEPTH):
        xf, xb = _ffn_ln(xf, xb, ffn1_w_in[i], ffn1_w_out[i], ln1_g[i], ln1_b[i])
        w_cat, w_gate = _split_mix_w_in(mix_w_in[i])
        h_rw, h_da, h_gd = _mix_in(xb, w_cat, gd_conv_w[i], l)
        ya = _rwkv7(h_rw.reshape(b, l, RW_COLS), rw_shift_mu[i], rw_w0[i], rw_w_up[i], rw_a0[i], rw_a_up[i],
                    rw_g_up[i], rw_k_k[i], rw_k_a[i], rw_r_k[i].reshape(-1), rw_ln_g[i], rw_ln_b[i])
        lam_init = 0.8 - 0.6 * math.exp(-0.3 * i)
        yb = _diff_attention(h_da.reshape(b, l, DA_COLS), da_lam_q1[i], da_lam_k1[i], da_lam_q2[i],
                             da_lam_k2[i], da_norm_g[i], lam_init)
        yc = _gated_deltanet(h_gd.reshape(b, l, GD_IN), gd_a_log[i], gd_dt_bias[i], gd_norm_g[i])
        xf, xb = _merge_ln(xf, xb, ya.reshape(t, RW_WIDTH), yb.reshape(t, DA_WIDTH), yc.reshape(t, GD_WIDTH),
                           w_gate, mix_w_branch[i].astype(BF16), mix_w_out[i].astype(BF16), ln2_g[i], ln2_b[i])
        xf, xb = _ffn_ln(xf, xb, ffn2_w_in[i], ffn2_w_out[i], ln3_g[i], ln3_b[i])
    return xf.reshape(b, l, d)
```

```python
import functools
import math

import jax
import jax.numpy as jnp
from jax import lax
from jax.experimental import pallas as pl
from jax.experimental.pallas import tpu as pltpu

F32 = jnp.float32
BF16 = jnp.bfloat16

D_MODEL = 1024
DEPTH = 2
D_FF = 2816
RW_HEADS = 8
RW_HEAD_DIM = 64
RW_WIDTH = 512
RW_DECAY_LORA = 64
RW_ICLR_LORA = 64
RW_GATE_LORA = 128
RW_COLS = 3 * RW_WIDTH + RW_DECAY_LORA + RW_ICLR_LORA + RW_GATE_LORA
RW_GN_EPS = 64e-5
DA_HEADS = 4
DA_HEAD_DIM = 64
DA_WIDTH = 512
DA_COLS = 3 * DA_WIDTH
GD_HEADS = 4
GD_HEAD_DIM = 128
GD_WIDTH = 512
GD_CONV = 4
GD_COLS = 3 * GD_WIDTH + 2 * GD_HEADS + GD_WIDTH
N_BRANCH = 3
ALPHA = (2.0 * DEPTH) ** 0.25

LANES = 128
SUBLANES = 8
CHUNK = 64
PAIR = 2 * CHUNK
DA_QUERY_GROUP = 256
MERGE_ROWS = 512
DA_ONES_ROWS = 16
LOG2E = math.log2(math.e)
GD_AB_PAD = LANES
GD_IN = 3 * GD_WIDTH + GD_AB_PAD + GD_WIDTH
VMEM_LIMIT = 56 * 1024 * 1024
SCAN_BATCHES = 2
SCAN_CHUNKS = 2


def _cparams(n_grid):
    return pltpu.CompilerParams(dimension_semantics=("arbitrary",) * n_grid,
                                vmem_limit_bytes=VMEM_LIMIT)


def _const_spec(shape):
    nd = len(shape)
    return pl.BlockSpec(shape, lambda *_: (0,) * nd, pipeline_mode=pl.Buffered(1))


def _mm(a, b):
    return jnp.dot(a.astype(BF16), b.astype(BF16), preferred_element_type=F32)


def _mm_nt(a, b):
    return lax.dot_general(a.astype(BF16), b.astype(BF16), (((1,), (1,)), ((), ())),
                           preferred_element_type=F32)


def _mm_tn(a, b):
    return lax.dot_general(a.astype(BF16), b.astype(BF16), (((0,), (0,)), ((), ())),
                           preferred_element_type=F32)


def _split_lhs_dot(x, exact_rhs, terms):
    acc = None
    rem = x
    for _ in range(terms):
        hi = rem.astype(BF16)
        part = jnp.dot(hi, exact_rhs, preferred_element_type=F32)
        acc = part if acc is None else acc + part
        rem = rem - hi.astype(F32)
    return acc


def _split_rhs_dot(exact_lhs, x, terms):
    acc = None
    rem = x
    for _ in range(terms):
        hi = rem.astype(BF16)
        part = jnp.dot(exact_lhs, hi, preferred_element_type=F32)
        acc = part if acc is None else acc + part
        rem = rem - hi.astype(F32)
    return acc


def _layernorm(z, g, b, eps=1e-5):
    mu = jnp.mean(z, axis=-1, keepdims=True)
    zc = z - mu
    var = jnp.mean(zc * zc, axis=-1, keepdims=True)
    return zc * lax.rsqrt(var + eps) * g + b


def _iota(shape, dim):
    return lax.broadcasted_iota(jnp.int32, shape, dim)


def _inv_unit_lower_many(lows):
    eye = (_iota((PAIR, PAIR), 0) == _iota((PAIR, PAIR), 1)).astype(F32)
    pw = [-low for low in lows]
    inv = [eye + n for n in pw]
    for _ in range(int(math.log2(CHUNK)) - 1):
        pw = [_mm(p, p) for p in pw]
        inv = [i + _mm(i, p) for i, p in zip(inv, pw)]
    return inv


def _stack_heads(x, lo_mask):
    return jnp.concatenate([jnp.where(lo_mask, x, 0.0), jnp.where(lo_mask, 0.0, x)], axis=0)


def _chunk_tril(n_chunks):
    r = jnp.arange(n_chunks * CHUNK)
    return ((r[:, None] >= r[None, :]) & (r[:, None] // CHUNK == r[None, :] // CHUNK)).astype(BF16)


def _rows_of_chunk(x, row):
    n = x.shape[0] // CHUNK
    parts = [jnp.broadcast_to(x[i * CHUNK + row:i * CHUNK + row + 1], (CHUNK,) + x.shape[1:]) for i in range(n)]
    return jnp.concatenate(parts, axis=0) if n > 1 else parts[0]


def _ffn_ln_kernel(x_ref, *refs, tf):
    if len(refs) == 8:
        xb_ref, win_ref, wo_ref, g_ref, b_ref, y_ref, yb_ref, acc_ref = refs
        xb = xb_ref[...]
    else:
        win_ref, wo_ref, g_ref, b_ref, y_ref, yb_ref, acc_ref = refs
        xb = x_ref[...].astype(BF16)
    dff = wo_ref.shape[0]
    nf = dff // tf

    def gate_up(f):
        cols = slice(f * tf, (f + 1) * tf)
        return (jnp.dot(xb, win_ref[:, cols], preferred_element_type=F32),
                jnp.dot(xb, win_ref[:, dff + f * tf:dff + (f + 1) * tf], preferred_element_type=F32))

    nxt = gate_up(0)
    for f in range(nf):
        gate, up = nxt
        if f + 1 < nf:
            nxt = gate_up(f + 1)
        act = (gate * jax.nn.sigmoid(gate) * up).astype(BF16)
        part = jnp.dot(act, wo_ref[f * tf:(f + 1) * tf, :], preferred_element_type=F32)
        if f == 0:
            acc_ref[...] = part
        else:
            acc_ref[...] += part
    y = _layernorm(ALPHA * x_ref[...] + 0.5 * acc_ref[...], g_ref[...], b_ref[...])
    y_ref[...] = y
    yb_ref[...] = y.astype(BF16)


def _ffn_ln(x, xb, w_in, w_out, g, b, *, tm=1024, tf=256):
    t, d = x.shape
    dff = w_out.shape[0]
    assert dff % tf == 0 and t % tm == 0
    row = pl.BlockSpec((tm, d), lambda i: (i, 0))
    acts = [x] if xb is None else [x, xb]
    return pl.pallas_call(
        functools.partial(_ffn_ln_kernel, tf=tf),
        grid=(t // tm,),
        in_specs=[row] * len(acts) + [_const_spec((d, 2 * dff)), _const_spec((dff, d)), _const_spec((1, d)),
                                      _const_spec((1, d))],
        out_specs=[row, row],
        out_shape=[jax.ShapeDtypeStruct((t, d), F32), jax.ShapeDtypeStruct((t, d), BF16)],
        scratch_shapes=[pltpu.VMEM((tm, d), F32)],
        compiler_params=_cparams(1),
        name="ffn_ln",
    )(*acts, w_in.astype(BF16), w_out.astype(BF16), g.reshape(1, d), b.reshape(1, d))


def _mix_in_kernel(xb_ref, w_ref, cw_ref, rw_ref, da_ref, gd_ref, stage_ref, *, tn, tiles_per_seq):
    dh = GD_HEAD_DIM

    @pl.when(pl.program_id(0) % tiles_per_seq == 0)
    def _():
        stage_ref[:, 0:SUBLANES, :] = jnp.zeros((stage_ref.shape[0], SUBLANES, dh), F32)

    xb = xb_ref[...]
    tm = xb.shape[0]

    def conv_act(res, h0):
        c = h0 // dh
        stage_ref[c, SUBLANES:SUBLANES + tm, :] = res
        cw = cw_ref[:, h0:h0 + dh]
        conv = res * cw[GD_CONV - 1:GD_CONV]
        for j in range(GD_CONV - 1):
            r0 = SUBLANES - (GD_CONV - 1 - j)
            conv = conv + stage_ref[c, r0:r0 + tm, :] * cw[j:j + 1]
        stage_ref[c, 0:SUBLANES, :] = res[tm - SUBLANES:tm]
        act = conv * jax.nn.sigmoid(conv)
        if h0 < 2 * GD_WIDTH:
            act = act * lax.rsqrt(jnp.sum(act * act, axis=-1, keepdims=True) + 1e-6)
            if h0 < GD_WIDTH:
                act = act * dh ** -0.5
        return act

    off = 0
    for out_ref in (rw_ref, da_ref, gd_ref):
        width = out_ref.shape[-1]
        for c0 in range(0, width, tn):
            c1 = min(c0 + tn, width)
            res = jnp.dot(xb, w_ref[:, off + c0:off + c1], preferred_element_type=F32)
            if out_ref is gd_ref and c0 < 3 * GD_WIDTH:
                for h0 in range(c0, c1, dh):
                    out_ref[:, h0:h0 + dh] = conv_act(res[:, h0 - c0:h0 - c0 + dh], h0)
            else:
                out_ref[:, c0:c1] = res.astype(out_ref.dtype)
        off += width


def _mix_in(xb, w_cat, conv_w, seq_len, *, tm=512, tn=256):
    t, d = xb.shape
    n = w_cat.shape[1]
    assert n == RW_COLS + DA_COLS + GD_IN and t % tm == 0 and seq_len % tm == 0 and tn % GD_HEAD_DIM == 0
    return pl.pallas_call(
        functools.partial(_mix_in_kernel, tn=tn, tiles_per_seq=seq_len // tm),
        grid=(t // tm,),
        in_specs=[pl.BlockSpec((tm, d), lambda i: (i, 0)), _const_spec((d, n)), _const_spec(conv_w.shape)],
        out_specs=[pl.BlockSpec((tm, RW_COLS), lambda i: (i, 0)),
                   pl.BlockSpec((tm, DA_COLS), lambda i: (i, 0)),
                   pl.BlockSpec((tm, GD_IN), lambda i: (i, 0))],
        out_shape=[jax.ShapeDtypeStruct((t, RW_COLS), F32),
                   jax.ShapeDtypeStruct((t, DA_COLS), BF16),
                   jax.ShapeDtypeStruct((t, GD_IN), F32)],
        scratch_shapes=[pltpu.VMEM((3 * GD_HEADS, SUBLANES + tm, GD_HEAD_DIM), F32)],
        compiler_params=_cparams(1),
        name="mix_in",
    )(xb, w_cat, conv_w.astype(F32))


def _rwkv_kernel(h_ref, mu_ref, w0_ref, a0_ref, kk_ref, ka_ref, rk_ref, lng_ref, lnb_ref,
                 lora_ref, gup_ref, bd_ref, tril_ref, y_ref, prev_ref, s_ref, *, nb, nc):
    rows_b = nc * CHUNK

    @pl.when(pl.program_id(1) == 0)
    def _():
        prev_ref[:, 0:SUBLANES, :] = jnp.zeros((nb, SUBLANES, RW_COLS), F32)
        s_ref[...] = jnp.zeros_like(s_ref)

    n_pairs = RW_HEADS // 2
    parts = []
    for b in range(nb):
        h = h_ref[b]
        prev_ref[b, SUBLANES:SUBLANES + rows_b, :] = h
        hprev = prev_ref[b, SUBLANES - 1:SUBLANES - 1 + rows_b, :]
        prev_ref[b, 0:SUBLANES, :] = h[rows_b - SUBLANES:rows_b]
        parts.append(h + mu_ref[...] * (hprev - h))
    hs = jnp.concatenate(parts, axis=0) if nb > 1 else parts[0]
    n_rows = nb * rows_b

    w = RW_WIDTH
    r = hs[:, 0:w]
    k = hs[:, w:2 * w]
    v = hs[:, 2 * w:3 * w]
    wa = hs[:, 3 * w:3 * w + LANES]
    gd = hs[:, 3 * w + LANES:3 * w + 2 * LANES]
    lo_n = _iota((n_rows, LANES), 1) < RW_HEAD_DIM
    lora = _mm(jnp.where(lo_n, jnp.tanh(wa), wa), lora_ref[...])
    logw = -math.exp(-0.5) * jax.nn.sigmoid(w0_ref[...] + lora[:, 0:w])
    a = jax.nn.sigmoid(a0_ref[...] + lora[:, w:2 * w])
    g = _mm(jax.nn.sigmoid(gd), gup_ref[...])

    bd = bd_ref[...]
    kkr = k * kk_ref[...]
    kk = kkr * lax.rsqrt(_split_lhs_dot(kkr * kkr, bd, 1) + 1e-6)
    k2 = k * (1.0 + (a - 1.0) * ka_ref[...])
    bvec = kk * a
    bonus = _split_lhs_dot(r * k2 * rk_ref[...], bd, 1) * v

    e_in = _split_rhs_dot(tril_ref[...], logw, 2)
    rho = _rows_of_chunk(e_in, CHUNK // 2 - 1)
    r_t = r * jnp.exp(e_in - rho)
    kk_t = kk * jnp.exp(e_in - logw - rho)
    e_neg = jnp.exp(rho - e_in)
    k_h = k2 * e_neg
    b_h = bvec * e_neg

    lane = _iota((CHUNK, LANES), 1)
    lo = lane < RW_HEAD_DIM
    tt = _iota((CHUNK, LANES), 0)
    ss = lane & (CHUNK - 1)
    strict = tt > ss
    incl = tt >= ss

    units = [(b, ci, p) for b in range(nb) for ci in range(nc) for p in range(n_pairs)]

    def tile(x, u):
        b, ci, p = u
        r0 = (b * nc + ci) * CHUNK
        return x[r0:r0 + CHUNK, p * LANES:(p + 1) * LANES]

    kks = [_stack_heads(tile(kk_t, u), lo) for u in units]
    rs = [_stack_heads(tile(r_t, u), lo) for u in units]
    vs = [_stack_heads(tile(v, u), lo) for u in units]
    khs = [_stack_heads(tile(k_h, u), lo) for u in units]
    bhs = [_stack_heads(tile(b_h, u), lo) for u in units]
    z = [_mm_nt(jnp.concatenate([tile(kk_t, u), tile(r_t, u)], axis=0), jnp.concatenate([kh, bh], axis=0))
         for u, kh, bh in zip(units, khs, bhs)]
    ak = [_stack_heads(jnp.where(strict, zz[0:CHUNK, 0:PAIR], 0.0), lo) for zz in z]
    ab = [_stack_heads(jnp.where(strict, zz[0:CHUNK, PAIR:2 * PAIR], 0.0), lo) for zz in z]
    bk = [_stack_heads(jnp.where(incl, zz[CHUNK:PAIR, 0:PAIR], 0.0), lo) for zz in z]
    bb = [_stack_heads(jnp.where(incl, zz[CHUNK:PAIR, PAIR:2 * PAIR], 0.0), lo) for zz in z]
    inv = _inv_unit_lower_many(ab)
    akv = [_mm(x, y) for x, y in zip(ak, vs)]
    bkv = [_mm(x, y) for x, y in zip(bk, vs)]
    pq = [_mm(i, jnp.concatenate([x, y], axis=1)) for i, x, y in zip(inv, kks, akv)]
    bpq = [_mm(x, y) for x, y in zip(bb, pq)]
    rp = [x - y[:, 0:PAIR] for x, y in zip(rs, bpq)]
    y0 = [x - y[:, PAIR:2 * PAIR] for x, y in zip(bkv, bpq)]
    ptb = [_mm_tn(x[:, 0:PAIR], y) for x, y in zip(pq, bhs)]
    hm = [_mm_tn(jnp.concatenate([x, y[:, PAIR:2 * PAIR]], axis=0), jnp.concatenate([kh, -bh], axis=0))
          for x, y, kh, bh in zip(vs, pq, khs, bhs)]

    s_scale = jnp.exp(rho)
    d_out = jnp.exp(_rows_of_chunk(e_in, CHUNK - 1) - rho)
    yo = {}
    chains = [(b, p) for b in range(nb) for p in range(n_pairs)]
    state = {bp: s_ref[bp[0] * n_pairs + bp[1]] for bp in chains}
    for ci in range(nc):
        for b, p in chains:
            i = units.index((b, ci, p))
            r0 = (b * nc + ci) * CHUNK
            sl = slice(p * LANES, (p + 1) * LANES)
            sp = state[(b, p)] * s_scale[r0:r0 + 1, sl]
            yo[(b, ci, p)] = _mm_nt(rp[i], sp) + y0[i]
            state[(b, p)] = (sp - _mm(sp, ptb[i]) + hm[i]) * d_out[r0:r0 + 1, sl]
    for b, p in chains:
        s_ref[b * n_pairs + p] = state[(b, p)]
    y = jnp.concatenate(
        [jnp.concatenate([yo[(b, ci, p)][0:CHUNK] + yo[(b, ci, p)][CHUNK:PAIR] for p in range(n_pairs)], axis=1)
         for b in range(nb) for ci in range(nc)], axis=0)

    inv_n = 1.0 / RW_HEAD_DIM
    ym = _split_lhs_dot(y, bd, 1) * inv_n
    yc = y - ym
    yv = _split_lhs_dot(yc * yc, bd, 1) * inv_n
    yn = yc * lax.rsqrt(yv + RW_GN_EPS) * lng_ref[...] + lnb_ref[...]
    out = ((yn + bonus) * g).astype(y_ref.dtype)
    for b in range(nb):
        y_ref[b] = out[b * rows_b:(b + 1) * rows_b]


def _rwkv7(h_rw, mu, w0, w_up, a0, a_up, g_up, k_k, k_a, r_k, ln_g, ln_b, *, nb=SCAN_BATCHES, nc=SCAN_CHUNKS):
    b, l, _ = h_rw.shape
    w = RW_WIDTH
    nb = min(nb, b)
    rows_b = nc * CHUNK
    assert l % rows_b == 0 and b % nb == 0
    lora = jnp.zeros((LANES, 2 * w), F32)
    lora = lora.at[0:RW_DECAY_LORA, 0:w].set(w_up).at[RW_DECAY_LORA:LANES, w:2 * w].set(a_up)
    hid = jnp.arange(w) // RW_HEAD_DIM
    bd = (hid[:, None] == hid[None, :]).astype(BF16)
    vec = lambda t: t.reshape(1, -1).astype(F32)
    params = [vec(mu), vec(w0), vec(a0), vec(k_k), vec(k_a), vec(r_k), vec(ln_g), vec(ln_b),
              lora.astype(BF16), g_up.astype(BF16), bd, _chunk_tril(nb * nc)]
    return pl.pallas_call(
        functools.partial(_rwkv_kernel, nb=nb, nc=nc),
        grid=(b // nb, l // rows_b),
        in_specs=[pl.BlockSpec((nb, rows_b, RW_COLS), lambda i, j: (i, j, 0))]
        + [_const_spec(p.shape) for p in params],
        out_specs=pl.BlockSpec((nb, rows_b, w), lambda i, j: (i, j, 0)),
        out_shape=jax.ShapeDtypeStruct((b, l, w), BF16),
        scratch_shapes=[pltpu.VMEM((nb, SUBLANES + rows_b, RW_COLS), F32),
                        pltpu.VMEM((nb * RW_HEADS // 2, PAIR, PAIR), F32)],
        compiler_params=_cparams(2),
        name="rwkv7",
    )(h_rw, *params)


def _attn_kernel(slope_ref, lam_ref, ng_ref, q_ref, k_ref, v_ref, o_ref, ka_ref, vt_ref, s0_ref, s1_ref, m_ref,
                 acc_ref, *, tq, tk, lam_init):
    d = DA_HEAD_DIM
    dv = 2 * d
    n_blk = vt_ref.shape[0]
    slope2 = slope_ref[0, 0:1, 0:1] * LOG2E

    lane_k = _iota((tk, LANES), 1)
    bias = slope2 * _iota((tk, LANES), 0).astype(F32)
    hi = bias.astype(BF16).astype(F32)
    mid = (bias - hi).astype(BF16).astype(F32)
    low = bias - hi - mid
    pieces = jnp.where(lane_k == 0, hi, jnp.where(lane_k == 1, mid, jnp.where(lane_k == 2, low, 0.0))).astype(BF16)
    ones_row = (_iota((DA_ONES_ROWS, tk), 0) == 0).astype(BF16)
    for jb in range(n_blk):
        rows = slice(jb * tk, (jb + 1) * tk)
        ka_ref[rows, 0:LANES] = k_ref[0, rows, :]
        ka_ref[rows, LANES:2 * LANES] = pieces
        vt_ref[jb, 0:dv, :] = v_ref[0, rows, :].astype(F32).T.astype(BF16)
        vt_ref[jb, dv:dv + DA_ONES_ROWS, :] = ones_row

    lp = lam_ref[...]
    lam = (jnp.exp(jnp.sum(lp[0:1] * lp[1:2], axis=-1, keepdims=True))
           - jnp.exp(jnp.sum(lp[2:3] * lp[3:4], axis=-1, keepdims=True)) + lam_init)
    lane_q = _iota((tq, LANES), 1)
    lo = lane_q < d
    bias_ones = (lane_q < 3).astype(BF16)

    def q_operand(qi):
        q = q_ref[0, qi * tq:(qi + 1) * tq, :].astype(F32) * (d ** -0.5 * LOG2E)
        return jnp.concatenate([
            jnp.concatenate([jnp.where(lo, q, 0.0).astype(BF16), bias_ones], axis=1),
            jnp.concatenate([jnp.where(lo, 0.0, q).astype(BF16), bias_ones], axis=1)], axis=0)

    def scores(q2, j, dst_ref):
        dst_ref[...] = lax.dot_general(ka_ref[j * tk:(j + 1) * tk, :], q2, (((1,), (1,)), ((), ())),
                                       preferred_element_type=F32)

    def consume(qi, j, src_ref):
        shift = slope2 * float(j * tk - qi * tq)
        for c0 in range(0, 2 * tq, DA_QUERY_GROUP):
            cols = slice(c0, c0 + DA_QUERY_GROUP)
            nk = min(tk, c0 % tq + DA_QUERY_GROUP) if j == qi else tk
            s = src_ref[0:nk, cols]
            if j == qi:
                key = _iota((nk, DA_QUERY_GROUP), 0)
                query = (_iota((nk, DA_QUERY_GROUP), 1) + c0) & (tq - 1)
                s = jnp.where(key <= query, s, -jnp.inf)
            vt = vt_ref[j, :, 0:nk]
            if j == 0:
                m_new = jnp.max(s, axis=0, keepdims=True) + shift
                p = jnp.exp2(s - (m_new - shift)).astype(BF16)
                acc_ref[:, cols] = jnp.dot(vt, p, preferred_element_type=F32)
            else:
                m_old = m_ref[:, cols]
                m_new = jnp.maximum(m_old, jnp.max(s, axis=0, keepdims=True) + shift)
                p = jnp.exp2(s - (m_new - shift)).astype(BF16)
                acc_ref[:, cols] = (jnp.exp2(m_old - m_new) * acc_ref[:, cols]
                                    + jnp.dot(vt, p, preferred_element_type=F32))
            m_ref[:, cols] = m_new

    def finalize(qi):
        acc = acc_ref[...]
        ot = acc[0:dv] / acc[dv:dv + 1]
        ot = ot[:, 0:tq] - lam * ot[:, tq:2 * tq]
        ot = ot * lax.rsqrt(jnp.mean(ot * ot, axis=0, keepdims=True) + 1e-5)
        o_ref[0, qi * tq:(qi + 1) * tq, :] = (ot.T * ng_ref[...] * (1.0 - lam_init)).astype(o_ref.dtype)

    pairs = [(qi, j) for qi in range(n_blk) for j in range(qi + 1)]
    bufs = (s0_ref, s1_ref)
    q2 = {0: q_operand(0)}
    scores(q2[0], 0, bufs[0])
    for idx, (qi, j) in enumerate(pairs):
        if idx + 1 < len(pairs):
            nqi, nj = pairs[idx + 1]
            if nqi not in q2:
                q2[nqi] = q_operand(nqi)
            scores(q2[nqi], nj, bufs[(idx + 1) % 2])
        consume(qi, j, bufs[idx % 2])
        if j == qi:
            finalize(qi)


def _diff_attention(h_da, lam_q1, lam_k1, lam_q2, lam_k2, norm_g, lam_init, *, tq=512):
    b, l, _ = h_da.shape
    tq = min(tq, l)
    tk = tq
    assert l % tq == 0 and tq & (tq - 1) == 0
    hh = DA_HEADS
    slopes = jnp.exp2(-8.0 * jnp.arange(1, hh + 1, dtype=F32) / hh)
    slopes = jnp.broadcast_to(slopes[:, None, None], (hh, SUBLANES, LANES))
    lam_p = jnp.stack([lam_q1, lam_k1, lam_q2, lam_k2]).astype(F32)
    ng = norm_g.reshape(1, 2 * DA_HEAD_DIM).astype(F32)
    seq = lambda col0: pl.BlockSpec((1, l, LANES), lambda i, h: (i, 0, col0 + h))
    return pl.pallas_call(
        functools.partial(_attn_kernel, tq=tq, tk=tk, lam_init=lam_init),
        grid=(b, hh),
        in_specs=[pl.BlockSpec((1, SUBLANES, LANES), lambda i, h: (h, 0, 0)),
                  pl.BlockSpec(lam_p.shape, lambda i, h: (0, 0)),
                  pl.BlockSpec(ng.shape, lambda i, h: (0, 0)),
                  seq(0), seq(hh), seq(2 * hh)],
        out_specs=seq(0),
        out_shape=jax.ShapeDtypeStruct((b, l, DA_WIDTH), BF16),
        scratch_shapes=[pltpu.VMEM((l, 2 * LANES), BF16),
                        pltpu.VMEM((l // tk, LANES + DA_ONES_ROWS, tk), BF16),
                        pltpu.VMEM((tk, 2 * tq), F32), pltpu.VMEM((tk, 2 * tq), F32),
                        pltpu.VMEM((1, 2 * tq), F32), pltpu.VMEM((LANES + DA_ONES_ROWS, 2 * tq), F32)],
        compiler_params=_cparams(2),
        name="diffattn",
    )(slopes, lam_p, ng, h_da, h_da, h_da)


def _softplus(x):
    return jnp.maximum(x, 0.0) + jnp.log(1.0 + jnp.exp(-jnp.abs(x)))


def _gdn_kernel(h_ref, alog_ref, dtb_ref, ng_ref, tril_ref, y_ref, s_ref, *, nb, nc):
    @pl.when(pl.program_id(1) == 0)
    def _():
        s_ref[...] = jnp.zeros_like(s_ref)

    wq = 3 * GD_WIDTH
    dh = GD_HEAD_DIM
    qkv = jnp.concatenate([h_ref[b, :, 0:wq] for b in range(nb)], axis=0)
    ab = jnp.concatenate([h_ref[b, :, wq:wq + GD_AB_PAD] for b in range(nb)], axis=0)
    gfull = -jnp.exp(alog_ref[...]) * _softplus(ab + dtb_ref[...])
    gcum = _split_rhs_dot(tril_ref[...], gfull, 3)
    glast = _rows_of_chunk(gcum, CHUNK - 1)
    beta_full = jax.nn.sigmoid(ab)

    qn = [qkv[:, h * dh:(h + 1) * dh] for h in range(GD_HEADS)]
    kn = [qkv[:, GD_WIDTH + h * dh:GD_WIDTH + (h + 1) * dh] for h in range(GD_HEADS)]
    vh = [qkv[:, 2 * GD_WIDTH + h * dh:2 * GD_WIDTH + (h + 1) * dh] for h in range(GD_HEADS)]

    r0 = _iota((PAIR, PAIR), 0)
    c0 = _iota((PAIR, PAIR), 1)
    same = (r0 >= CHUNK) == (c0 >= CHUNK)
    incl = same & (r0 >= c0)
    strict = same & (r0 > c0)

    n_pairs = GD_HEADS // 2
    units = [(b, ci, p) for b in range(nb) for ci in range(nc) for p in range(n_pairs)]

    def stack(per_head, u, lane_of=None):
        b, ci, p = u
        rr = slice((b * nc + ci) * CHUNK, (b * nc + ci + 1) * CHUNK)
        if lane_of is None:
            return jnp.concatenate([per_head[2 * p][rr], per_head[2 * p + 1][rr]], axis=0)
        return jnp.concatenate([per_head[rr, lane_of + 2 * p:lane_of + 2 * p + 1],
                                per_head[rr, lane_of + 2 * p + 1:lane_of + 2 * p + 2]], axis=0)

    q_s = [stack(qn, u) for u in units]
    k_s = [stack(kn, u) for u in units]
    v_s = [stack(vh, u) for u in units]
    g_s = [stack(gcum, u, 0) for u in units]
    gl_s = [stack(glast, u, 0) for u in units]
    beta_s = [stack(beta_full, u, GD_HEADS) for u in units]

    decay = []
    for gs in g_s:
        g_b = jnp.broadcast_to(gs, (PAIR, PAIR))
        decay.append(jnp.exp(jnp.where(incl, g_b - g_b.T, -jnp.inf)))
    kb = [x * y for x, y in zip(k_s, beta_s)]
    kk = [_mm_nt(jnp.concatenate([x, y], axis=0), z) for x, y, z in zip(kb, q_s, k_s)]
    low = [jnp.where(strict, x[0:PAIR] * dc, 0.0) for x, dc in zip(kk, decay)]
    intra = [x[PAIR:2 * PAIR] * dc for x, dc in zip(kk, decay)]
    inv = _inv_unit_lower_many(low)
    eg = [jnp.exp(gs) for gs in g_s]
    uw = [_mm(i, jnp.concatenate([v * bt, x * e], axis=1)) for i, v, bt, x, e in zip(inv, v_s, beta_s, kb, eg)]
    qg = [x * e for x, e in zip(q_s, eg)]
    kd = [x * jnp.exp(gl - gs) for x, gl, gs in zip(k_s, gl_s, g_s)]

    v_new = {}
    o_state = {}
    chains = [(b, h) for b in range(nb) for h in range(GD_HEADS)]
    state = {bh: s_ref[bh[0] * GD_HEADS + bh[1]] for bh in chains}
    for ci in range(nc):
        ws = {}
        for b, h in chains:
            p, i_h = divmod(h, 2)
            rows = slice(i_h * CHUNK, (i_h + 1) * CHUNK)
            i = units.index((b, ci, p))
            ws[(b, h)] = _mm(jnp.concatenate([uw[i][rows, dh:2 * dh], qg[i][rows]], axis=0), state[(b, h)])
        for b, h in chains:
            p, i_h = divmod(h, 2)
            rows = slice(i_h * CHUNK, (i_h + 1) * CHUNK)
            i = units.index((b, ci, p))
            vn = uw[i][rows, 0:dh] - ws[(b, h)][0:CHUNK]
            v_new[(b, ci, h)] = vn
            o_state[(b, ci, h)] = ws[(b, h)][CHUNK:PAIR]
            state[(b, h)] = state[(b, h)] * jnp.exp(gl_s[i][rows][0:1]) + _mm_tn(kd[i][rows], vn)
    for b, h in chains:
        s_ref[b * GD_HEADS + h] = state[(b, h)]

    for i, (b, ci, p) in enumerate(units):
        vn = jnp.concatenate([v_new[(b, ci, 2 * p)], v_new[(b, ci, 2 * p + 1)]], axis=0)
        o = jnp.concatenate([o_state[(b, ci, 2 * p)], o_state[(b, ci, 2 * p + 1)]], axis=0) + _mm(intra[i], vn)
        for i_h in range(2):
            h = 2 * p + i_h
            oh = o[i_h * CHUNK:(i_h + 1) * CHUNK]
            oh = oh * lax.rsqrt(jnp.mean(oh * oh, axis=-1, keepdims=True) + 1e-6) * ng_ref[...]
            zc = wq + GD_AB_PAD + h * dh
            z = h_ref[b, ci * CHUNK:(ci + 1) * CHUNK, zc:zc + dh]
            y_ref[b, ci * CHUNK:(ci + 1) * CHUNK, h * dh:(h + 1) * dh] = (
                oh * (z * jax.nn.sigmoid(z))).astype(y_ref.dtype)


def _gated_deltanet(h_gd, a_log, dt_bias, norm_g, *, nb=2 * SCAN_BATCHES, nc=SCAN_CHUNKS):
    b, l, _ = h_gd.shape
    nb = min(nb, b)
    rows_b = nc * CHUNK
    assert l % rows_b == 0 and b % nb == 0
    pad = lambda t: jnp.zeros((1, GD_AB_PAD), F32).at[0, 0:GD_HEADS].set(t.astype(F32))
    params = [pad(a_log), pad(dt_bias), norm_g.reshape(1, GD_HEAD_DIM).astype(F32), _chunk_tril(nb * nc)]
    return pl.pallas_call(
        functools.partial(_gdn_kernel, nb=nb, nc=nc),
        grid=(b // nb, l // rows_b),
        in_specs=[pl.BlockSpec((nb, rows_b, GD_IN), lambda i, j: (i, j, 0))]
        + [_const_spec(p.shape) for p in params],
        out_specs=pl.BlockSpec((nb, rows_b, GD_WIDTH), lambda i, j: (i, j, 0)),
        out_shape=jax.ShapeDtypeStruct((b, l, GD_WIDTH), BF16),
        scratch_shapes=[pltpu.VMEM((nb * GD_HEADS, GD_HEAD_DIM, GD_HEAD_DIM), F32)],
        compiler_params=_cparams(2),
        name="gdn",
    )(h_gd, *params)


def _merge_ln_kernel(x_ref, xb_ref, ya_ref, yb_ref, yc_ref, wgate_ref, wbr_ref, wout_ref, g_ref, b_ref,
                     y_ref, y16_ref):
    for r0 in range(0, x_ref.shape[0], MERGE_ROWS):
        rows = slice(r0, r0 + MERGE_ROWS)
        xb = xb_ref[rows, :]
        merged = None
        for n, br_ref in enumerate((ya_ref, yb_ref, yc_ref)):
            gate = jax.nn.sigmoid(jnp.dot(xb, wgate_ref[n], preferred_element_type=F32))
            term = gate * jnp.dot(br_ref[rows, :], wbr_ref[n], preferred_element_type=F32)
            merged = term if merged is None else merged + term
        mix = jnp.dot(merged.astype(BF16), wout_ref[...], preferred_element_type=F32)
        y = _layernorm(ALPHA * x_ref[rows, :] + mix, g_ref[...], b_ref[...])
        y_ref[rows, :] = y
        y16_ref[rows, :] = y.astype(BF16)


def _merge_ln(x, xb, ya, yb, yc, w_gate, w_branch, w_out, g, b, *, tm=2 * MERGE_ROWS):
    t, d = x.shape
    bw = ya.shape[-1]
    assert t % tm == 0 and tm % MERGE_ROWS == 0
    row = pl.BlockSpec((tm, d), lambda i: (i, 0))
    brow = pl.BlockSpec((tm, bw), lambda i: (i, 0))
    return pl.pallas_call(
        _merge_ln_kernel,
        grid=(t // tm,),
        in_specs=[row, row, brow, brow, brow, _const_spec((N_BRANCH, d, d)),
                  _const_spec((N_BRANCH, bw, d)), _const_spec((d, d)), _const_spec((1, d)), _const_spec((1, d))],
        out_specs=[row, row],
        out_shape=[jax.ShapeDtypeStruct((t, d), F32), jax.ShapeDtypeStruct((t, d), BF16)],
        compiler_params=_cparams(1),
        name="merge_ln",
    )(x, xb, ya, yb, yc, w_gate, w_branch, w_out, g.reshape(1, d), b.reshape(1, d))


def _split_mix_w_in(w_in):
    d = w_in.shape[0]
    o1 = RW_COLS
    o2 = o1 + DA_COLS
    o3 = o2 + GD_COLS
    gq = o2 + 3 * GD_WIDTH
    ab = jnp.zeros((d, GD_AB_PAD), w_in.dtype).at[:, 0:2 * GD_HEADS].set(w_in[:, gq:gq + 2 * GD_HEADS])
    w_cat = jnp.concatenate([w_in[:, 0:o2], w_in[:, o2:gq], ab, w_in[:, gq + 2 * GD_HEADS:o3]], axis=1)
    w_gate = w_in[:, o3:].reshape(d, N_BRANCH, d).transpose(1, 0, 2)
    return w_cat.astype(BF16), w_gate.astype(BF16)


def kernel(x, ffn1_w_in, ffn1_w_out, ln1_g, ln1_b, mix_w_in, rw_shift_mu, rw_w0, rw_w_up, rw_a0, rw_a_up, rw_g_up, rw_k_k, rw_k_a, rw_r_k, rw_ln_g, rw_ln_b, da_lam_q1, da_lam_k1, da_lam_q2, da_lam_k2, da_norm_g, gd_conv_w, gd_a_log, gd_dt_bias, gd_norm_g, mix_w_branch, mix_w_out, ln2_g, ln2_b, ffn2_w_in, ffn2_w_out, ln3_g, ln3_b):
    b, l, d = x.shape
    t = b * l
    xf = x.reshape(t, d)
    xb = None
    for i in range(DEPTH):
        xf, xb = _ffn_ln(xf, xb, ffn1_w_in[i], ffn1_w_out[i], ln1_g[i], ln1_b[i])
        w_cat, w_gate = _split_mix_w_in(mix_w_in[i])
        h_rw, h_da, h_gd = _mix_in(xb, w_cat, gd_conv_w[i], l)
        ya = _rwkv7(h_rw.reshape(b, l, RW_COLS), rw_shift_mu[i], rw_w0[i], rw_w_up[i], rw_a0[i], rw_a_up[i],
                    rw_g_up[i], rw_k_k[i], rw_k_a[i], rw_r_k[i].reshape(-1), rw_ln_g[i], rw_ln_b[i])
        lam_init = 0.8 - 0.6 * math.exp(-0.3 * i)
        yb = _diff_attention(h_da.reshape(b, l, DA_COLS), da_lam_q1[i], da_lam_k1[i], da_lam_q2[i],
                             da_lam_k2[i], da_norm_g[i], lam_init)
        yc = _gated_deltanet(h_gd.reshape(b, l, GD_IN), gd_a_log[i], gd_dt_bias[i], gd_norm_g[i])
        xf, xb = _merge_ln(xf, xb, ya.reshape(t, RW_WIDTH), yb.reshape(t, DA_WIDTH), yc.reshape(t, GD_WIDTH),
                           w_gate, mix_w_branch[i].astype(BF16), mix_w_out[i].astype(BF16), ln2_g[i], ln2_b[i])
        xf, xb = _ffn_ln(xf, xb, ffn2_w_in[i], ffn2_w_out[i], ln3_g[i], ln3_b[i])
    return xf.reshape(b, l, d)
```

```python
import functools
import math

import jax
import jax.numpy as jnp
from jax import lax
from jax.experimental import pallas as pl
from jax.experimental.pallas import tpu as pltpu

F32 = jnp.float32
BF16 = jnp.bfloat16

D_MODEL = 1024
DEPTH = 2
D_FF = 2816
RW_HEADS = 8
RW_HEAD_DIM = 64
RW_WIDTH = 512
RW_DECAY_LORA = 64
RW_ICLR_LORA = 64
RW_GATE_LORA = 128
RW_COLS = 3 * RW_WIDTH + RW_DECAY_LORA + RW_ICLR_LORA + RW_GATE_LORA
RW_GN_EPS = 64e-5
DA_HEADS = 4
DA_HEAD_DIM = 64
DA_WIDTH = 512
DA_COLS = 3 * DA_WIDTH
GD_HEADS = 4
GD_HEAD_DIM = 128
GD_WIDTH = 512
GD_CONV = 4
GD_COLS = 3 * GD_WIDTH + 2 * GD_HEADS + GD_WIDTH
N_BRANCH = 3
ALPHA = (2.0 * DEPTH) ** 0.25

LANES = 128
SUBLANES = 8
CHUNK = 64
PAIR = 2 * CHUNK
DA_BIAS_LANES = 3
DA_QUERY_GROUP = 512
MERGE_ROWS = 512
DA_ONES_ROWS = 16
LOG2E = math.log2(math.e)
GD_AB_PAD = LANES
GD_IN = 3 * GD_WIDTH + GD_AB_PAD + GD_WIDTH
VMEM_LIMIT = 56 * 1024 * 1024
SCAN_BATCHES = 2
SCAN_CHUNKS = 2


def _cparams(n_grid):
    return pltpu.CompilerParams(dimension_semantics=("arbitrary",) * n_grid,
                                vmem_limit_bytes=VMEM_LIMIT)


def _const_spec(shape):
    nd = len(shape)
    return pl.BlockSpec(shape, lambda *_: (0,) * nd, pipeline_mode=pl.Buffered(1))


def _mm(a, b):
    return jnp.dot(a.astype(BF16), b.astype(BF16), preferred_element_type=F32)


def _mm_nt(a, b):
    return lax.dot_general(a.astype(BF16), b.astype(BF16), (((1,), (1,)), ((), ())),
                           preferred_element_type=F32)


def _mm_tn(a, b):
    return lax.dot_general(a.astype(BF16), b.astype(BF16), (((0,), (0,)), ((), ())),
                           preferred_element_type=F32)


def _split_lhs_dot(x, exact_rhs, terms):
    acc = None
    rem = x
    for _ in range(terms):
        hi = rem.astype(BF16)
        part = jnp.dot(hi, exact_rhs, preferred_element_type=F32)
        acc = part if acc is None else acc + part
        rem = rem - hi.astype(F32)
    return acc


def _split_rhs_dot(exact_lhs, x, terms):
    acc = None
    rem = x
    for _ in range(terms):
        hi = rem.astype(BF16)
        part = jnp.dot(exact_lhs, hi, preferred_element_type=F32)
        acc = part if acc is None else acc + part
        rem = rem - hi.astype(F32)
    return acc


def _layernorm(z, g, b, eps=1e-5):
    mu = jnp.mean(z, axis=-1, keepdims=True)
    zc = z - mu
    var = jnp.mean(zc * zc, axis=-1, keepdims=True)
    return zc * lax.rsqrt(var + eps) * g + b


def _iota(shape, dim):
    return lax.broadcasted_iota(jnp.int32, shape, dim)


def _inv_unit_lower_many(lows):
    eye = (_iota((PAIR, PAIR), 0) == _iota((PAIR, PAIR), 1)).astype(F32)
    pw = [-low for low in lows]
    inv = [eye + n for n in pw]
    for _ in range(int(math.log2(CHUNK)) - 1):
        pw = [_mm(p, p) for p in pw]
        inv = [i + _mm(i, p) for i, p in zip(inv, pw)]
    return inv


def _stack_heads(x, lo_mask):
    return jnp.concatenate([jnp.where(lo_mask, x, 0.0), jnp.where(lo_mask, 0.0, x)], axis=0)


def _chunk_tril(n_chunks):
    r = jnp.arange(n_chunks * CHUNK)
    return ((r[:, None] >= r[None, :]) & (r[:, None] // CHUNK == r[None, :] // CHUNK)).astype(BF16)


def _rows_of_chunk(x, row):
    n = x.shape[0] // CHUNK
    parts = [jnp.broadcast_to(x[i * CHUNK + row:i * CHUNK + row + 1], (CHUNK,) + x.shape[1:]) for i in range(n)]
    return jnp.concatenate(parts, axis=0) if n > 1 else parts[0]


def _ffn_ln_kernel(x_ref, *refs, tf):
    if len(refs) == 8:
        xb_ref, win_ref, wo_ref, g_ref, b_ref, y_ref, yb_ref, acc_ref = refs
        xb = xb_ref[...]
    else:
        win_ref, wo_ref, g_ref, b_ref, y_ref, yb_ref, acc_ref = refs
        xb = x_ref[...].astype(BF16)
    dff = wo_ref.shape[0]
    nf = dff // tf

    def gate_up(f):
        cols = slice(f * tf, (f + 1) * tf)
        return (jnp.dot(xb, win_ref[:, cols], preferred_element_type=F32),
                jnp.dot(xb, win_ref[:, dff + f * tf:dff + (f + 1) * tf], preferred_element_type=F32))

    nxt = gate_up(0)
    for f in range(nf):
        gate, up = nxt
        if f + 1 < nf:
            nxt = gate_up(f + 1)
        act = (gate * jax.nn.sigmoid(gate) * up).astype(BF16)
        part = jnp.dot(act, wo_ref[f * tf:(f + 1) * tf, :], preferred_element_type=F32)
        if f == 0:
            acc_ref[...] = part
        else:
            acc_ref[...] += part
    y = _layernorm(ALPHA * x_ref[...] + 0.5 * acc_ref[...], g_ref[...], b_ref[...])
    y_ref[...] = y
    yb_ref[...] = y.astype(BF16)


def _ffn_ln(x, xb, w_in, w_out, g, b, *, tm=1024, tf=256):
    t, d = x.shape
    dff = w_out.shape[0]
    assert dff % tf == 0 and t % tm == 0
    row = pl.BlockSpec((tm, d), lambda i: (i, 0))
    acts = [x] if xb is None else [x, xb]
    return pl.pallas_call(
        functools.partial(_ffn_ln_kernel, tf=tf),
        grid=(t // tm,),
        in_specs=[row] * len(acts) + [_const_spec((d, 2 * dff)), _const_spec((dff, d)), _const_spec((1, d)),
                                      _const_spec((1, d))],
        out_specs=[row, row],
        out_shape=[jax.ShapeDtypeStruct((t, d), F32), jax.ShapeDtypeStruct((t, d), BF16)],
        scratch_shapes=[pltpu.VMEM((tm, d), F32)],
        compiler_params=_cparams(1),
        name="ffn_ln",
    )(*acts, w_in.astype(BF16), w_out.astype(BF16), g.reshape(1, d), b.reshape(1, d))


def _mix_in_kernel(xb_ref, w_ref, cw_ref, rw_ref, da_ref, gd_ref, stage_ref, *, tn, tiles_per_seq):
    dh = GD_HEAD_DIM

    @pl.when(pl.program_id(0) % tiles_per_seq == 0)
    def _():
        stage_ref[:, 0:SUBLANES, :] = jnp.zeros((stage_ref.shape[0], SUBLANES, dh), F32)

    xb = xb_ref[...]
    tm = xb.shape[0]

    def conv_act(res, h0):
        c = h0 // dh
        stage_ref[c, SUBLANES:SUBLANES + tm, :] = res
        cw = cw_ref[:, h0:h0 + dh]
        conv = res * cw[GD_CONV - 1:GD_CONV]
        for j in range(GD_CONV - 1):
            r0 = SUBLANES - (GD_CONV - 1 - j)
            conv = conv + stage_ref[c, r0:r0 + tm, :] * cw[j:j + 1]
        stage_ref[c, 0:SUBLANES, :] = res[tm - SUBLANES:tm]
        act = conv * jax.nn.sigmoid(conv)
        if h0 < 2 * GD_WIDTH:
            act = act * lax.rsqrt(jnp.sum(act * act, axis=-1, keepdims=True) + 1e-6)
            if h0 < GD_WIDTH:
                act = act * dh ** -0.5
        return act

    off = 0
    for out_ref in (rw_ref, da_ref, gd_ref):
        width = out_ref.shape[-1]
        for c0 in range(0, width, tn):
            c1 = min(c0 + tn, width)
            res = jnp.dot(xb, w_ref[:, off + c0:off + c1], preferred_element_type=F32)
            if out_ref is gd_ref and c0 < 3 * GD_WIDTH:
                for h0 in range(c0, c1, dh):
                    out_ref[:, h0:h0 + dh] = conv_act(res[:, h0 - c0:h0 - c0 + dh], h0)
            else:
                out_ref[:, c0:c1] = res.astype(out_ref.dtype)
        off += width


def _mix_in(xb, w_cat, conv_w, seq_len, *, tm=512, tn=256):
    t, d = xb.shape
    n = w_cat.shape[1]
    assert n == RW_COLS + DA_COLS + GD_IN and t % tm == 0 and seq_len % tm == 0 and tn % GD_HEAD_DIM == 0
    return pl.pallas_call(
        functools.partial(_mix_in_kernel, tn=tn, tiles_per_seq=seq_len // tm),
        grid=(t // tm,),
        in_specs=[pl.BlockSpec((tm, d), lambda i: (i, 0)), _const_spec((d, n)), _const_spec(conv_w.shape)],
        out_specs=[pl.BlockSpec((tm, RW_COLS), lambda i: (i, 0)),
                   pl.BlockSpec((tm, DA_COLS), lambda i: (i, 0)),
                   pl.BlockSpec((tm, GD_IN), lambda i: (i, 0))],
        out_shape=[jax.ShapeDtypeStruct((t, RW_COLS), F32),
                   jax.ShapeDtypeStruct((t, DA_COLS), BF16),
                   jax.ShapeDtypeStruct((t, GD_IN), F32)],
        scratch_shapes=[pltpu.VMEM((3 * GD_HEADS, SUBLANES + tm, GD_HEAD_DIM), F32)],
        compiler_params=_cparams(1),
        name="mix_in",
    )(xb, w_cat, conv_w.astype(F32))


def _rwkv_kernel(h_ref, mu_ref, w0_ref, a0_ref, kk_ref, ka_ref, rk_ref, lng_ref, lnb_ref,
                 lora_ref, gup_ref, bd_ref, tril_ref, y_ref, prev_ref, s_ref, *, nb, nc):
    rows_b = nc * CHUNK

    @pl.when(pl.program_id(1) == 0)
    def _():
        prev_ref[:, 0:SUBLANES, :] = jnp.zeros((nb, SUBLANES, RW_COLS), F32)
        s_ref[...] = jnp.zeros_like(s_ref)

    n_pairs = RW_HEADS // 2
    parts = []
    for b in range(nb):
        h = h_ref[b]
        prev_ref[b, SUBLANES:SUBLANES + rows_b, :] = h
        hprev = prev_ref[b, SUBLANES - 1:SUBLANES - 1 + rows_b, :]
        prev_ref[b, 0:SUBLANES, :] = h[rows_b - SUBLANES:rows_b]
        parts.append(h + mu_ref[...] * (hprev - h))
    hs = jnp.concatenate(parts, axis=0) if nb > 1 else parts[0]
    n_rows = nb * rows_b

    w = RW_WIDTH
    r = hs[:, 0:w]
    k = hs[:, w:2 * w]
    v = hs[:, 2 * w:3 * w]
    wa = hs[:, 3 * w:3 * w + LANES]
    gd = hs[:, 3 * w + LANES:3 * w + 2 * LANES]
    lo_n = _iota((n_rows, LANES), 1) < RW_HEAD_DIM
    lora = _mm(jnp.where(lo_n, jnp.tanh(wa), wa), lora_ref[...])
    logw = -math.exp(-0.5) * jax.nn.sigmoid(w0_ref[...] + lora[:, 0:w])
    a = jax.nn.sigmoid(a0_ref[...] + lora[:, w:2 * w])
    g = _mm(jax.nn.sigmoid(gd), gup_ref[...])

    bd = bd_ref[...]
    kkr = k * kk_ref[...]
    kk = kkr * lax.rsqrt(_split_lhs_dot(kkr * kkr, bd, 1) + 1e-6)
    k2 = k * (1.0 + (a - 1.0) * ka_ref[...])
    bvec = kk * a
    bonus = _split_lhs_dot(r * k2 * rk_ref[...], bd, 1) * v

    e_in = _split_rhs_dot(tril_ref[...], logw, 2)
    rho = _rows_of_chunk(e_in, CHUNK // 2 - 1)
    r_t = r * jnp.exp(e_in - rho)
    kk_t = kk * jnp.exp(e_in - logw - rho)
    e_neg = jnp.exp(rho - e_in)
    k_h = k2 * e_neg
    b_h = bvec * e_neg

    lane = _iota((CHUNK, LANES), 1)
    lo = lane < RW_HEAD_DIM
    tt = _iota((CHUNK, LANES), 0)
    ss = lane & (CHUNK - 1)
    strict = tt > ss
    incl = tt >= ss

    units = [(b, ci, p) for b in range(nb) for ci in range(nc) for p in range(n_pairs)]

    def tile(x, u):
        b, ci, p = u
        r0 = (b * nc + ci) * CHUNK
        return x[r0:r0 + CHUNK, p * LANES:(p + 1) * LANES]

    kks = [_stack_heads(tile(kk_t, u), lo) for u in units]
    rs = [_stack_heads(tile(r_t, u), lo) for u in units]
    vs = [_stack_heads(tile(v, u), lo) for u in units]
    khs = [_stack_heads(tile(k_h, u), lo) for u in units]
    bhs = [_stack_heads(tile(b_h, u), lo) for u in units]
    z = [_mm_nt(jnp.concatenate([tile(kk_t, u), tile(r_t, u)], axis=0), jnp.concatenate([kh, bh], axis=0))
         for u, kh, bh in zip(units, khs, bhs)]
    ak = [_stack_heads(jnp.where(strict, zz[0:CHUNK, 0:PAIR], 0.0), lo) for zz in z]
    ab = [_stack_heads(jnp.where(strict, zz[0:CHUNK, PAIR:2 * PAIR], 0.0), lo) for zz in z]
    bk = [_stack_heads(jnp.where(incl, zz[CHUNK:PAIR, 0:PAIR], 0.0), lo) for zz in z]
    bb = [_stack_heads(jnp.where(incl, zz[CHUNK:PAIR, PAIR:2 * PAIR], 0.0), lo) for zz in z]
    inv = _inv_unit_lower_many(ab)
    akv = [_mm(x, y) for x, y in zip(ak, vs)]
    bkv = [_mm(x, y) for x, y in zip(bk, vs)]
    pq = [_mm(i, jnp.concatenate([x, y], axis=1)) for i, x, y in zip(inv, kks, akv)]
    bpq = [_mm(x, y) for x, y in zip(bb, pq)]
    rp = [x - y[:, 0:PAIR] for x, y in zip(rs, bpq)]
    y0 = [x - y[:, PAIR:2 * PAIR] for x, y in zip(bkv, bpq)]
    ptb = [_mm_tn(x[:, 0:PAIR], y) for x, y in zip(pq, bhs)]
    hm = [_mm_tn(jnp.concatenate([x, y[:, PAIR:2 * PAIR]], axis=0), jnp.concatenate([kh, -bh], axis=0))
          for x, y, kh, bh in zip(vs, pq, khs, bhs)]

    s_scale = jnp.exp(rho)
    d_out = jnp.exp(_rows_of_chunk(e_in, CHUNK - 1) - rho)
    yo = {}
    chains = [(b, p) for b in range(nb) for p in range(n_pairs)]
    state = {bp: s_ref[bp[0] * n_pairs + bp[1]] for bp in chains}
    for ci in range(nc):
        for b, p in chains:
            i = units.index((b, ci, p))
            r0 = (b * nc + ci) * CHUNK
            sl = slice(p * LANES, (p + 1) * LANES)
            sp = state[(b, p)] * s_scale[r0:r0 + 1, sl]
            yo[(b, ci, p)] = _mm_nt(rp[i], sp) + y0[i]
            state[(b, p)] = (sp - _mm(sp, ptb[i]) + hm[i]) * d_out[r0:r0 + 1, sl]
    for b, p in chains:
        s_ref[b * n_pairs + p] = state[(b, p)]
    y = jnp.concatenate(
        [jnp.concatenate([yo[(b, ci, p)][0:CHUNK] + yo[(b, ci, p)][CHUNK:PAIR] for p in range(n_pairs)], axis=1)
         for b in range(nb) for ci in range(nc)], axis=0)

    inv_n = 1.0 / RW_HEAD_DIM
    ym = _split_lhs_dot(y, bd, 1) * inv_n
    yc = y - ym
    yv = _split_lhs_dot(yc * yc, bd, 1) * inv_n
    yn = yc * lax.rsqrt(yv + RW_GN_EPS) * lng_ref[...] + lnb_ref[...]
    out = ((yn + bonus) * g).astype(y_ref.dtype)
    for b in range(nb):
        y_ref[b] = out[b * rows_b:(b + 1) * rows_b]


def _rwkv7(h_rw, mu, w0, w_up, a0, a_up, g_up, k_k, k_a, r_k, ln_g, ln_b, *, nb=SCAN_BATCHES, nc=SCAN_CHUNKS):
    b, l, _ = h_rw.shape
    w = RW_WIDTH
    nb = min(nb, b)
    rows_b = nc * CHUNK
    assert l % rows_b == 0 and b % nb == 0
    lora = jnp.zeros((LANES, 2 * w), F32)
    lora = lora.at[0:RW_DECAY_LORA, 0:w].set(w_up).at[RW_DECAY_LORA:LANES, w:2 * w].set(a_up)
    hid = jnp.arange(w) // RW_HEAD_DIM
    bd = (hid[:, None] == hid[None, :]).astype(BF16)
    vec = lambda t: t.reshape(1, -1).astype(F32)
    params = [vec(mu), vec(w0), vec(a0), vec(k_k), vec(k_a), vec(r_k), vec(ln_g), vec(ln_b),
              lora.astype(BF16), g_up.astype(BF16), bd, _chunk_tril(nb * nc)]
    return pl.pallas_call(
        functools.partial(_rwkv_kernel, nb=nb, nc=nc),
        grid=(b // nb, l // rows_b),
        in_specs=[pl.BlockSpec((nb, rows_b, RW_COLS), lambda i, j: (i, j, 0))]
        + [_const_spec(p.shape) for p in params],
        out_specs=pl.BlockSpec((nb, rows_b, w), lambda i, j: (i, j, 0)),
        out_shape=jax.ShapeDtypeStruct((b, l, w), BF16),
        scratch_shapes=[pltpu.VMEM((nb, SUBLANES + rows_b, RW_COLS), F32),
                        pltpu.VMEM((nb * RW_HEADS // 2, PAIR, PAIR), F32)],
        compiler_params=_cparams(2),
        name="rwkv7",
    )(h_rw, *params)


def _attn_kernel(slope_ref, lam_ref, ng_ref, q_ref, k_ref, v_ref, o_ref, ka_ref, vt_ref, s0_ref, s1_ref, m_ref,
                 acc_ref, *, tq, tk, lam_init):
    d = DA_HEAD_DIM
    dv = 2 * d
    n_blk = vt_ref.shape[0]
    slope2 = slope_ref[0, 0:1, 0:1] * LOG2E

    lane_k = _iota((tk, LANES), 1)
    bias = slope2 * _iota((tk, LANES), 0).astype(F32)
    hi = bias.astype(BF16).astype(F32)
    mid = (bias - hi).astype(BF16).astype(F32)
    low = bias - hi - mid
    pieces = 0.0
    for n, piece in enumerate((hi, mid, low)):
        pieces = jnp.where(lane_k == n, piece, pieces)
    pieces = pieces.astype(BF16)
    ones_row = (_iota((DA_ONES_ROWS, tk), 0) == 0).astype(BF16)
    for jb in range(n_blk):
        rows = slice(jb * tk, (jb + 1) * tk)
        ka_ref[rows, 0:LANES] = k_ref[0, rows, :]
        ka_ref[rows, LANES:2 * LANES] = pieces
        vt_ref[jb, 0:dv, :] = v_ref[0, rows, :].astype(F32).T.astype(BF16)
        vt_ref[jb, dv:dv + DA_ONES_ROWS, :] = ones_row

    lp = lam_ref[...]
    lam = (jnp.exp(jnp.sum(lp[0:1] * lp[1:2], axis=-1, keepdims=True))
           - jnp.exp(jnp.sum(lp[2:3] * lp[3:4], axis=-1, keepdims=True)) + lam_init)
    lane_q = _iota((tq, LANES), 1)
    lo = lane_q < d
    bias_ones = (lane_q < DA_BIAS_LANES).astype(BF16)

    def q_operand(qi):
        q = q_ref[0, qi * tq:(qi + 1) * tq, :].astype(F32) * (d ** -0.5 * LOG2E)
        return jnp.concatenate([
            jnp.concatenate([jnp.where(lo, q, 0.0).astype(BF16), bias_ones], axis=1),
            jnp.concatenate([jnp.where(lo, 0.0, q).astype(BF16), bias_ones], axis=1)], axis=0)

    def scores(q2, j, dst_ref):
        dst_ref[...] = lax.dot_general(ka_ref[j * tk:(j + 1) * tk, :], q2, (((1,), (1,)), ((), ())),
                                       preferred_element_type=F32)

    def consume(qi, j, src_ref):
        shift = slope2 * float(j * tk - qi * tq)
        for c0 in range(0, 2 * tq, DA_QUERY_GROUP):
            cols = slice(c0, c0 + DA_QUERY_GROUP)
            nk = min(tk, c0 % tq + DA_QUERY_GROUP) if j == qi else tk
            s = src_ref[0:nk, cols]
            if j == qi:
                key = _iota((nk, DA_QUERY_GROUP), 0)
                query = (_iota((nk, DA_QUERY_GROUP), 1) + c0) & (tq - 1)
                s = jnp.where(key <= query, s, -jnp.inf)
            vt = vt_ref[j, :, 0:nk]
            if j == 0:
                m_new = jnp.max(s, axis=0, keepdims=True) + shift
                p = jnp.exp2(s - (m_new - shift)).astype(BF16)
                acc_ref[:, cols] = jnp.dot(vt, p, preferred_element_type=F32)
            else:
                m_old = m_ref[:, cols]
                m_new = jnp.maximum(m_old, jnp.max(s, axis=0, keepdims=True) + shift)
                p = jnp.exp2(s - (m_new - shift)).astype(BF16)
                acc_ref[:, cols] = (jnp.exp2(m_old - m_new) * acc_ref[:, cols]
                                    + jnp.dot(vt, p, preferred_element_type=F32))
            m_ref[:, cols] = m_new

    def finalize(qi):
        acc = acc_ref[...]
        ot = acc[0:dv] / acc[dv:dv + 1]
        ot = ot[:, 0:tq] - lam * ot[:, tq:2 * tq]
        ot = ot * lax.rsqrt(jnp.mean(ot * ot, axis=0, keepdims=True) + 1e-5)
        o_ref[0, qi * tq:(qi + 1) * tq, :] = (ot.T * ng_ref[...] * (1.0 - lam_init)).astype(o_ref.dtype)

    pairs = [(qi, j) for qi in range(n_blk) for j in range(qi + 1)]
    bufs = (s0_ref, s1_ref)
    q2 = {0: q_operand(0)}
    scores(q2[0], 0, bufs[0])
    for idx, (qi, j) in enumerate(pairs):
        if idx + 1 < len(pairs):
            nqi, nj = pairs[idx + 1]
            if nqi not in q2:
                q2[nqi] = q_operand(nqi)
            scores(q2[nqi], nj, bufs[(idx + 1) % 2])
        consume(qi, j, bufs[idx % 2])
        if j == qi:
            finalize(qi)


def _diff_attention(h_da, lam_q1, lam_k1, lam_q2, lam_k2, norm_g, lam_init, *, tq=512):
    b, l, _ = h_da.shape
    tq = min(tq, l)
    tk = tq
    assert l % tq == 0 and tq & (tq - 1) == 0
    hh = DA_HEADS
    slopes = jnp.exp2(-8.0 * jnp.arange(1, hh + 1, dtype=F32) / hh)
    slopes = jnp.broadcast_to(slopes[:, None, None], (hh, SUBLANES, LANES))
    lam_p = jnp.stack([lam_q1, lam_k1, lam_q2, lam_k2]).astype(F32)
    ng = norm_g.reshape(1, 2 * DA_HEAD_DIM).astype(F32)
    seq = lambda col0: pl.BlockSpec((1, l, LANES), lambda i, h: (i, 0, col0 + h))
    return pl.pallas_call(
        functools.partial(_attn_kernel, tq=tq, tk=tk, lam_init=lam_init),
        grid=(b, hh),
        in_specs=[pl.BlockSpec((1, SUBLANES, LANES), lambda i, h: (h, 0, 0)),
                  pl.BlockSpec(lam_p.shape, lambda i, h: (0, 0)),
                  pl.BlockSpec(ng.shape, lambda i, h: (0, 0)),
                  seq(0), seq(hh), seq(2 * hh)],
        out_specs=seq(0),
        out_shape=jax.ShapeDtypeStruct((b, l, DA_WIDTH), BF16),
        scratch_shapes=[pltpu.VMEM((l, 2 * LANES), BF16),
                        pltpu.VMEM((l // tk, LANES + DA_ONES_ROWS, tk), BF16),
                        pltpu.VMEM((tk, 2 * tq), F32), pltpu.VMEM((tk, 2 * tq), F32),
                        pltpu.VMEM((1, 2 * tq), F32), pltpu.VMEM((LANES + DA_ONES_ROWS, 2 * tq), F32)],
        compiler_params=_cparams(2),
        name="diffattn",
    )(slopes, lam_p, ng, h_da, h_da, h_da)


def _softplus(x):
    return jnp.maximum(x, 0.0) + jnp.log(1.0 + jnp.exp(-jnp.abs(x)))


def _gdn_kernel(h_ref, alog_ref, dtb_ref, ng_ref, tril_ref, y_ref, s_ref, *, nb, nc):
    @pl.when(pl.program_id(1) == 0)
    def _():
        s_ref[...] = jnp.zeros_like(s_ref)

    wq = 3 * GD_WIDTH
    dh = GD_HEAD_DIM
    qkv = jnp.concatenate([h_ref[b, :, 0:wq] for b in range(nb)], axis=0)
    ab = jnp.concatenate([h_ref[b, :, wq:wq + GD_AB_PAD] for b in range(nb)], axis=0)
    gfull = -jnp.exp(alog_ref[...]) * _softplus(ab + dtb_ref[...])
    gcum = _split_rhs_dot(tril_ref[...], gfull, 3)
    glast = _rows_of_chunk(gcum, CHUNK - 1)
    beta_full = jax.nn.sigmoid(ab)

    qn = [qkv[:, h * dh:(h + 1) * dh] for h in range(GD_HEADS)]
    kn = [qkv[:, GD_WIDTH + h * dh:GD_WIDTH + (h + 1) * dh] for h in range(GD_HEADS)]
    vh = [qkv[:, 2 * GD_WIDTH + h * dh:2 * GD_WIDTH + (h + 1) * dh] for h in range(GD_HEADS)]

    r0 = _iota((PAIR, PAIR), 0)
    c0 = _iota((PAIR, PAIR), 1)
    same = (r0 >= CHUNK) == (c0 >= CHUNK)
    incl = same & (r0 >= c0)
    strict = same & (r0 > c0)

    n_pairs = GD_HEADS // 2
    units = [(b, ci, p) for b in range(nb) for ci in range(nc) for p in range(n_pairs)]

    def stack(per_head, u, lane_of=None):
        b, ci, p = u
        rr = slice((b * nc + ci) * CHUNK, (b * nc + ci + 1) * CHUNK)
        if lane_of is None:
            return jnp.concatenate([per_head[2 * p][rr], per_head[2 * p + 1][rr]], axis=0)
        return jnp.concatenate([per_head[rr, lane_of + 2 * p:lane_of + 2 * p + 1],
                                per_head[rr, lane_of + 2 * p + 1:lane_of + 2 * p + 2]], axis=0)

    q_s = [stack(qn, u) for u in units]
    k_s = [stack(kn, u) for u in units]
    v_s = [stack(vh, u) for u in units]
    g_s = [stack(gcum, u, 0) for u in units]
    gl_s = [stack(glast, u, 0) for u in units]
    beta_s = [stack(beta_full, u, GD_HEADS) for u in units]

    decay = []
    for gs in g_s:
        g_b = jnp.broadcast_to(gs, (PAIR, PAIR))
        decay.append(jnp.exp(jnp.where(incl, g_b - g_b.T, -jnp.inf)))
    kb = [x * y for x, y in zip(k_s, beta_s)]
    kk = [_mm_nt(jnp.concatenate([x, y], axis=0), z) for x, y, z in zip(kb, q_s, k_s)]
    low = [jnp.where(strict, x[0:PAIR] * dc, 0.0) for x, dc in zip(kk, decay)]
    intra = [x[PAIR:2 * PAIR] * dc for x, dc in zip(kk, decay)]
    inv = _inv_unit_lower_many(low)
    eg = [jnp.exp(gs) for gs in g_s]
    uw = [_mm(i, jnp.concatenate([v * bt, x * e], axis=1)) for i, v, bt, x, e in zip(inv, v_s, beta_s, kb, eg)]
    qg = [x * e for x, e in zip(q_s, eg)]
    kd = [x * jnp.exp(gl - gs) for x, gl, gs in zip(k_s, gl_s, g_s)]

    v_new = {}
    o_state = {}
    chains = [(b, h) for b in range(nb) for h in range(GD_HEADS)]
    state = {bh: s_ref[bh[0] * GD_HEADS + bh[1]] for bh in chains}
    for ci in range(nc):
        ws = {}
        for b, h in chains:
            p, i_h = divmod(h, 2)
            rows = slice(i_h * CHUNK, (i_h + 1) * CHUNK)
            i = units.index((b, ci, p))
            ws[(b, h)] = _mm(jnp.concatenate([uw[i][rows, dh:2 * dh], qg[i][rows]], axis=0), state[(b, h)])
        for b, h in chains:
            p, i_h = divmod(h, 2)
            rows = slice(i_h * CHUNK, (i_h + 1) * CHUNK)
            i = units.index((b, ci, p))
            vn = uw[i][rows, 0:dh] - ws[(b, h)][0:CHUNK]
            v_new[(b, ci, h)] = vn
            o_state[(b, ci, h)] = ws[(b, h)][CHUNK:PAIR]
            state[(b, h)] = state[(b, h)] * jnp.exp(gl_s[i][rows][0:1]) + _mm_tn(kd[i][rows], vn)
    for b, h in chains:
        s_ref[b * GD_HEADS + h] = state[(b, h)]

    for i, (b, ci, p) in enumerate(units):
        vn = jnp.concatenate([v_new[(b, ci, 2 * p)], v_new[(b, ci, 2 * p + 1)]], axis=0)
        o = jnp.concatenate([o_state[(b, ci, 2 * p)], o_state[(b, ci, 2 * p + 1)]], axis=0) + _mm(intra[i], vn)
        for i_h in range(2):
            h = 2 * p + i_h
            oh = o[i_h * CHUNK:(i_h + 1) * CHUNK]
            oh = oh * lax.rsqrt(jnp.mean(oh * oh, axis=-1, keepdims=True) + 1e-6) * ng_ref[...]
            zc = wq + GD_AB_PAD + h * dh
            z = h_ref[b, ci * CHUNK:(ci + 1) * CHUNK, zc:zc + dh]
            y_ref[b, ci * CHUNK:(ci + 1) * CHUNK, h * dh:(h + 1) * dh] = (
                oh * (z * jax.nn.sigmoid(z))).astype(y_ref.dtype)


def _gated_deltanet(h_gd, a_log, dt_bias, norm_g, *, nb=2 * SCAN_BATCHES, nc=SCAN_CHUNKS):
    b, l, _ = h_gd.shape
    nb = min(nb, b)
    rows_b = nc * CHUNK
    assert l % rows_b == 0 and b % nb == 0
    pad = lambda t: jnp.zeros((1, GD_AB_PAD), F32).at[0, 0:GD_HEADS].set(t.astype(F32))
    params = [pad(a_log), pad(dt_bias), norm_g.reshape(1, GD_HEAD_DIM).astype(F32), _chunk_tril(nb * nc)]
    return pl.pallas_call(
        functools.partial(_gdn_kernel, nb=nb, nc=nc),
        grid=(b // nb, l // rows_b),
        in_specs=[pl.BlockSpec((nb, rows_b, GD_IN), lambda i, j: (i, j, 0))]
        + [_const_spec(p.shape) for p in params],
        out_specs=pl.BlockSpec((nb, rows_b, GD_WIDTH), lambda i, j: (i, j, 0)),
        out_shape=jax.ShapeDtypeStruct((b, l, GD_WIDTH), BF16),
        scratch_shapes=[pltpu.VMEM((nb * GD_HEADS, GD_HEAD_DIM, GD_HEAD_DIM), F32)],
        compiler_params=_cparams(2),
        name="gdn",
    )(h_gd, *params)


def _merge_ln_kernel(x_ref, xb_ref, ya_ref, yb_ref, yc_ref, wgate_ref, wbr_ref, wout_ref, g_ref, b_ref,
                     y_ref, y16_ref):
    for r0 in range(0, x_ref.shape[0], MERGE_ROWS):
        rows = slice(r0, r0 + MERGE_ROWS)
        xb = xb_ref[rows, :]
        merged = None
        for n, br_ref in enumerate((ya_ref, yb_ref, yc_ref)):
            gate = jax.nn.sigmoid(jnp.dot(xb, wgate_ref[n], preferred_element_type=F32))
            term = gate * jnp.dot(br_ref[rows, :], wbr_ref[n], preferred_element_type=F32)
            merged = term if merged is None else merged + term
        mix = jnp.dot(merged.astype(BF16), wout_ref[...], preferred_element_type=F32)
        y = _layernorm(ALPHA * x_ref[rows, :] + mix, g_ref[...], b_ref[...])
        y_ref[rows, :] = y
        y16_ref[rows, :] = y.astype(BF16)


def _merge_ln(x, xb, ya, yb, yc, w_gate, w_branch, w_out, g, b, *, tm=2 * MERGE_ROWS):
    t, d = x.shape
    bw = ya.shape[-1]
    assert t % tm == 0 and tm % MERGE_ROWS == 0
    row = pl.BlockSpec((tm, d), lambda i: (i, 0))
    brow = pl.BlockSpec((tm, bw), lambda i: (i, 0))
    return pl.pallas_call(
        _merge_ln_kernel,
        grid=(t // tm,),
        in_specs=[row, row, brow, brow, brow, _const_spec((N_BRANCH, d, d)),
                  _const_spec((N_BRANCH, bw, d)), _const_spec((d, d)), _const_spec((1, d)), _const_spec((1, d))],
        out_specs=[row, row],
        out_shape=[jax.ShapeDtypeStruct((t, d), F32), jax.ShapeDtypeStruct((t, d), BF16)],
        compiler_params=_cparams(1),
        name="merge_ln",
    )(x, xb, ya, yb, yc, w_gate, w_branch, w_out, g.reshape(1, d), b.reshape(1, d))


def _split_mix_w_in(w_in):
    d = w_in.shape[0]
    o1 = RW_COLS
    o2 = o1 + DA_COLS
    o3 = o2 + GD_COLS
    gq = o2 + 3 * GD_WIDTH
    ab = jnp.zeros((d, GD_AB_PAD), w_in.dtype).at[:, 0:2 * GD_HEADS].set(w_in[:, gq:gq + 2 * GD_HEADS])
    w_cat = jnp.concatenate([w_in[:, 0:o2], w_in[:, o2:gq], ab, w_in[:, gq + 2 * GD_HEADS:o3]], axis=1)
    w_gate = w_in[:, o3:].reshape(d, N_BRANCH, d).transpose(1, 0, 2)
    return w_cat.astype(BF16), w_gate.astype(BF16)


def kernel(x, ffn1_w_in, ffn1_w_out, ln1_g, ln1_b, mix_w_in, rw_shift_mu, rw_w0, rw_w_up, rw_a0, rw_a_up, rw_g_up, rw_k_k, rw_k_a, rw_r_k, rw_ln_g, rw_ln_b, da_lam_q1, da_lam_k1, da_lam_q2, da_lam_k2, da_norm_g, gd_conv_w, gd_a_log, gd_dt_bias, gd_norm_g, mix_w_branch, mix_w_out, ln2_g, ln2_b, ffn2_w_in, ffn2_w_out, ln3_g, ln3_b):
    b, l, d = x.shape
    t = b * l
    xf = x.reshape(t, d)
    xb = None
    for i in range(DEPTH):
        xf, xb = _ffn_ln(xf, xb, ffn1_w_in[i], ffn1_w_out[i], ln1_g[i], ln1_b[i])
        w_cat, w_gate = _split_mix_w_in(mix_w_in[i])
        h_rw, h_da, h_gd = _mix_in(xb, w_cat, gd_conv_w[i], l)
        ya = _rwkv7(h_rw.reshape(b, l, RW_COLS), rw_shift_mu[i], rw_w0[i], rw_w_up[i], rw_a0[i], rw_a_up[i],
                    rw_g_up[i], rw_k_k[i], rw_k_a[i], rw_r_k[i].reshape(-1), rw_ln_g[i], rw_ln_b[i])
        lam_init = 0.8 - 0.6 * math.exp(-0.3 * i)
        yb = _diff_attention(h_da.reshape(b, l, DA_COLS), da_lam_q1[i], da_lam_k1[i], da_lam_q2[i],
                             da_lam_k2[i], da_norm_g[i], lam_init)
        yc = _gated_deltanet(h_gd.reshape(b, l, GD_IN), gd_a_log[i], gd_dt_bias[i], gd_norm_g[i])
        xf, xb = _merge_ln(xf, xb, ya.reshape(t, RW_WIDTH), yb.reshape(t, DA_WIDTH), yc.reshape(t, GD_WIDTH),
                           w_gate, mix_w_branch[i].astype(BF16), mix_w_out[i].astype(BF16), ln2_g[i], ln2_b[i])
        xf, xb = _ffn_ln(xf, xb, ffn2_w_in[i], ffn2_w_out[i], ln3_g[i], ln3_b[i])
    return xf.reshape(b, l, d)
```

```python
import functools
import math

import jax
import jax.numpy as jnp
from jax import lax
from jax.experimental import pallas as pl
from jax.experimental.pallas import tpu as pltpu

F32 = jnp.float32
BF16 = jnp.bfloat16

D_MODEL = 1024
DEPTH = 2
D_FF = 2816
RW_HEADS = 8
RW_HEAD_DIM = 64
RW_WIDTH = 512
RW_DECAY_LORA = 64
RW_ICLR_LORA = 64
RW_GATE_LORA = 128
RW_COLS = 3 * RW_WIDTH + RW_DECAY_LORA + RW_ICLR_LORA + RW_GATE_LORA
RW_GN_EPS = 64e-5
DA_HEADS = 4
DA_HEAD_DIM = 64
DA_WIDTH = 512
DA_COLS = 3 * DA_WIDTH
GD_HEADS = 4
GD_HEAD_DIM = 128
GD_WIDTH = 512
GD_CONV = 4
GD_COLS = 3 * GD_WIDTH + 2 * GD_HEADS + GD_WIDTH
N_BRANCH = 3
ALPHA = (2.0 * DEPTH) ** 0.25

LANES = 128
SUBLANES = 8
CHUNK = 64
PAIR = 2 * CHUNK
DA_BIAS_LANES = 3
DA_QUERY_GROUP = 512
MERGE_ROWS = 512
FFN_ROWS = 512
DA_ONES_ROWS = 16
LOG2E = math.log2(math.e)
GD_AB_PAD = LANES
GD_IN = 3 * GD_WIDTH + GD_AB_PAD + GD_WIDTH
VMEM_LIMIT = 56 * 1024 * 1024
SCAN_BATCHES = 2
SCAN_CHUNKS = 2


def _cparams(n_grid):
    return pltpu.CompilerParams(dimension_semantics=("arbitrary",) * n_grid,
                                vmem_limit_bytes=VMEM_LIMIT)


def _const_spec(shape):
    nd = len(shape)
    return pl.BlockSpec(shape, lambda *_: (0,) * nd, pipeline_mode=pl.Buffered(1))


def _mm(a, b):
    return jnp.dot(a.astype(BF16), b.astype(BF16), preferred_element_type=F32)


def _mm_nt(a, b):
    return lax.dot_general(a.astype(BF16), b.astype(BF16), (((1,), (1,)), ((), ())),
                           preferred_element_type=F32)


def _mm_tn(a, b):
    return lax.dot_general(a.astype(BF16), b.astype(BF16), (((0,), (0,)), ((), ())),
                           preferred_element_type=F32)


def _split_lhs_dot(x, exact_rhs, terms):
    acc = None
    rem = x
    for _ in range(terms):
        hi = rem.astype(BF16)
        part = jnp.dot(hi, exact_rhs, preferred_element_type=F32)
        acc = part if acc is None else acc + part
        rem = rem - hi.astype(F32)
    return acc


def _split_rhs_dot(exact_lhs, x, terms):
    acc = None
    rem = x
    for _ in range(terms):
        hi = rem.astype(BF16)
        part = jnp.dot(exact_lhs, hi, preferred_element_type=F32)
        acc = part if acc is None else acc + part
        rem = rem - hi.astype(F32)
    return acc


def _layernorm(z, g, b, eps=1e-5):
    mu = jnp.mean(z, axis=-1, keepdims=True)
    zc = z - mu
    var = jnp.mean(zc * zc, axis=-1, keepdims=True)
    return zc * lax.rsqrt(var + eps) * g + b


def _iota(shape, dim):
    return lax.broadcasted_iota(jnp.int32, shape, dim)


def _inv_unit_lower_many(lows):
    eye = (_iota((PAIR, PAIR), 0) == _iota((PAIR, PAIR), 1)).astype(F32)
    pw = [-low for low in lows]
    inv = [eye + n for n in pw]
    for _ in range(int(math.log2(CHUNK)) - 1):
        pw = [_mm(p, p) for p in pw]
        inv = [i + _mm(i, p) for i, p in zip(inv, pw)]
    return inv


def _stack_heads(x, lo_mask):
    return jnp.concatenate([jnp.where(lo_mask, x, 0.0), jnp.where(lo_mask, 0.0, x)], axis=0)


def _chunk_tril(n_chunks):
    r = jnp.arange(n_chunks * CHUNK)
    return ((r[:, None] >= r[None, :]) & (r[:, None] // CHUNK == r[None, :] // CHUNK)).astype(BF16)


def _rows_of_chunk(x, row):
    n = x.shape[0] // CHUNK
    parts = [jnp.broadcast_to(x[i * CHUNK + row:i * CHUNK + row + 1], (CHUNK,) + x.shape[1:]) for i in range(n)]
    return jnp.concatenate(parts, axis=0) if n > 1 else parts[0]


def _ffn_ln_kernel(x_ref, *refs, tf):
    if len(refs) == 8:
        xb_ref, win_ref, wo_ref, g_ref, b_ref, y_ref, yb_ref, acc_ref = refs
    else:
        xb_ref = None
        win_ref, wo_ref, g_ref, b_ref, y_ref, yb_ref, acc_ref = refs
    dff = wo_ref.shape[0]
    nf = dff // tf

    for r0 in range(0, x_ref.shape[0], FFN_ROWS):
        rows = slice(r0, r0 + FFN_ROWS)
        xb = xb_ref[rows, :] if xb_ref is not None else x_ref[rows, :].astype(BF16)

        def gate_up(f):
            return (jnp.dot(xb, win_ref[:, f * tf:(f + 1) * tf], preferred_element_type=F32),
                    jnp.dot(xb, win_ref[:, dff + f * tf:dff + (f + 1) * tf], preferred_element_type=F32))

        nxt = gate_up(0)
        for f in range(nf):
            gate, up = nxt
            if f + 1 < nf:
                nxt = gate_up(f + 1)
            act = (gate * jax.nn.sigmoid(gate) * up).astype(BF16)
            part = jnp.dot(act, wo_ref[f * tf:(f + 1) * tf, :], preferred_element_type=F32)
            if f == 0:
                acc_ref[rows, :] = part
            else:
                acc_ref[rows, :] += part
        y = _layernorm(ALPHA * x_ref[rows, :] + 0.5 * acc_ref[rows, :], g_ref[...], b_ref[...])
        y_ref[rows, :] = y
        yb_ref[rows, :] = y.astype(BF16)


def _ffn_ln(x, xb, w_in, w_out, g, b, *, tm=1024, tf=256):
    t, d = x.shape
    dff = w_out.shape[0]
    assert dff % tf == 0 and t % tm == 0 and tm % FFN_ROWS == 0
    row = pl.BlockSpec((tm, d), lambda i: (i, 0))
    acts = [x] if xb is None else [x, xb]
    return pl.pallas_call(
        functools.partial(_ffn_ln_kernel, tf=tf),
        grid=(t // tm,),
        in_specs=[row] * len(acts) + [_const_spec((d, 2 * dff)), _const_spec((dff, d)), _const_spec((1, d)),
                                      _const_spec((1, d))],
        out_specs=[row, row],
        out_shape=[jax.ShapeDtypeStruct((t, d), F32), jax.ShapeDtypeStruct((t, d), BF16)],
        scratch_shapes=[pltpu.VMEM((tm, d), F32)],
        compiler_params=_cparams(1),
        name="ffn_ln",
    )(*acts, w_in.astype(BF16), w_out.astype(BF16), g.reshape(1, d), b.reshape(1, d))


def _mix_in_kernel(xb_ref, w_ref, cw_ref, rw_ref, da_ref, gd_ref, stage_ref, *, tn, tiles_per_seq):
    dh = GD_HEAD_DIM

    @pl.when(pl.program_id(0) % tiles_per_seq == 0)
    def _():
        stage_ref[:, 0:SUBLANES, :] = jnp.zeros((stage_ref.shape[0], SUBLANES, dh), F32)

    xb = xb_ref[...]
    tm = xb.shape[0]

    def conv_act(res, h0):
        c = h0 // dh
        stage_ref[c, SUBLANES:SUBLANES + tm, :] = res
        cw = cw_ref[:, h0:h0 + dh]
        conv = res * cw[GD_CONV - 1:GD_CONV]
        for j in range(GD_CONV - 1):
            r0 = SUBLANES - (GD_CONV - 1 - j)
            conv = conv + stage_ref[c, r0:r0 + tm, :] * cw[j:j + 1]
        stage_ref[c, 0:SUBLANES, :] = res[tm - SUBLANES:tm]
        act = conv * jax.nn.sigmoid(conv)
        if h0 < 2 * GD_WIDTH:
            act = act * lax.rsqrt(jnp.sum(act * act, axis=-1, keepdims=True) + 1e-6)
            if h0 < GD_WIDTH:
                act = act * dh ** -0.5
        return act

    off = 0
    for out_ref in (rw_ref, da_ref, gd_ref):
        width = out_ref.shape[-1]
        for c0 in range(0, width, tn):
            c1 = min(c0 + tn, width)
            res = jnp.dot(xb, w_ref[:, off + c0:off + c1], preferred_element_type=F32)
            if out_ref is gd_ref and c0 < 3 * GD_WIDTH:
                for h0 in range(c0, c1, dh):
                    out_ref[:, h0:h0 + dh] = conv_act(res[:, h0 - c0:h0 - c0 + dh], h0)
            else:
                out_ref[:, c0:c1] = res.astype(out_ref.dtype)
        off += width


def _mix_in(xb, w_cat, conv_w, seq_len, *, tm=512, tn=256):
    t, d = xb.shape
    n = w_cat.shape[1]
    assert n == RW_COLS + DA_COLS + GD_IN and t % tm == 0 and seq_len % tm == 0 and tn % GD_HEAD_DIM == 0
    return pl.pallas_call(
        functools.partial(_mix_in_kernel, tn=tn, tiles_per_seq=seq_len // tm),
        grid=(t // tm,),
        in_specs=[pl.BlockSpec((tm, d), lambda i: (i, 0)), _const_spec((d, n)), _const_spec(conv_w.shape)],
        out_specs=[pl.BlockSpec((tm, RW_COLS), lambda i: (i, 0)),
                   pl.BlockSpec((tm, DA_COLS), lambda i: (i, 0)),
                   pl.BlockSpec((tm, GD_IN), lambda i: (i, 0))],
        out_shape=[jax.ShapeDtypeStruct((t, RW_COLS), F32),
                   jax.ShapeDtypeStruct((t, DA_COLS), BF16),
                   jax.ShapeDtypeStruct((t, GD_IN), F32)],
        scratch_shapes=[pltpu.VMEM((3 * GD_HEADS, SUBLANES + tm, GD_HEAD_DIM), F32)],
        compiler_params=_cparams(1),
        name="mix_in",
    )(xb, w_cat, conv_w.astype(F32))


def _rwkv_kernel(h_ref, mu_ref, w0_ref, a0_ref, kk_ref, ka_ref, rk_ref, lng_ref, lnb_ref,
                 lora_ref, gup_ref, bd_ref, tril_ref, y_ref, prev_ref, s_ref, *, nb, nc):
    rows_b = nc * CHUNK

    @pl.when(pl.program_id(1) == 0)
    def _():
        prev_ref[:, 0:SUBLANES, :] = jnp.zeros((nb, SUBLANES, RW_COLS), F32)
        s_ref[...] = jnp.zeros_like(s_ref)

    n_pairs = RW_HEADS // 2
    parts = []
    for b in range(nb):
        h = h_ref[b]
        prev_ref[b, SUBLANES:SUBLANES + rows_b, :] = h
        hprev = prev_ref[b, SUBLANES - 1:SUBLANES - 1 + rows_b, :]
        prev_ref[b, 0:SUBLANES, :] = h[rows_b - SUBLANES:rows_b]
        parts.append(h + mu_ref[...] * (hprev - h))
    hs = jnp.concatenate(parts, axis=0) if nb > 1 else parts[0]
    n_rows = nb * rows_b

    w = RW_WIDTH
    r = hs[:, 0:w]
    k = hs[:, w:2 * w]
    v = hs[:, 2 * w:3 * w]
    wa = hs[:, 3 * w:3 * w + LANES]
    gd = hs[:, 3 * w + LANES:3 * w + 2 * LANES]
    lo_n = _iota((n_rows, LANES), 1) < RW_HEAD_DIM
    lora = _mm(jnp.where(lo_n, jnp.tanh(wa), wa), lora_ref[...])
    logw = -math.exp(-0.5) * jax.nn.sigmoid(w0_ref[...] + lora[:, 0:w])
    a = jax.nn.sigmoid(a0_ref[...] + lora[:, w:2 * w])
    g = _mm(jax.nn.sigmoid(gd), gup_ref[...])

    bd = bd_ref[...]
    kkr = k * kk_ref[...]
    kk = kkr * lax.rsqrt(_split_lhs_dot(kkr * kkr, bd, 1) + 1e-6)
    k2 = k * (1.0 + (a - 1.0) * ka_ref[...])
    bvec = kk * a
    bonus = _split_lhs_dot(r * k2 * rk_ref[...], bd, 1) * v

    e_in = _split_rhs_dot(tril_ref[...], logw, 2)
    rho = _rows_of_chunk(e_in, CHUNK // 2 - 1)
    r_t = r * jnp.exp(e_in - rho)
    kk_t = kk * jnp.exp(e_in - logw - rho)
    e_neg = jnp.exp(rho - e_in)
    k_h = k2 * e_neg
    b_h = bvec * e_neg

    lane = _iota((CHUNK, LANES), 1)
    lo = lane < RW_HEAD_DIM
    tt = _iota((CHUNK, LANES), 0)
    ss = lane & (CHUNK - 1)
    strict = tt > ss
    incl = tt >= ss

    units = [(b, ci, p) for b in range(nb) for ci in range(nc) for p in range(n_pairs)]

    def tile(x, u):
        b, ci, p = u
        r0 = (b * nc + ci) * CHUNK
        return x[r0:r0 + CHUNK, p * LANES:(p + 1) * LANES]

    kks = [_stack_heads(tile(kk_t, u), lo) for u in units]
    rs = [_stack_heads(tile(r_t, u), lo) for u in units]
    vs = [_stack_heads(tile(v, u), lo) for u in units]
    khs = [_stack_heads(tile(k_h, u), lo) for u in units]
    bhs = [_stack_heads(tile(b_h, u), lo) for u in units]
    z = [_mm_nt(jnp.concatenate([tile(kk_t, u), tile(r_t, u)], axis=0), jnp.concatenate([kh, bh], axis=0))
         for u, kh, bh in zip(units, khs, bhs)]
    ak = [_stack_heads(jnp.where(strict, zz[0:CHUNK, 0:PAIR], 0.0), lo) for zz in z]
    ab = [_stack_heads(jnp.where(strict, zz[0:CHUNK, PAIR:2 * PAIR], 0.0), lo) for zz in z]
    bk = [_stack_heads(jnp.where(incl, zz[CHUNK:PAIR, 0:PAIR], 0.0), lo) for zz in z]
    bb = [_stack_heads(jnp.where(incl, zz[CHUNK:PAIR, PAIR:2 * PAIR], 0.0), lo) for zz in z]
    inv = _inv_unit_lower_many(ab)
    akv = [_mm(x, y) for x, y in zip(ak, vs)]
    bkv = [_mm(x, y) for x, y in zip(bk, vs)]
    pq = [_mm(i, jnp.concatenate([x, y], axis=1)) for i, x, y in zip(inv, kks, akv)]
    bpq = [_mm(x, y) for x, y in zip(bb, pq)]
    rp = [x - y[:, 0:PAIR] for x, y in zip(rs, bpq)]
    y0 = [x - y[:, PAIR:2 * PAIR] for x, y in zip(bkv, bpq)]
    ptb = [_mm_tn(x[:, 0:PAIR], y) for x, y in zip(pq, bhs)]
    hm = [_mm_tn(jnp.concatenate([x, y[:, PAIR:2 * PAIR]], axis=0), jnp.concatenate([kh, -bh], axis=0))
          for x, y, kh, bh in zip(vs, pq, khs, bhs)]

    s_scale = jnp.exp(rho)
    d_out = jnp.exp(_rows_of_chunk(e_in, CHUNK - 1) - rho)
    yo = {}
    chains = [(b, p) for b in range(nb) for p in range(n_pairs)]
    state = {bp: s_ref[bp[0] * n_pairs + bp[1]] for bp in chains}
    for ci in range(nc):
        for b, p in chains:
            i = units.index((b, ci, p))
            r0 = (b * nc + ci) * CHUNK
            sl = slice(p * LANES, (p + 1) * LANES)
            sp = state[(b, p)] * s_scale[r0:r0 + 1, sl]
            yo[(b, ci, p)] = _mm_nt(rp[i], sp) + y0[i]
            state[(b, p)] = (sp - _mm(sp, ptb[i]) + hm[i]) * d_out[r0:r0 + 1, sl]
    for b, p in chains:
        s_ref[b * n_pairs + p] = state[(b, p)]
    y = jnp.concatenate(
        [jnp.concatenate([yo[(b, ci, p)][0:CHUNK] + yo[(b, ci, p)][CHUNK:PAIR] for p in range(n_pairs)], axis=1)
         for b in range(nb) for ci in range(nc)], axis=0)

    inv_n = 1.0 / RW_HEAD_DIM
    ym = _split_lhs_dot(y, bd, 1) * inv_n
    yc = y - ym
    yv = _split_lhs_dot(yc * yc, bd, 1) * inv_n
    yn = yc * lax.rsqrt(yv + RW_GN_EPS) * lng_ref[...] + lnb_ref[...]
    out = ((yn + bonus) * g).astype(y_ref.dtype)
    for b in range(nb):
        y_ref[b] = out[b * rows_b:(b + 1) * rows_b]


def _rwkv7(h_rw, mu, w0, w_up, a0, a_up, g_up, k_k, k_a, r_k, ln_g, ln_b, *, nb=SCAN_BATCHES, nc=SCAN_CHUNKS):
    b, l, _ = h_rw.shape
    w = RW_WIDTH
    nb = min(nb, b)
    rows_b = nc * CHUNK
    assert l % rows_b == 0 and b % nb == 0
    lora = jnp.zeros((LANES, 2 * w), F32)
    lora = lora.at[0:RW_DECAY_LORA, 0:w].set(w_up).at[RW_DECAY_LORA:LANES, w:2 * w].set(a_up)
    hid = jnp.arange(w) // RW_HEAD_DIM
    bd = (hid[:, None] == hid[None, :]).astype(BF16)
    vec = lambda t: t.reshape(1, -1).astype(F32)
    params = [vec(mu), vec(w0), vec(a0), vec(k_k), vec(k_a), vec(r_k), vec(ln_g), vec(ln_b),
              lora.astype(BF16), g_up.astype(BF16), bd, _chunk_tril(nb * nc)]
    return pl.pallas_call(
        functools.partial(_rwkv_kernel, nb=nb, nc=nc),
        grid=(b // nb, l // rows_b),
        in_specs=[pl.BlockSpec((nb, rows_b, RW_COLS), lambda i, j: (i, j, 0))]
        + [_const_spec(p.shape) for p in params],
        out_specs=pl.BlockSpec((nb, rows_b, w), lambda i, j: (i, j, 0)),
        out_shape=jax.ShapeDtypeStruct((b, l, w), BF16),
        scratch_shapes=[pltpu.VMEM((nb, SUBLANES + rows_b, RW_COLS), F32),
                        pltpu.VMEM((nb * RW_HEADS // 2, PAIR, PAIR), F32)],
        compiler_params=_cparams(2),
        name="rwkv7",
    )(h_rw, *params)


def _attn_kernel(slope_ref, lam_ref, ng_ref, q_ref, k_ref, v_ref, o_ref, ka_ref, vt_ref, s0_ref, s1_ref, m_ref,
                 acc_ref, *, tq, tk, lam_init):
    d = DA_HEAD_DIM
    dv = 2 * d
    n_blk = vt_ref.shape[0]
    slope2 = slope_ref[0, 0:1, 0:1] * LOG2E

    lane_k = _iota((tk, LANES), 1)
    bias = slope2 * _iota((tk, LANES), 0).astype(F32)
    hi = bias.astype(BF16).astype(F32)
    mid = (bias - hi).astype(BF16).astype(F32)
    low = bias - hi - mid
    pieces = 0.0
    for n, piece in enumerate((hi, mid, low)):
        pieces = jnp.where(lane_k == n, piece, pieces)
    pieces = pieces.astype(BF16)
    ones_row = (_iota((DA_ONES_ROWS, tk), 0) == 0).astype(BF16)
    for jb in range(n_blk):
        rows = slice(jb * tk, (jb + 1) * tk)
        ka_ref[rows, 0:LANES] = k_ref[0, rows, :]
        ka_ref[rows, LANES:2 * LANES] = pieces
        vt_ref[jb, 0:dv, :] = v_ref[0, rows, :].astype(F32).T.astype(BF16)
        vt_ref[jb, dv:dv + DA_ONES_ROWS, :] = ones_row

    lp = lam_ref[...]
    lam = (jnp.exp(jnp.sum(lp[0:1] * lp[1:2], axis=-1, keepdims=True))
           - jnp.exp(jnp.sum(lp[2:3] * lp[3:4], axis=-1, keepdims=True)) + lam_init)
    lane_q = _iota((tq, LANES), 1)
    lo = lane_q < d
    bias_ones = (lane_q < DA_BIAS_LANES).astype(BF16)

    def q_operand(qi):
        q = q_ref[0, qi * tq:(qi + 1) * tq, :].astype(F32) * (d ** -0.5 * LOG2E)
        return jnp.concatenate([
            jnp.concatenate([jnp.where(lo, q, 0.0).astype(BF16), bias_ones], axis=1),
            jnp.concatenate([jnp.where(lo, 0.0, q).astype(BF16), bias_ones], axis=1)], axis=0)

    def scores(q2, j, dst_ref):
        dst_ref[...] = lax.dot_general(ka_ref[j * tk:(j + 1) * tk, :], q2, (((1,), (1,)), ((), ())),
                                       preferred_element_type=F32)

    def consume(qi, j, src_ref):
        shift = slope2 * float(j * tk - qi * tq)
        for c0 in range(0, 2 * tq, DA_QUERY_GROUP):
            cols = slice(c0, c0 + DA_QUERY_GROUP)
            nk = min(tk, c0 % tq + DA_QUERY_GROUP) if j == qi else tk
            s = src_ref[0:nk, cols]
            if j == qi:
                key = _iota((nk, DA_QUERY_GROUP), 0)
                query = (_iota((nk, DA_QUERY_GROUP), 1) + c0) & (tq - 1)
                s = jnp.where(key <= query, s, -jnp.inf)
            vt = vt_ref[j, :, 0:nk]
            if j == 0:
                m_new = jnp.max(s, axis=0, keepdims=True) + shift
                p = jnp.exp2(s - (m_new - shift)).astype(BF16)
                acc_ref[:, cols] = jnp.dot(vt, p, preferred_element_type=F32)
            else:
                m_old = m_ref[:, cols]
                m_new = jnp.maximum(m_old, jnp.max(s, axis=0, keepdims=True) + shift)
                p = jnp.exp2(s - (m_new - shift)).astype(BF16)
                acc_ref[:, cols] = (jnp.exp2(m_old - m_new) * acc_ref[:, cols]
                                    + jnp.dot(vt, p, preferred_element_type=F32))
            m_ref[:, cols] = m_new

    def finalize(qi):
        acc = acc_ref[...]
        ot = acc[0:dv] / acc[dv:dv + 1]
        ot = ot[:, 0:tq] - lam * ot[:, tq:2 * tq]
        ot = ot * lax.rsqrt(jnp.mean(ot * ot, axis=0, keepdims=True) + 1e-5)
        o_ref[0, qi * tq:(qi + 1) * tq, :] = (ot.T * ng_ref[...] * (1.0 - lam_init)).astype(o_ref.dtype)

    pairs = [(qi, j) for qi in range(n_blk) for j in range(qi + 1)]
    bufs = (s0_ref, s1_ref)
    q2 = {0: q_operand(0)}
    scores(q2[0], 0, bufs[0])
    for idx, (qi, j) in enumerate(pairs):
        if idx + 1 < len(pairs):
            nqi, nj = pairs[idx + 1]
            if nqi not in q2:
                q2[nqi] = q_operand(nqi)
            scores(q2[nqi], nj, bufs[(idx + 1) % 2])
        consume(qi, j, bufs[idx % 2])
        if j == qi:
            finalize(qi)


def _diff_attention(h_da, lam_q1, lam_k1, lam_q2, lam_k2, norm_g, lam_init, *, tq=512):
    b, l, _ = h_da.shape
    tq = min(tq, l)
    tk = tq
    assert l % tq == 0 and tq & (tq - 1) == 0
    hh = DA_HEADS
    slopes = jnp.exp2(-8.0 * jnp.arange(1, hh + 1, dtype=F32) / hh)
    slopes = jnp.broadcast_to(slopes[:, None, None], (hh, SUBLANES, LANES))
    lam_p = jnp.stack([lam_q1, lam_k1, lam_q2, lam_k2]).astype(F32)
    ng = norm_g.reshape(1, 2 * DA_HEAD_DIM).astype(F32)
    seq = lambda col0: pl.BlockSpec((1, l, LANES), lambda i, h: (i, 0, col0 + h))
    return pl.pallas_call(
        functools.partial(_attn_kernel, tq=tq, tk=tk, lam_init=lam_init),
        grid=(b, hh),
        in_specs=[pl.BlockSpec((1, SUBLANES, LANES), lambda i, h: (h, 0, 0)),
                  pl.BlockSpec(lam_p.shape, lambda i, h: (0, 0)),
                  pl.BlockSpec(ng.shape, lambda i, h: (0, 0)),
                  seq(0), seq(hh), seq(2 * hh)],
        out_specs=seq(0),
        out_shape=jax.ShapeDtypeStruct((b, l, DA_WIDTH), BF16),
        scratch_shapes=[pltpu.VMEM((l, 2 * LANES), BF16),
                        pltpu.VMEM((l // tk, LANES + DA_ONES_ROWS, tk), BF16),
                        pltpu.VMEM((tk, 2 * tq), F32), pltpu.VMEM((tk, 2 * tq), F32),
                        pltpu.VMEM((1, 2 * tq), F32), pltpu.VMEM((LANES + DA_ONES_ROWS, 2 * tq), F32)],
        compiler_params=_cparams(2),
        name="diffattn",
    )(slopes, lam_p, ng, h_da, h_da, h_da)


def _softplus(x):
    return jnp.maximum(x, 0.0) + jnp.log(1.0 + jnp.exp(-jnp.abs(x)))


def _gdn_kernel(h_ref, alog_ref, dtb_ref, ng_ref, tril_ref, y_ref, s_ref, *, nb, nc):
    @pl.when(pl.program_id(1) == 0)
    def _():
        s_ref[...] = jnp.zeros_like(s_ref)

    wq = 3 * GD_WIDTH
    dh = GD_HEAD_DIM
    qkv = jnp.concatenate([h_ref[b, :, 0:wq] for b in range(nb)], axis=0)
    ab = jnp.concatenate([h_ref[b, :, wq:wq + GD_AB_PAD] for b in range(nb)], axis=0)
    gfull = -jnp.exp(alog_ref[...]) * _softplus(ab + dtb_ref[...])
    gcum = _split_rhs_dot(tril_ref[...], gfull, 3)
    glast = _rows_of_chunk(gcum, CHUNK - 1)
    beta_full = jax.nn.sigmoid(ab)

    qn = [qkv[:, h * dh:(h + 1) * dh] for h in range(GD_HEADS)]
    kn = [qkv[:, GD_WIDTH + h * dh:GD_WIDTH + (h + 1) * dh] for h in range(GD_HEADS)]
    vh = [qkv[:, 2 * GD_WIDTH + h * dh:2 * GD_WIDTH + (h + 1) * dh] for h in range(GD_HEADS)]

    r0 = _iota((PAIR, PAIR), 0)
    c0 = _iota((PAIR, PAIR), 1)
    same = (r0 >= CHUNK) == (c0 >= CHUNK)
    incl = same & (r0 >= c0)
    strict = same & (r0 > c0)

    n_pairs = GD_HEADS // 2
    units = [(b, ci, p) for b in range(nb) for ci in range(nc) for p in range(n_pairs)]

    def stack(per_head, u, lane_of=None):
        b, ci, p = u
        rr = slice((b * nc + ci) * CHUNK, (b * nc + ci + 1) * CHUNK)
        if lane_of is None:
            return jnp.concatenate([per_head[2 * p][rr], per_head[2 * p + 1][rr]], axis=0)
        return jnp.concatenate([per_head[rr, lane_of + 2 * p:lane_of + 2 * p + 1],
                                per_head[rr, lane_of + 2 * p + 1:lane_of + 2 * p + 2]], axis=0)

    q_s = [stack(qn, u) for u in units]
    k_s = [stack(kn, u) for u in units]
    v_s = [stack(vh, u) for u in units]
    g_s = [stack(gcum, u, 0) for u in units]
    gl_s = [stack(glast, u, 0) for u in units]
    beta_s = [stack(beta_full, u, GD_HEADS) for u in units]

    decay = []
    for gs in g_s:
        g_b = jnp.broadcast_to(gs, (PAIR, PAIR))
        decay.append(jnp.exp(jnp.where(incl, g_b - g_b.T, -jnp.inf)))
    kb = [x * y for x, y in zip(k_s, beta_s)]
    kk = [_mm_nt(jnp.concatenate([x, y], axis=0), z) for x, y, z in zip(kb, q_s, k_s)]
    low = [jnp.where(strict, x[0:PAIR] * dc, 0.0) for x, dc in zip(kk, decay)]
    intra = [x[PAIR:2 * PAIR] * dc for x, dc in zip(kk, decay)]
    inv = _inv_unit_lower_many(low)
    eg = [jnp.exp(gs) for gs in g_s]
    uw = [_mm(i, jnp.concatenate([v * bt, x * e], axis=1)) for i, v, bt, x, e in zip(inv, v_s, beta_s, kb, eg)]
    qg = [x * e for x, e in zip(q_s, eg)]
    kd = [x * jnp.exp(gl - gs) for x, gl, gs in zip(k_s, gl_s, g_s)]

    v_new = {}
    o_state = {}
    chains = [(b, h) for b in range(nb) for h in range(GD_HEADS)]
    state = {bh: s_ref[bh[0] * GD_HEADS + bh[1]] for bh in chains}
    for ci in range(nc):
        ws = {}
        for b, h in chains:
            p, i_h = divmod(h, 2)
            rows = slice(i_h * CHUNK, (i_h + 1) * CHUNK)
            i = units.index((b, ci, p))
            ws[(b, h)] = _mm(jnp.concatenate([uw[i][rows, dh:2 * dh], qg[i][rows]], axis=0), state[(b, h)])
        for b, h in chains:
            p, i_h = divmod(h, 2)
            rows = slice(i_h * CHUNK, (i_h + 1) * CHUNK)
            i = units.index((b, ci, p))
            vn = uw[i][rows, 0:dh] - ws[(b, h)][0:CHUNK]
            v_new[(b, ci, h)] = vn
            o_state[(b, ci, h)] = ws[(b, h)][CHUNK:PAIR]
            state[(b, h)] = state[(b, h)] * jnp.exp(gl_s[i][rows][0:1]) + _mm_tn(kd[i][rows], vn)
    for b, h in chains:
        s_ref[b * GD_HEADS + h] = state[(b, h)]

    for i, (b, ci, p) in enumerate(units):
        vn = jnp.concatenate([v_new[(b, ci, 2 * p)], v_new[(b, ci, 2 * p + 1)]], axis=0)
        o = jnp.concatenate([o_state[(b, ci, 2 * p)], o_state[(b, ci, 2 * p + 1)]], axis=0) + _mm(intra[i], vn)
        for i_h in range(2):
            h = 2 * p + i_h
            oh = o[i_h * CHUNK:(i_h + 1) * CHUNK]
            oh = oh * lax.rsqrt(jnp.mean(oh * oh, axis=-1, keepdims=True) + 1e-6) * ng_ref[...]
            zc = wq + GD_AB_PAD + h * dh
            z = h_ref[b, ci * CHUNK:(ci + 1) * CHUNK, zc:zc + dh]
            y_ref[b, ci * CHUNK:(ci + 1) * CHUNK, h * dh:(h + 1) * dh] = (
                oh * (z * jax.nn.sigmoid(z))).astype(y_ref.dtype)


def _gated_deltanet(h_gd, a_log, dt_bias, norm_g, *, nb=2 * SCAN_BATCHES, nc=SCAN_CHUNKS):
    b, l, _ = h_gd.shape
    nb = min(nb, b)
    rows_b = nc * CHUNK
    assert l % rows_b == 0 and b % nb == 0
    pad = lambda t: jnp.zeros((1, GD_AB_PAD), F32).at[0, 0:GD_HEADS].set(t.astype(F32))
    params = [pad(a_log), pad(dt_bias), norm_g.reshape(1, GD_HEAD_DIM).astype(F32), _chunk_tril(nb * nc)]
    return pl.pallas_call(
        functools.partial(_gdn_kernel, nb=nb, nc=nc),
        grid=(b // nb, l // rows_b),
        in_specs=[pl.BlockSpec((nb, rows_b, GD_IN), lambda i, j: (i, j, 0))]
        + [_const_spec(p.shape) for p in params],
        out_specs=pl.BlockSpec((nb, rows_b, GD_WIDTH), lambda i, j: (i, j, 0)),
        out_shape=jax.ShapeDtypeStruct((b, l, GD_WIDTH), BF16),
        scratch_shapes=[pltpu.VMEM((nb * GD_HEADS, GD_HEAD_DIM, GD_HEAD_DIM), F32)],
        compiler_params=_cparams(2),
        name="gdn",
    )(h_gd, *params)


def _merge_ln_kernel(x_ref, xb_ref, ya_ref, yb_ref, yc_ref, wgate_ref, wbr_ref, wout_ref, g_ref, b_ref,
                     y_ref, y16_ref):
    for r0 in range(0, x_ref.shape[0], MERGE_ROWS):
        rows = slice(r0, r0 + MERGE_ROWS)
        xb = xb_ref[rows, :]
        merged = None
        for n, br_ref in enumerate((ya_ref, yb_ref, yc_ref)):
            gate = jax.nn.sigmoid(jnp.dot(xb, wgate_ref[n], preferred_element_type=F32))
            term = gate * jnp.dot(br_ref[rows, :], wbr_ref[n], preferred_element_type=F32)
            merged = term if merged is None else merged + term
        mix = jnp.dot(merged.astype(BF16), wout_ref[...], preferred_element_type=F32)
        y = _layernorm(ALPHA * x_ref[rows, :] + mix, g_ref[...], b_ref[...])
        y_ref[rows, :] = y
        y16_ref[rows, :] = y.astype(BF16)


def _merge_ln(x, xb, ya, yb, yc, w_gate, w_branch, w_out, g, b, *, tm=2 * MERGE_ROWS):
    t, d = x.shape
    bw = ya.shape[-1]
    assert t % tm == 0 and tm % MERGE_ROWS == 0
    row = pl.BlockSpec((tm, d), lambda i: (i, 0))
    brow = pl.BlockSpec((tm, bw), lambda i: (i, 0))
    return pl.pallas_call(
        _merge_ln_kernel,
        grid=(t // tm,),
        in_specs=[row, row, brow, brow, brow, _const_spec((N_BRANCH, d, d)),
                  _const_spec((N_BRANCH, bw, d)), _const_spec((d, d)), _const_spec((1, d)), _const_spec((1, d))],
        out_specs=[row, row],
        out_shape=[jax.ShapeDtypeStruct((t, d), F32), jax.ShapeDtypeStruct((t, d), BF16)],
        compiler_params=_cparams(1),
        name="merge_ln",
    )(x, xb, ya, yb, yc, w_gate, w_branch, w_out, g.reshape(1, d), b.reshape(1, d))


def _split_mix_w_in(w_in):
    d = w_in.shape[0]
    o1 = RW_COLS
    o2 = o1 + DA_COLS
    o3 = o2 + GD_COLS
    gq = o2 + 3 * GD_WIDTH
    ab = jnp.zeros((d, GD_AB_PAD), w_in.dtype).at[:, 0:2 * GD_HEADS].set(w_in[:, gq:gq + 2 * GD_HEADS])
    w_cat = jnp.concatenate([w_in[:, 0:o2], w_in[:, o2:gq], ab, w_in[:, gq + 2 * GD_HEADS:o3]], axis=1)
    w_gate = w_in[:, o3:].reshape(d, N_BRANCH, d).transpose(1, 0, 2)
    return w_cat.astype(BF16), w_gate.astype(BF16)


def kernel(x, ffn1_w_in, ffn1_w_out, ln1_g, ln1_b, mix_w_in, rw_shift_mu, rw_w0, rw_w_up, rw_a0, rw_a_up, rw_g_up, rw_k_k, rw_k_a, rw_r_k, rw_ln_g, rw_ln_b, da_lam_q1, da_lam_k1, da_lam_q2, da_lam_k2, da_norm_g, gd_conv_w, gd_a_log, gd_dt_bias, gd_norm_g, mix_w_branch, mix_w_out, ln2_g, ln2_b, ffn2_w_in, ffn2_w_out, ln3_g, ln3_b):
    b, l, d = x.shape
    t = b * l
    xf = x.reshape(t, d)
    xb = None
    for i in range(DEPTH):
        xf, xb = _ffn_ln(xf, xb, ffn1_w_in[i], ffn1_w_out[i], ln1_g[i], ln1_b[i])
        w_cat, w_gate = _split_mix_w_in(mix_w_in[i])
        h_rw, h_da, h_gd = _mix_in(xb, w_cat, gd_conv_w[i], l)
        ya = _rwkv7(h_rw.reshape(b, l, RW_COLS), rw_shift_mu[i], rw_w0[i], rw_w_up[i], rw_a0[i], rw_a_up[i],
                    rw_g_up[i], rw_k_k[i], rw_k_a[i], rw_r_k[i].reshape(-1), rw_ln_g[i], rw_ln_b[i])
        lam_init = 0.8 - 0.6 * math.exp(-0.3 * i)
        yb = _diff_attention(h_da.reshape(b, l, DA_COLS), da_lam_q1[i], da_lam_k1[i], da_lam_q2[i],
                             da_lam_k2[i], da_norm_g[i], lam_init)
        yc = _gated_deltanet(h_gd.reshape(b, l, GD_IN), gd_a_log[i], gd_dt_bias[i], gd_norm_g[i])
        xf, xb = _merge_ln(xf, xb, ya.reshape(t, RW_WIDTH), yb.reshape(t, DA_WIDTH), yc.reshape(t, GD_WIDTH),
                           w_gate, mix_w_branch[i].astype(BF16), mix_w_out[i].astype(BF16), ln2_g[i], ln2_b[i])
        xf, xb = _ffn_ln(xf, xb, ffn2_w_in[i], ffn2_w_out[i], ln3_g[i], ln3_b[i])
    return xf.reshape(b, l, d)
```

```python
import functools
import math

import jax
import jax.numpy as jnp
from jax import lax
from jax.experimental import pallas as pl
from jax.experimental.pallas import tpu as pltpu

F32 = jnp.float32
BF16 = jnp.bfloat16

D_MODEL = 1024
DEPTH = 2
D_FF = 2816
RW_HEADS = 8
RW_HEAD_DIM = 64
RW_WIDTH = 512
RW_DECAY_LORA = 64
RW_ICLR_LORA = 64
RW_GATE_LORA = 128
RW_COLS = 3 * RW_WIDTH + RW_DECAY_LORA + RW_ICLR_LORA + RW_GATE_LORA
RW_GN_EPS = 64e-5
DA_HEADS = 4
DA_HEAD_DIM = 64
DA_WIDTH = 512
DA_COLS = 3 * DA_WIDTH
GD_HEADS = 4
GD_HEAD_DIM = 128
GD_WIDTH = 512
GD_CONV = 4
GD_COLS = 3 * GD_WIDTH + 2 * GD_HEADS + GD_WIDTH
N_BRANCH = 3
ALPHA = (2.0 * DEPTH) ** 0.25

LANES = 128
SUBLANES = 8
CHUNK = 64
PAIR = 2 * CHUNK
DA_BIAS_LANES = 3
DA_QUERY_GROUP = 512
MERGE_ROWS = 256
FFN_ROWS = 512
DA_ONES_ROWS = 16
LOG2E = math.log2(math.e)
GD_AB_PAD = LANES
GD_IN = 3 * GD_WIDTH + GD_AB_PAD + GD_WIDTH
VMEM_LIMIT = 56 * 1024 * 1024
SCAN_BATCHES = 2
SCAN_CHUNKS = 2


def _cparams(n_grid):
    return pltpu.CompilerParams(dimension_semantics=("arbitrary",) * n_grid,
                                vmem_limit_bytes=VMEM_LIMIT)


def _const_spec(shape):
    nd = len(shape)
    return pl.BlockSpec(shape, lambda *_: (0,) * nd, pipeline_mode=pl.Buffered(1))


def _mm(a, b):
    return jnp.dot(a.astype(BF16), b.astype(BF16), preferred_element_type=F32)


def _mm_nt(a, b):
    return lax.dot_general(a.astype(BF16), b.astype(BF16), (((1,), (1,)), ((), ())),
                           preferred_element_type=F32)


def _mm_tn(a, b):
    return lax.dot_general(a.astype(BF16), b.astype(BF16), (((0,), (0,)), ((), ())),
                           preferred_element_type=F32)


def _split_lhs_dot(x, exact_rhs, terms):
    acc = None
    rem = x
    for _ in range(terms):
        hi = rem.astype(BF16)
        part = jnp.dot(hi, exact_rhs, preferred_element_type=F32)
        acc = part if acc is None else acc + part
        rem = rem - hi.astype(F32)
    return acc


def _split_rhs_dot(exact_lhs, x, terms):
    acc = None
    rem = x
    for _ in range(terms):
        hi = rem.astype(BF16)
        part = jnp.dot(exact_lhs, hi, preferred_element_type=F32)
        acc = part if acc is None else acc + part
        rem = rem - hi.astype(F32)
    return acc


def _layernorm(z, g, b, eps=1e-5):
    mu = jnp.mean(z, axis=-1, keepdims=True)
    zc = z - mu
    var = jnp.mean(zc * zc, axis=-1, keepdims=True)
    return zc * lax.rsqrt(var + eps) * g + b


def _iota(shape, dim):
    return lax.broadcasted_iota(jnp.int32, shape, dim)


def _inv_unit_lower_many(lows):
    eye = (_iota((PAIR, PAIR), 0) == _iota((PAIR, PAIR), 1)).astype(F32)
    pw = [-low for low in lows]
    inv = [eye + n for n in pw]
    for _ in range(int(math.log2(CHUNK)) - 1):
        pw = [_mm(p, p) for p in pw]
        inv = [i + _mm(i, p) for i, p in zip(inv, pw)]
    return inv


def _stack_heads(x, lo_mask):
    return jnp.concatenate([jnp.where(lo_mask, x, 0.0), jnp.where(lo_mask, 0.0, x)], axis=0)


def _chunk_tril(n_chunks):
    r = jnp.arange(n_chunks * CHUNK)
    return ((r[:, None] >= r[None, :]) & (r[:, None] // CHUNK == r[None, :] // CHUNK)).astype(BF16)


def _rows_of_chunk(x, row):
    n = x.shape[0] // CHUNK
    parts = [jnp.broadcast_to(x[i * CHUNK + row:i * CHUNK + row + 1], (CHUNK,) + x.shape[1:]) for i in range(n)]
    return jnp.concatenate(parts, axis=0) if n > 1 else parts[0]


def _ffn_ln_kernel(x_ref, *refs, tf):
    if len(refs) == 8:
        xb_ref, win_ref, wo_ref, g_ref, b_ref, y_ref, yb_ref, acc_ref = refs
    else:
        xb_ref = None
        win_ref, wo_ref, g_ref, b_ref, y_ref, yb_ref, acc_ref = refs
    dff = wo_ref.shape[0]
    nf = dff // tf

    for r0 in range(0, x_ref.shape[0], FFN_ROWS):
        rows = slice(r0, r0 + FFN_ROWS)
        xb = xb_ref[rows, :] if xb_ref is not None else x_ref[rows, :].astype(BF16)

        def gate_up(f):
            return (jnp.dot(xb, win_ref[:, f * tf:(f + 1) * tf], preferred_element_type=F32),
                    jnp.dot(xb, win_ref[:, dff + f * tf:dff + (f + 1) * tf], preferred_element_type=F32))

        nxt = gate_up(0)
        for f in range(nf):
            gate, up = nxt
            if f + 1 < nf:
                nxt = gate_up(f + 1)
            act = (gate * jax.nn.sigmoid(gate) * up).astype(BF16)
            part = jnp.dot(act, wo_ref[f * tf:(f + 1) * tf, :], preferred_element_type=F32)
            if f == 0:
                acc_ref[rows, :] = part
            else:
                acc_ref[rows, :] += part
        y = _layernorm(ALPHA * x_ref[rows, :] + 0.5 * acc_ref[rows, :], g_ref[...], b_ref[...])
        y_ref[rows, :] = y
        yb_ref[rows, :] = y.astype(BF16)


def _ffn_ln(x, xb, w_in, w_out, g, b, *, tm=1024, tf=256):
    t, d = x.shape
    dff = w_out.shape[0]
    assert dff % tf == 0 and t % tm == 0 and tm % FFN_ROWS == 0
    row = pl.BlockSpec((tm, d), lambda i: (i, 0))
    acts = [x] if xb is None else [x, xb]
    return pl.pallas_call(
        functools.partial(_ffn_ln_kernel, tf=tf),
        grid=(t // tm,),
        in_specs=[row] * len(acts) + [_const_spec((d, 2 * dff)), _const_spec((dff, d)), _const_spec((1, d)),
                                      _const_spec((1, d))],
        out_specs=[row, row],
        out_shape=[jax.ShapeDtypeStruct((t, d), F32), jax.ShapeDtypeStruct((t, d), BF16)],
        scratch_shapes=[pltpu.VMEM((tm, d), F32)],
        compiler_params=_cparams(1),
        name="ffn_ln",
    )(*acts, w_in.astype(BF16), w_out.astype(BF16), g.reshape(1, d), b.reshape(1, d))


def _mix_in_kernel(xb_ref, w_ref, cw_ref, rw_ref, da_ref, gd_ref, stage_ref, *, tn, tiles_per_seq):
    dh = GD_HEAD_DIM

    @pl.when(pl.program_id(0) % tiles_per_seq == 0)
    def _():
        stage_ref[:, 0:SUBLANES, :] = jnp.zeros((stage_ref.shape[0], SUBLANES, dh), F32)

    xb = xb_ref[...]
    tm = xb.shape[0]

    def conv_act(res, h0):
        c = h0 // dh
        stage_ref[c, SUBLANES:SUBLANES + tm, :] = res
        cw = cw_ref[:, h0:h0 + dh]
        conv = res * cw[GD_CONV - 1:GD_CONV]
        for j in range(GD_CONV - 1):
            r0 = SUBLANES - (GD_CONV - 1 - j)
            conv = conv + stage_ref[c, r0:r0 + tm, :] * cw[j:j + 1]
        stage_ref[c, 0:SUBLANES, :] = res[tm - SUBLANES:tm]
        act = conv * jax.nn.sigmoid(conv)
        if h0 < 2 * GD_WIDTH:
            act = act * lax.rsqrt(jnp.sum(act * act, axis=-1, keepdims=True) + 1e-6)
            if h0 < GD_WIDTH:
                act = act * dh ** -0.5
        return act

    off = 0
    for out_ref in (rw_ref, da_ref, gd_ref):
        width = out_ref.shape[-1]
        for c0 in range(0, width, tn):
            c1 = min(c0 + tn, width)
            res = jnp.dot(xb, w_ref[:, off + c0:off + c1], preferred_element_type=F32)
            if out_ref is gd_ref and c0 < 3 * GD_WIDTH:
                for h0 in range(c0, c1, dh):
                    out_ref[:, h0:h0 + dh] = conv_act(res[:, h0 - c0:h0 - c0 + dh], h0)
            else:
                out_ref[:, c0:c1] = res.astype(out_ref.dtype)
        off += width


def _mix_in(xb, w_cat, conv_w, seq_len, *, tm=512, tn=256):
    t, d = xb.shape
    n = w_cat.shape[1]
    assert n == RW_COLS + DA_COLS + GD_IN and t % tm == 0 and seq_len % tm == 0 and tn % GD_HEAD_DIM == 0
    return pl.pallas_call(
        functools.partial(_mix_in_kernel, tn=tn, tiles_per_seq=seq_len // tm),
        grid=(t // tm,),
        in_specs=[pl.BlockSpec((tm, d), lambda i: (i, 0)), _const_spec((d, n)), _const_spec(conv_w.shape)],
        out_specs=[pl.BlockSpec((tm, RW_COLS), lambda i: (i, 0)),
                   pl.BlockSpec((tm, DA_COLS), lambda i: (i, 0)),
                   pl.BlockSpec((tm, GD_IN), lambda i: (i, 0))],
        out_shape=[jax.ShapeDtypeStruct((t, RW_COLS), F32),
                   jax.ShapeDtypeStruct((t, DA_COLS), BF16),
                   jax.ShapeDtypeStruct((t, GD_IN), F32)],
        scratch_shapes=[pltpu.VMEM((3 * GD_HEADS, SUBLANES + tm, GD_HEAD_DIM), F32)],
        compiler_params=_cparams(1),
        name="mix_in",
    )(xb, w_cat, conv_w.astype(F32))


def _rwkv_kernel(h_ref, mu_ref, w0_ref, a0_ref, kk_ref, ka_ref, rk_ref, lng_ref, lnb_ref,
                 lora_ref, gup_ref, bd_ref, tril_ref, y_ref, prev_ref, s_ref, *, nb, nc):
    rows_b = nc * CHUNK

    @pl.when(pl.program_id(1) == 0)
    def _():
        prev_ref[:, 0:SUBLANES, :] = jnp.zeros((nb, SUBLANES, RW_COLS), F32)
        s_ref[...] = jnp.zeros_like(s_ref)

    n_pairs = RW_HEADS // 2
    parts = []
    for b in range(nb):
        h = h_ref[b]
        prev_ref[b, SUBLANES:SUBLANES + rows_b, :] = h
        hprev = prev_ref[b, SUBLANES - 1:SUBLANES - 1 + rows_b, :]
        prev_ref[b, 0:SUBLANES, :] = h[rows_b - SUBLANES:rows_b]
        parts.append(h + mu_ref[...] * (hprev - h))
    hs = jnp.concatenate(parts, axis=0) if nb > 1 else parts[0]
    n_rows = nb * rows_b

    w = RW_WIDTH
    r = hs[:, 0:w]
    k = hs[:, w:2 * w]
    v = hs[:, 2 * w:3 * w]
    wa = hs[:, 3 * w:3 * w + LANES]
    gd = hs[:, 3 * w + LANES:3 * w + 2 * LANES]
    lo_n = _iota((n_rows, LANES), 1) < RW_HEAD_DIM
    lora = _mm(jnp.where(lo_n, jnp.tanh(wa), wa), lora_ref[...])
    logw = -math.exp(-0.5) * jax.nn.sigmoid(w0_ref[...] + lora[:, 0:w])
    a = jax.nn.sigmoid(a0_ref[...] + lora[:, w:2 * w])
    g = _mm(jax.nn.sigmoid(gd), gup_ref[...])

    bd = bd_ref[...]
    kkr = k * kk_ref[...]
    kk = kkr * lax.rsqrt(_split_lhs_dot(kkr * kkr, bd, 1) + 1e-6)
    k2 = k * (1.0 + (a - 1.0) * ka_ref[...])
    bvec = kk * a
    bonus = _split_lhs_dot(r * k2 * rk_ref[...], bd, 1) * v

    e_in = _split_rhs_dot(tril_ref[...], logw, 2)
    rho = _rows_of_chunk(e_in, CHUNK // 2 - 1)
    r_t = r * jnp.exp(e_in - rho)
    kk_t = kk * jnp.exp(e_in - logw - rho)
    e_neg = jnp.exp(rho - e_in)
    k_h = k2 * e_neg
    b_h = bvec * e_neg

    lane = _iota((CHUNK, LANES), 1)
    lo = lane < RW_HEAD_DIM
    tt = _iota((CHUNK, LANES), 0)
    ss = lane & (CHUNK - 1)
    strict = tt > ss
    incl = tt >= ss

    units = [(b, ci, p) for b in range(nb) for ci in range(nc) for p in range(n_pairs)]

    def tile(x, u):
        b, ci, p = u
        r0 = (b * nc + ci) * CHUNK
        return x[r0:r0 + CHUNK, p * LANES:(p + 1) * LANES]

    kks = [_stack_heads(tile(kk_t, u), lo) for u in units]
    rs = [_stack_heads(tile(r_t, u), lo) for u in units]
    vs = [_stack_heads(tile(v, u), lo) for u in units]
    khs = [_stack_heads(tile(k_h, u), lo) for u in units]
    bhs = [_stack_heads(tile(b_h, u), lo) for u in units]
    z = [_mm_nt(jnp.concatenate([tile(kk_t, u), tile(r_t, u)], axis=0), jnp.concatenate([kh, bh], axis=0))
         for u, kh, bh in zip(units, khs, bhs)]
    ak = [_stack_heads(jnp.where(strict, zz[0:CHUNK, 0:PAIR], 0.0), lo) for zz in z]
    ab = [_stack_heads(jnp.where(strict, zz[0:CHUNK, PAIR:2 * PAIR], 0.0), lo) for zz in z]
    bk = [_stack_heads(jnp.where(incl, zz[CHUNK:PAIR, 0:PAIR], 0.0), lo) for zz in z]
    bb = [_stack_heads(jnp.where(incl, zz[CHUNK:PAIR, PAIR:2 * PAIR], 0.0), lo) for zz in z]
    inv = _inv_unit_lower_many(ab)
    akv = [_mm(x, y) for x, y in zip(ak, vs)]
    bkv = [_mm(x, y) for x, y in zip(bk, vs)]
    pq = [_mm(i, jnp.concatenate([x, y], axis=1)) for i, x, y in zip(inv, kks, akv)]
    bpq = [_mm(x, y) for x, y in zip(bb, pq)]
    rp = [x - y[:, 0:PAIR] for x, y in zip(rs, bpq)]
    y0 = [x - y[:, PAIR:2 * PAIR] for x, y in zip(bkv, bpq)]
    ptb = [_mm_tn(x[:, 0:PAIR], y) for x, y in zip(pq, bhs)]
    hm = [_mm_tn(jnp.concatenate([x, y[:, PAIR:2 * PAIR]], axis=0), jnp.concatenate([kh, -bh], axis=0))
          for x, y, kh, bh in zip(vs, pq, khs, bhs)]

    s_scale = jnp.exp(rho)
    d_out = jnp.exp(_rows_of_chunk(e_in, CHUNK - 1) - rho)
    yo = {}
    chains = [(b, p) for b in range(nb) for p in range(n_pairs)]
    state = {bp: s_ref[bp[0] * n_pairs + bp[1]] for bp in chains}
    for ci in range(nc):
        for b, p in chains:
            i = units.index((b, ci, p))
            r0 = (b * nc + ci) * CHUNK
            sl = slice(p * LANES, (p + 1) * LANES)
            sp = state[(b, p)] * s_scale[r0:r0 + 1, sl]
            yo[(b, ci, p)] = _mm_nt(rp[i], sp) + y0[i]
            state[(b, p)] = (sp - _mm(sp, ptb[i]) + hm[i]) * d_out[r0:r0 + 1, sl]
    for b, p in chains:
        s_ref[b * n_pairs + p] = state[(b, p)]
    y = jnp.concatenate(
        [jnp.concatenate([yo[(b, ci, p)][0:CHUNK] + yo[(b, ci, p)][CHUNK:PAIR] for p in range(n_pairs)], axis=1)
         for b in range(nb) for ci in range(nc)], axis=0)

    inv_n = 1.0 / RW_HEAD_DIM
    ym = _split_lhs_dot(y, bd, 1) * inv_n
    yc = y - ym
    yv = _split_lhs_dot(yc * yc, bd, 1) * inv_n
    yn = yc * lax.rsqrt(yv + RW_GN_EPS) * lng_ref[...] + lnb_ref[...]
    out = ((yn + bonus) * g).astype(y_ref.dtype)
    for b in range(nb):
        y_ref[b] = out[b * rows_b:(b + 1) * rows_b]


def _rwkv7(h_rw, mu, w0, w_up, a0, a_up, g_up, k_k, k_a, r_k, ln_g, ln_b, *, nb=SCAN_BATCHES, nc=SCAN_CHUNKS):
    b, l, _ = h_rw.shape
    w = RW_WIDTH
    nb = min(nb, b)
    rows_b = nc * CHUNK
    assert l % rows_b == 0 and b % nb == 0
    lora = jnp.zeros((LANES, 2 * w), F32)
    lora = lora.at[0:RW_DECAY_LORA, 0:w].set(w_up).at[RW_DECAY_LORA:LANES, w:2 * w].set(a_up)
    hid = jnp.arange(w) // RW_HEAD_DIM
    bd = (hid[:, None] == hid[None, :]).astype(BF16)
    vec = lambda t: t.reshape(1, -1).astype(F32)
    params = [vec(mu), vec(w0), vec(a0), vec(k_k), vec(k_a), vec(r_k), vec(ln_g), vec(ln_b),
              lora.astype(BF16), g_up.astype(BF16), bd, _chunk_tril(nb * nc)]
    return pl.pallas_call(
        functools.partial(_rwkv_kernel, nb=nb, nc=nc),
        grid=(b // nb, l // rows_b),
        in_specs=[pl.BlockSpec((nb, rows_b, RW_COLS), lambda i, j: (i, j, 0))]
        + [_const_spec(p.shape) for p in params],
        out_specs=pl.BlockSpec((nb, rows_b, w), lambda i, j: (i, j, 0)),
        out_shape=jax.ShapeDtypeStruct((b, l, w), BF16),
        scratch_shapes=[pltpu.VMEM((nb, SUBLANES + rows_b, RW_COLS), F32),
                        pltpu.VMEM((nb * RW_HEADS // 2, PAIR, PAIR), F32)],
        compiler_params=_cparams(2),
        name="rwkv7",
    )(h_rw, *params)


def _attn_kernel(slope_ref, lam_ref, ng_ref, q_ref, k_ref, v_ref, o_ref, ka_ref, vt_ref, s0_ref, s1_ref, m_ref,
                 acc_ref, *, tq, tk, lam_init):
    d = DA_HEAD_DIM
    dv = 2 * d
    n_blk = vt_ref.shape[0]
    slope2 = slope_ref[0, 0:1, 0:1] * LOG2E

    lane_k = _iota((tk, LANES), 1)
    bias = slope2 * _iota((tk, LANES), 0).astype(F32)
    hi = bias.astype(BF16).astype(F32)
    mid = (bias - hi).astype(BF16).astype(F32)
    low = bias - hi - mid
    pieces = 0.0
    for n, piece in enumerate((hi, mid, low)):
        pieces = jnp.where(lane_k == n, piece, pieces)
    pieces = pieces.astype(BF16)
    ones_row = (_iota((DA_ONES_ROWS, tk), 0) == 0).astype(BF16)
    for jb in range(n_blk):
        rows = slice(jb * tk, (jb + 1) * tk)
        ka_ref[rows, 0:LANES] = k_ref[0, rows, :]
        ka_ref[rows, LANES:2 * LANES] = pieces
        vt_ref[jb, 0:dv, :] = v_ref[0, rows, :].astype(F32).T.astype(BF16)
        vt_ref[jb, dv:dv + DA_ONES_ROWS, :] = ones_row

    lp = lam_ref[...]
    lam = (jnp.exp(jnp.sum(lp[0:1] * lp[1:2], axis=-1, keepdims=True))
           - jnp.exp(jnp.sum(lp[2:3] * lp[3:4], axis=-1, keepdims=True)) + lam_init)
    lane_q = _iota((tq, LANES), 1)
    lo = lane_q < d
    bias_ones = (lane_q < DA_BIAS_LANES).astype(BF16)

    def q_operand(qi):
        q = q_ref[0, qi * tq:(qi + 1) * tq, :].astype(F32) * (d ** -0.5 * LOG2E)
        return jnp.concatenate([
            jnp.concatenate([jnp.where(lo, q, 0.0).astype(BF16), bias_ones], axis=1),
            jnp.concatenate([jnp.where(lo, 0.0, q).astype(BF16), bias_ones], axis=1)], axis=0)

    def scores(q2, j, dst_ref):
        dst_ref[...] = lax.dot_general(ka_ref[j * tk:(j + 1) * tk, :], q2, (((1,), (1,)), ((), ())),
                                       preferred_element_type=F32)

    def consume(qi, j, src_ref):
        shift = slope2 * float(j * tk - qi * tq)
        for c0 in range(0, 2 * tq, DA_QUERY_GROUP):
            cols = slice(c0, c0 + DA_QUERY_GROUP)
            nk = min(tk, c0 % tq + DA_QUERY_GROUP) if j == qi else tk
            s = src_ref[0:nk, cols]
            if j == qi:
                key = _iota((nk, DA_QUERY_GROUP), 0)
                query = (_iota((nk, DA_QUERY_GROUP), 1) + c0) & (tq - 1)
                s = jnp.where(key <= query, s, -jnp.inf)
            vt = vt_ref[j, :, 0:nk]
            if j == 0:
                m_new = jnp.max(s, axis=0, keepdims=True) + shift
                p = jnp.exp2(s - (m_new - shift)).astype(BF16)
                acc_ref[:, cols] = jnp.dot(vt, p, preferred_element_type=F32)
            else:
                m_old = m_ref[:, cols]
                m_new = jnp.maximum(m_old, jnp.max(s, axis=0, keepdims=True) + shift)
                p = jnp.exp2(s - (m_new - shift)).astype(BF16)
                acc_ref[:, cols] = (jnp.exp2(m_old - m_new) * acc_ref[:, cols]
                                    + jnp.dot(vt, p, preferred_element_type=F32))
            m_ref[:, cols] = m_new

    def finalize(qi):
        acc = acc_ref[...]
        ot = acc[0:dv] / acc[dv:dv + 1]
        ot = ot[:, 0:tq] - lam * ot[:, tq:2 * tq]
        ot = ot * lax.rsqrt(jnp.mean(ot * ot, axis=0, keepdims=True) + 1e-5)
        o_ref[0, qi * tq:(qi + 1) * tq, :] = (ot.T * ng_ref[...] * (1.0 - lam_init)).astype(o_ref.dtype)

    pairs = [(qi, j) for qi in range(n_blk) for j in range(qi + 1)]
    bufs = (s0_ref, s1_ref)
    q2 = {0: q_operand(0)}
    scores(q2[0], 0, bufs[0])
    for idx, (qi, j) in enumerate(pairs):
        if idx + 1 < len(pairs):
            nqi, nj = pairs[idx + 1]
            if nqi not in q2:
                q2[nqi] = q_operand(nqi)
            scores(q2[nqi], nj, bufs[(idx + 1) % 2])
        consume(qi, j, bufs[idx % 2])
        if j == qi:
            finalize(qi)


def _diff_attention(h_da, lam_q1, lam_k1, lam_q2, lam_k2, norm_g, lam_init, *, tq=512):
    b, l, _ = h_da.shape
    tq = min(tq, l)
    tk = tq
    assert l % tq == 0 and tq & (tq - 1) == 0
    hh = DA_HEADS
    slopes = jnp.exp2(-8.0 * jnp.arange(1, hh + 1, dtype=F32) / hh)
    slopes = jnp.broadcast_to(slopes[:, None, None], (hh, SUBLANES, LANES))
    lam_p = jnp.stack([lam_q1, lam_k1, lam_q2, lam_k2]).astype(F32)
    ng = norm_g.reshape(1, 2 * DA_HEAD_DIM).astype(F32)
    seq = lambda col0: pl.BlockSpec((1, l, LANES), lambda i, h: (i, 0, col0 + h))
    return pl.pallas_call(
        functools.partial(_attn_kernel, tq=tq, tk=tk, lam_init=lam_init),
        grid=(b, hh),
        in_specs=[pl.BlockSpec((1, SUBLANES, LANES), lambda i, h: (h, 0, 0)),
                  pl.BlockSpec(lam_p.shape, lambda i, h: (0, 0)),
                  pl.BlockSpec(ng.shape, lambda i, h: (0, 0)),
                  seq(0), seq(hh), seq(2 * hh)],
        out_specs=seq(0),
        out_shape=jax.ShapeDtypeStruct((b, l, DA_WIDTH), BF16),
        scratch_shapes=[pltpu.VMEM((l, 2 * LANES), BF16),
                        pltpu.VMEM((l // tk, LANES + DA_ONES_ROWS, tk), BF16),
                        pltpu.VMEM((tk, 2 * tq), F32), pltpu.VMEM((tk, 2 * tq), F32),
                        pltpu.VMEM((1, 2 * tq), F32), pltpu.VMEM((LANES + DA_ONES_ROWS, 2 * tq), F32)],
        compiler_params=_cparams(2),
        name="diffattn",
    )(slopes, lam_p, ng, h_da, h_da, h_da)


def _softplus(x):
    return jnp.maximum(x, 0.0) + jnp.log(1.0 + jnp.exp(-jnp.abs(x)))


def _gdn_kernel(h_ref, alog_ref, dtb_ref, ng_ref, tril_ref, y_ref, s_ref, *, nb, nc):
    @pl.when(pl.program_id(1) == 0)
    def _():
        s_ref[...] = jnp.zeros_like(s_ref)

    wq = 3 * GD_WIDTH
    dh = GD_HEAD_DIM
    qkv = jnp.concatenate([h_ref[b, :, 0:wq] for b in range(nb)], axis=0)
    ab = jnp.concatenate([h_ref[b, :, wq:wq + GD_AB_PAD] for b in range(nb)], axis=0)
    gfull = -jnp.exp(alog_ref[...]) * _softplus(ab + dtb_ref[...])
    gcum = _split_rhs_dot(tril_ref[...], gfull, 3)
    glast = _rows_of_chunk(gcum, CHUNK - 1)
    beta_full = jax.nn.sigmoid(ab)

    qn = [qkv[:, h * dh:(h + 1) * dh] for h in range(GD_HEADS)]
    kn = [qkv[:, GD_WIDTH + h * dh:GD_WIDTH + (h + 1) * dh] for h in range(GD_HEADS)]
    vh = [qkv[:, 2 * GD_WIDTH + h * dh:2 * GD_WIDTH + (h + 1) * dh] for h in range(GD_HEADS)]

    r0 = _iota((PAIR, PAIR), 0)
    c0 = _iota((PAIR, PAIR), 1)
    same = (r0 >= CHUNK) == (c0 >= CHUNK)
    incl = same & (r0 >= c0)
    strict = same & (r0 > c0)

    n_pairs = GD_HEADS // 2
    units = [(b, ci, p) for b in range(nb) for ci in range(nc) for p in range(n_pairs)]

    def stack(per_head, u, lane_of=None):
        b, ci, p = u
        rr = slice((b * nc + ci) * CHUNK, (b * nc + ci + 1) * CHUNK)
        if lane_of is None:
            return jnp.concatenate([per_head[2 * p][rr], per_head[2 * p + 1][rr]], axis=0)
        return jnp.concatenate([per_head[rr, lane_of + 2 * p:lane_of + 2 * p + 1],
                                per_head[rr, lane_of + 2 * p + 1:lane_of + 2 * p + 2]], axis=0)

    q_s = [stack(qn, u) for u in units]
    k_s = [stack(kn, u) for u in units]
    v_s = [stack(vh, u) for u in units]
    g_s = [stack(gcum, u, 0) for u in units]
    gl_s = [stack(glast, u, 0) for u in units]
    beta_s = [stack(beta_full, u, GD_HEADS) for u in units]

    decay = []
    for gs in g_s:
        g_b = jnp.broadcast_to(gs, (PAIR, PAIR))
        decay.append(jnp.exp(jnp.where(incl, g_b - g_b.T, -jnp.inf)))
    kb = [x * y for x, y in zip(k_s, beta_s)]
    kk = [_mm_nt(jnp.concatenate([x, y], axis=0), z) for x, y, z in zip(kb, q_s, k_s)]
    low = [jnp.where(strict, x[0:PAIR] * dc, 0.0) for x, dc in zip(kk, decay)]
    intra = [x[PAIR:2 * PAIR] * dc for x, dc in zip(kk, decay)]
    inv = _inv_unit_lower_many(low)
    eg = [jnp.exp(gs) for gs in g_s]
    uw = [_mm(i, jnp.concatenate([v * bt, x * e], axis=1)) for i, v, bt, x, e in zip(inv, v_s, beta_s, kb, eg)]
    qg = [x * e for x, e in zip(q_s, eg)]
    kd = [x * jnp.exp(gl - gs) for x, gl, gs in zip(k_s, gl_s, g_s)]

    v_new = {}
    o_state = {}
    chains = [(b, h) for b in range(nb) for h in range(GD_HEADS)]
    state = {bh: s_ref[bh[0] * GD_HEADS + bh[1]] for bh in chains}
    for ci in range(nc):
        ws = {}
        for b, h in chains:
            p, i_h = divmod(h, 2)
            rows = slice(i_h * CHUNK, (i_h + 1) * CHUNK)
            i = units.index((b, ci, p))
            ws[(b, h)] = _mm(jnp.concatenate([uw[i][rows, dh:2 * dh], qg[i][rows]], axis=0), state[(b, h)])
        for b, h in chains:
            p, i_h = divmod(h, 2)
            rows = slice(i_h * CHUNK, (i_h + 1) * CHUNK)
            i = units.index((b, ci, p))
            vn = uw[i][rows, 0:dh] - ws[(b, h)][0:CHUNK]
            v_new[(b, ci, h)] = vn
            o_state[(b, ci, h)] = ws[(b, h)][CHUNK:PAIR]
            state[(b, h)] = state[(b, h)] * jnp.exp(gl_s[i][rows][0:1]) + _mm_tn(kd[i][rows], vn)
    for b, h in chains:
        s_ref[b * GD_HEADS + h] = state[(b, h)]

    for i, (b, ci, p) in enumerate(units):
        vn = jnp.concatenate([v_new[(b, ci, 2 * p)], v_new[(b, ci, 2 * p + 1)]], axis=0)
        o = jnp.concatenate([o_state[(b, ci, 2 * p)], o_state[(b, ci, 2 * p + 1)]], axis=0) + _mm(intra[i], vn)
        for i_h in range(2):
            h = 2 * p + i_h
            oh = o[i_h * CHUNK:(i_h + 1) * CHUNK]
            oh = oh * lax.rsqrt(jnp.mean(oh * oh, axis=-1, keepdims=True) + 1e-6) * ng_ref[...]
            zc = wq + GD_AB_PAD + h * dh
            z = h_ref[b, ci * CHUNK:(ci + 1) * CHUNK, zc:zc + dh]
            y_ref[b, ci * CHUNK:(ci + 1) * CHUNK, h * dh:(h + 1) * dh] = (
                oh * (z * jax.nn.sigmoid(z))).astype(y_ref.dtype)


def _gated_deltanet(h_gd, a_log, dt_bias, norm_g, *, nb=2 * SCAN_BATCHES, nc=SCAN_CHUNKS):
    b, l, _ = h_gd.shape
    nb = min(nb, b)
    rows_b = nc * CHUNK
    assert l % rows_b == 0 and b % nb == 0
    pad = lambda t: jnp.zeros((1, GD_AB_PAD), F32).at[0, 0:GD_HEADS].set(t.astype(F32))
    params = [pad(a_log), pad(dt_bias), norm_g.reshape(1, GD_HEAD_DIM).astype(F32), _chunk_tril(nb * nc)]
    return pl.pallas_call(
        functools.partial(_gdn_kernel, nb=nb, nc=nc),
        grid=(b // nb, l // rows_b),
        in_specs=[pl.BlockSpec((nb, rows_b, GD_IN), lambda i, j: (i, j, 0))]
        + [_const_spec(p.shape) for p in params],
        out_specs=pl.BlockSpec((nb, rows_b, GD_WIDTH), lambda i, j: (i, j, 0)),
        out_shape=jax.ShapeDtypeStruct((b, l, GD_WIDTH), BF16),
        scratch_shapes=[pltpu.VMEM((nb * GD_HEADS, GD_HEAD_DIM, GD_HEAD_DIM), F32)],
        compiler_params=_cparams(2),
        name="gdn",
    )(h_gd, *params)


def _merge_ln_kernel(x_ref, xb_ref, ya_ref, yb_ref, yc_ref, wgate_ref, wbr_ref, wout_ref, g_ref, b_ref,
                     y_ref, y16_ref):
    for r0 in range(0, x_ref.shape[0], MERGE_ROWS):
        rows = slice(r0, r0 + MERGE_ROWS)
        xb = xb_ref[rows, :]
        merged = None
        for n, br_ref in enumerate((ya_ref, yb_ref, yc_ref)):
            gate = jax.nn.sigmoid(jnp.dot(xb, wgate_ref[n], preferred_element_type=F32))
            term = gate * jnp.dot(br_ref[rows, :], wbr_ref[n], preferred_element_type=F32)
            merged = term if merged is None else merged + term
        mix = jnp.dot(merged.astype(BF16), wout_ref[...], preferred_element_type=F32)
        y = _layernorm(ALPHA * x_ref[rows, :] + mix, g_ref[...], b_ref[...])
        y_ref[rows, :] = y
        y16_ref[rows, :] = y.astype(BF16)


def _merge_ln(x, xb, ya, yb, yc, w_gate, w_branch, w_out, g, b, *, tm=4 * MERGE_ROWS):
    t, d = x.shape
    bw = ya.shape[-1]
    assert t % tm == 0 and tm % MERGE_ROWS == 0
    row = pl.BlockSpec((tm, d), lambda i: (i, 0))
    brow = pl.BlockSpec((tm, bw), lambda i: (i, 0))
    return pl.pallas_call(
        _merge_ln_kernel,
        grid=(t // tm,),
        in_specs=[row, row, brow, brow, brow, _const_spec((N_BRANCH, d, d)),
                  _const_spec((N_BRANCH, bw, d)), _const_spec((d, d)), _const_spec((1, d)), _const_spec((1, d))],
        out_specs=[row, row],
        out_shape=[jax.ShapeDtypeStruct((t, d), F32), jax.ShapeDtypeStruct((t, d), BF16)],
        compiler_params=_cparams(1),
        name="merge_ln",
    )(x, xb, ya, yb, yc, w_gate, w_branch, w_out, g.reshape(1, d), b.reshape(1, d))


def _split_mix_w_in(w_in):
    d = w_in.shape[0]
    o1 = RW_COLS
    o2 = o1 + DA_COLS
    o3 = o2 + GD_COLS
    gq = o2 + 3 * GD_WIDTH
    ab = jnp.zeros((d, GD_AB_PAD), w_in.dtype).at[:, 0:2 * GD_HEADS].set(w_in[:, gq:gq + 2 * GD_HEADS])
    w_cat = jnp.concatenate([w_in[:, 0:o2], w_in[:, o2:gq], ab, w_in[:, gq + 2 * GD_HEADS:o3]], axis=1)
    w_gate = w_in[:, o3:].reshape(d, N_BRANCH, d).transpose(1, 0, 2)
    return w_cat.astype(BF16), w_gate.astype(BF16)


def kernel(x, ffn1_w_in, ffn1_w_out, ln1_g, ln1_b, mix_w_in, rw_shift_mu, rw_w0, rw_w_up, rw_a0, rw_a_up, rw_g_up, rw_k_k, rw_k_a, rw_r_k, rw_ln_g, rw_ln_b, da_lam_q1, da_lam_k1, da_lam_q2, da_lam_k2, da_norm_g, gd_conv_w, gd_a_log, gd_dt_bias, gd_norm_g, mix_w_branch, mix_w_out, ln2_g, ln2_b, ffn2_w_in, ffn2_w_out, ln3_g, ln3_b):
    b, l, d = x.shape
    t = b * l
    xf = x.reshape(t, d)
    xb = None
    for i in range(DEPTH):
        xf, xb = _ffn_ln(xf, xb, ffn1_w_in[i], ffn1_w_out[i], ln1_g[i], ln1_b[i])
        w_cat, w_gate = _split_mix_w_in(mix_w_in[i])
        h_rw, h_da, h_gd = _mix_in(xb, w_cat, gd_conv_w[i], l)
        ya = _rwkv7(h_rw.reshape(b, l, RW_COLS), rw_shift_mu[i], rw_w0[i], rw_w_up[i], rw_a0[i], rw_a_up[i],
                    rw_g_up[i], rw_k_k[i], rw_k_a[i], rw_r_k[i].reshape(-1), rw_ln_g[i], rw_ln_b[i])
        lam_init = 0.8 - 0.6 * math.exp(-0.3 * i)
        yb = _diff_attention(h_da.reshape(b, l, DA_COLS), da_lam_q1[i], da_lam_k1[i], da_lam_q2[i],
                             da_lam_k2[i], da_norm_g[i], lam_init)
        yc = _gated_deltanet(h_gd.reshape(b, l, GD_IN), gd_a_log[i], gd_dt_bias[i], gd_norm_g[i])
        xf, xb = _merge_ln(xf, xb, ya.reshape(t, RW_WIDTH), yb.reshape(t, DA_WIDTH), yc.reshape(t, GD_WIDTH),
                           w_gate, mix_w_branch[i].astype(BF16), mix_w_out[i].astype(BF16), ln2_g[i], ln2_b[i])
        xf, xb = _ffn_ln(xf, xb, ffn2_w_in[i], ffn2_w_out[i], ln3_g[i], ln3_b[i])
    return xf.reshape(b, l, d)
```
